```python
import math
import jax, jax.numpy as jnp
from jax import lax
import numpy as np

D_MODEL = 4096
BATCH = 2
SEQ = 8192
DEPTH = 4

GRID_W = 64
CTX_LEN = 256
HEAD_SIZE = 64
N_HEADS = D_MODEL // HEAD_SIZE
R_DECAY = max(32, int(round(1.8 * D_MODEL ** 0.5 / 32)) * 32)
R_AAA = max(32, int(round(1.8 * D_MODEL ** 0.5 / 32)) * 32)
R_MV = max(32, int(round(1.3 * D_MODEL ** 0.5 / 32)) * 32)
R_GATE = max(32, int(round(0.6 * D_MODEL ** 0.8 / 32)) * 32)
N_MIX = 6
S5_GROUP = 16
S5_GROUPS = D_MODEL // S5_GROUP
S5_STATE = 64
S5_BLOCK = 32
S5_NBLK = S5_GROUPS // S5_BLOCK
D_FF = 4 * D_MODEL
R_ADA = 256
N_MOD = 6
N_DIR = 2
N_RWKV = (DEPTH + 1) // 2
N_S5 = DEPTH // 2
ALPHA = (2 * DEPTH) ** 0.25
BETA = (8 * DEPTH) ** -0.25
LN_EPS = 1e-6
GN_EPS = 64e-5

kernel_name = "hybrid_rwkv7_s5_deepnorm_flow_backbone"


def layer_norm(x, g, b):
    xf = x.astype(jnp.float32)
    mu = xf.mean(-1, keepdims=True)
    var = jnp.mean(jnp.square(xf - mu), -1, keepdims=True)
    return ((xf - mu) * lax.rsqrt(var + LN_EPS) * g + b).astype(x.dtype)


def adaln(cvec, down, up, bias):
    return (jax.nn.silu(cvec) @ down) @ up + bias


def modulate(x, shift, scale):
    return x * (1.0 + scale) + shift


def sq_relu_mlp(h, w1, w2):
    return jnp.square(jax.nn.relu(h @ w1)) @ w2


def qshift_grid(h):
    b, L, d = h.shape
    rows = L // GRID_W
    g = h.reshape(b, rows, GRID_W, d)
    q = d // 4
    gw = jnp.pad(g[..., :2 * q], ((0, 0), (0, 0), (1, 1), (0, 0)))
    gh = jnp.pad(g[..., 2 * q:], ((0, 0), (1, 1), (0, 0), (0, 0)))
    out = jnp.concatenate([gw[:, :, :-2, :q], gw[:, :, 2:, q:], gh[:, :-2, :, :q], gh[:, 2:, :, q:]], axis=-1)
    return out.reshape(b, L, d)


def shift_seq(h):
    half = h.shape[-1] // 2
    hp = jnp.pad(h, ((0, 0), (1, 1), (0, 0)))
    return jnp.concatenate([hp[:, :-2, :half], hp[:, 2:, half:]], axis=-1)


def _heads(t):
    return t.reshape(t.shape[0], t.shape[1], N_HEADS, HEAD_SIZE)


def rwkv7_features(h, h_shift, v_first, p):
    f32 = jnp.float32
    xx = h_shift - h
    mix = p["mix"]

    def lerp(j):
        return h + xx * mix[j]

    r = lerp(0) @ p["wr"]
    k = lerp(2) @ p["wk"]
    xv = lerp(3)
    v = xv @ p["wv"]
    if v_first is not None:
        v = v + (v_first - v) * jax.nn.sigmoid(p["v0"] + (xv @ p["v1"]) @ p["v2"])
    g = jax.nn.sigmoid(lerp(5) @ p["g1"]) @ p["g2"]
    kk = _heads(k * p["k_k"]).astype(f32)
    kk = kk / jnp.maximum(jnp.linalg.norm(kk, axis=-1, keepdims=True), 1e-12)
    xw, xa = lerp(1), lerp(4)
    dirs = []
    for d in range(N_DIR):
        w = -jax.nn.softplus(-(p["w0"][d] + jnp.tanh(xw @ p["w1"][d]) @ p["w2"][d])) - 0.5
        a = jax.nn.sigmoid(p["a0"][d] + (xa @ p["a1"][d]) @ p["a2"][d])
        k_d = k * (1.0 + (a - 1.0) * p["k_a"])
        dirs.append((jnp.exp(-jnp.exp(_heads(w).astype(f32))), _heads(k_d).astype(f32), _heads(a).astype(f32)))
    feats = {"r": _heads(r).astype(f32), "v": _heads(v).astype(f32), "kk": kk, "g": g, "dirs": dirs}
    return feats, v


def rwkv7_scan(state0, r, decay, k, a, b, v, reverse):
    def step(S, inp):
        r_t, w_t, k_t, a_t, b_t, v_t = inp
        sa = jnp.einsum("bhvk,bhk->bhv", S, a_t)
        S = S * w_t[:, :, None, :] + sa[..., None] * b_t[:, :, None, :] + v_t[..., None] * k_t[:, :, None, :]
        return S, jnp.einsum("bhvk,bhk->bhv", S, r_t)

    xs = tuple(jnp.moveaxis(t, 1, 0) for t in (r, decay, k, a, b, v))
    S, ys = lax.scan(step, state0, xs, reverse=reverse)
    return jnp.moveaxis(ys, 0, 1), S


def rwkv7_readout(f, ys, p, dtype):
    b, L = ys.shape[:2]
    mu = ys.mean(-1, keepdims=True)
    var = jnp.mean(jnp.square(ys - mu), -1, keepdims=True)
    y = ((ys - mu) * lax.rsqrt(var + GN_EPS)).reshape(b, L, D_MODEL) * p["lnx_g"] + p["lnx_b"]
    bonus = sum((f["r"] * kd * p["r_k"]).sum(-1, keepdims=True) for _, kd, _ in f["dirs"]) * f["v"]
    y = y + bonus.reshape(b, L, D_MODEL)
    return (y.astype(dtype) * f["g"]) @ p["wo"]


def rwkv7_mixer(hc, hx, vf_c, vf_x, p, with_ctx_out):
    fc, v_c = rwkv7_features(hc, shift_seq(hc), vf_c, p)
    fx, v_x = rwkv7_features(hx, qshift_grid(hx), vf_x, p)
    ys_c = 0.0
    ys_x = 0.0
    for d, rev in enumerate((False, True)):
        dec_c, k_c, a_c = fc["dirs"][d]
        s0 = jnp.zeros((hc.shape[0], N_HEADS, HEAD_SIZE, HEAD_SIZE), jnp.float32)
        y_c, s_c = rwkv7_scan(s0, fc["r"], dec_c, k_c, -fc["kk"], fc["kk"] * a_c, fc["v"], rev)
        dec_x, k_x, a_x = fx["dirs"][d]
        y_x, _ = rwkv7_scan(s_c, fx["r"], dec_x, k_x, -fx["kk"], fx["kk"] * a_x, fx["v"], rev)
        ys_c = ys_c + y_c
        ys_x = ys_x + y_x
    out_x = rwkv7_readout(fx, ys_x, p, hx.dtype)
    out_c = rwkv7_readout(fc, ys_c, p, hc.dtype) if with_ctx_out else None
    return out_c, out_x, v_c, v_x


def _to_blocks(u):
    b, L, _ = u.shape
    return u.astype(jnp.float32).reshape(b, L, S5_NBLK, S5_BLOCK, S5_GROUP).transpose(2, 1, 0, 3, 4)


def _from_blocks(y):
    L, b = y.shape[1], y.shape[2]
    return y.transpose(2, 1, 0, 3, 4).reshape(b, L, D_MODEL)


def _param_blocks(t):
    return jnp.moveaxis(t.reshape((N_DIR, S5_NBLK, S5_BLOCK) + t.shape[2:]), 1, 0)


def _diag_combine(e1, e2):
    a1, b1 = e1
    a2, b2 = e2
    return a1 * a2, a2 * b1 + b2


def _diag_scan(lam_bar, bu, h0, reverse):
    if reverse:
        bu = jnp.flip(bu, 0)
    if h0 is not None:
        bu = bu.at[0].add(lam_bar * h0)
    a = jnp.broadcast_to(lam_bar, bu.shape)
    _, h = lax.associative_scan(_diag_combine, (a, bu), axis=0)
    last = h[-1]
    if reverse:
        h = jnp.flip(h, 0)
    return h, last


def s5_mixer(hc, hx, p, with_ctx_out):
    f32 = jnp.float32
    lam = lax.complex(p["lam_re"].astype(f32), p["lam_im"].astype(f32))
    dt = jnp.exp(p["log_dt"].astype(f32))[..., None]
    lam_bar = jnp.exp(lam * dt)
    b_bar = ((lam_bar - 1.0) / lam)[..., None] * lax.complex(p["b_re"].astype(f32), p["b_im"].astype(f32))
    c_mat = lax.complex(p["c_re"].astype(f32), p["c_im"].astype(f32))
    blocks = (_to_blocks(hc), _to_blocks(hx), _param_blocks(lam_bar), _param_blocks(b_bar), _param_blocks(c_mat))

    def block(args):
        uc, ux, lb, bb, cm = args
        yx = 0.0
        yc = 0.0
        for d, rev in enumerate((False, True)):
            hs_c, last_c = _diag_scan(lb[d], jnp.einsum("gpi,lbgi->lbgp", bb[d], uc), None, rev)
            hs_x, _ = _diag_scan(lb[d], jnp.einsum("gpi,lbgi->lbgp", bb[d], ux), last_c, rev)
            yx = yx + jnp.einsum("gip,lbgp->lbgi", cm[d], hs_x).real
            if with_ctx_out:
                yc = yc + jnp.einsum("gip,lbgp->lbgi", cm[d], hs_c).real
        return yx, (yc if with_ctx_out else None)

    yx, yc = lax.map(block, blocks)

    def finish(yb, u):
        y = _from_blocks(yb) + p["d"] * u
        z = jax.nn.gelu(y) @ p["glu_w"] + p["glu_b"]
        return (z[..., :D_MODEL] * jax.nn.sigmoid(z[..., D_MODEL:])).astype(u.dtype)

    out_x = finish(yx, hx)
    out_c = finish(yc, hc) if with_ctx_out else None
    return out_c, out_x


def setup_inputs(seed: int = 0) -> dict:
    key = jax.random.key(seed)
    keys = iter(jax.random.split(key, 48))

    def nrm(shape, std):
        return std * jax.random.normal(next(keys), shape, jnp.float32)

    def uni(shape, lo, hi):
        return jax.random.uniform(next(keys), shape, jnp.float32, lo, hi)

    D = D_MODEL
    inv = D ** -0.5
    n_idx = jnp.arange(S5_STATE, dtype=jnp.float32)
    return {
        "x": nrm((BATCH, SEQ, D), 1.0),
        "c": nrm((BATCH, D), 1.0),
        "ctx": nrm((BATCH, CTX_LEN, D), 1.0),
        "c_ctx": nrm((D,), 1.0),
        "ada_down": nrm((DEPTH, D, R_ADA), inv),
        "ada_up": nrm((DEPTH, R_ADA, N_MOD * D), 0.5 * R_ADA ** -0.5),
        "ada_bias": nrm((DEPTH, N_MOD * D), 0.02),
        "ln_g": 1.0 + nrm((DEPTH, 2, D), 0.02),
        "ln_b": nrm((DEPTH, 2, D), 0.02),
        "rw_mix": uni((N_RWKV, N_MIX, D), 0.0, 1.0),
        "rw_wr": nrm((N_RWKV, D, D), inv),
        "rw_wk": nrm((N_RWKV, D, D), inv),
        "rw_wv": nrm((N_RWKV, D, D), BETA * inv),
        "rw_wo": nrm((N_RWKV, D, D), BETA * inv),
        "rw_w0": uni((N_RWKV, N_DIR, D), -6.5, -1.5),
        "rw_w1": nrm((N_RWKV, N_DIR, D, R_DECAY), inv),
        "rw_w2": nrm((N_RWKV, N_DIR, R_DECAY, D), 0.1 * R_DECAY ** -0.5),
        "rw_a0": nrm((N_RWKV, N_DIR, D), 0.1),
        "rw_a1": nrm((N_RWKV, N_DIR, D, R_AAA), inv),
        "rw_a2": nrm((N_RWKV, N_DIR, R_AAA, D), 0.1 * R_AAA ** -0.5),
        "rw_v0": 1.0 + nrm((N_RWKV - 1, D), 0.1),
        "rw_v1": nrm((N_RWKV - 1, D, R_MV), inv),
        "rw_v2": nrm((N_RWKV - 1, R_MV, D), 0.1 * R_MV ** -0.5),
        "rw_g1": nrm((N_RWKV, D, R_GATE), inv),
        "rw_g2": nrm((N_RWKV, R_GATE, D), R_GATE ** -0.5),
        "rw_kk": 0.85 + nrm((N_RWKV, D), 0.02),
        "rw_ka": 1.0 + nrm((N_RWKV, D), 0.02),
        "rw_rk": nrm((N_RWKV, N_HEADS, HEAD_SIZE), 0.1),
        "rw_lnx_g": 1.0 + nrm((N_RWKV, D), 0.02),
        "rw_lnx_b": nrm((N_RWKV, D), 0.02),
        "s5_lam_re": -0.5 + nrm((N_S5, N_DIR, S5_GROUPS, S5_STATE), 0.01),
        "s5_lam_im": math.pi * n_idx + nrm((N_S5, N_DIR, S5_GROUPS, S5_STATE), 0.01),
        "s5_log_dt": uni((N_S5, N_DIR, S5_GROUPS), math.log(1e-3), math.log(1e-1)),
        "s5_b_re": nrm((N_S5, N_DIR, S5_GROUPS, S5_STATE, S5_GROUP), (2 * S5_GROUP) ** -0.5),
        "s5_b_im": nrm((N_S5, N_DIR, S5_GROUPS, S5_STATE, S5_GROUP), (2 * S5_GROUP) ** -0.5),
        "s5_c_re": nrm((N_S5, N_DIR, S5_GROUPS, S5_GROUP, S5_STATE), (2 * S5_STATE) ** -0.5),
        "s5_c_im": nrm((N_S5, N_DIR, S5_GROUPS, S5_GROUP, S5_STATE), (2 * S5_STATE) ** -0.5),
        "s5_d": nrm((N_S5, D), 1.0),
        "s5_glu_w": nrm((N_S5, D, 2 * D), BETA * inv),
        "s5_glu_b": nrm((N_S5, 2 * D), 0.02),
        "mlp_w1": nrm((DEPTH, D, D_FF), inv),
        "mlp_w2": nrm((DEPTH, D_FF, D), BETA * D_FF ** -0.5),
    }


def reference(x, c, ctx, c_ctx, ada_down, ada_up, ada_bias, ln_g, ln_b,
              rw_mix, rw_wr, rw_wk, rw_wv, rw_wo, rw_w0, rw_w1, rw_w2, rw_a0, rw_a1, rw_a2,
              rw_v0, rw_v1, rw_v2, rw_g1, rw_g2, rw_kk, rw_ka, rw_rk, rw_lnx_g, rw_lnx_b,
              s5_lam_re, s5_lam_im, s5_log_dt, s5_b_re, s5_b_im, s5_c_re, s5_c_im, s5_d, s5_glu_w, s5_glu_b,
              mlp_w1, mlp_w2):
    xl, xc = x, ctx
    vf_c = None
    vf_x = None
    for i in range(DEPTH):
        last = i == DEPTH - 1
        sh1_x, sc1_x, gt1_x, sh2_x, sc2_x, gt2_x = jnp.split(
            adaln(c, ada_down[i], ada_up[i], ada_bias[i])[:, None, :], N_MOD, axis=-1)
        sh1_c, sc1_c, gt1_c, sh2_c, sc2_c, gt2_c = jnp.split(
            adaln(c_ctx, ada_down[i], ada_up[i], ada_bias[i])[None, None, :], N_MOD, axis=-1)
        hx = modulate(xl, sh1_x, sc1_x)
        hc = modulate(xc, sh1_c, sc1_c)
        j = i // 2
        if i % 2 == 0:
            p = {"mix": rw_mix[j], "wr": rw_wr[j], "wk": rw_wk[j], "wv": rw_wv[j], "wo": rw_wo[j],
                 "w0": rw_w0[j], "w1": rw_w1[j], "w2": rw_w2[j], "a0": rw_a0[j], "a1": rw_a1[j],
                 "a2": rw_a2[j], "g1": rw_g1[j], "g2": rw_g2[j], "k_k": rw_kk[j], "k_a": rw_ka[j],
                 "r_k": rw_rk[j], "lnx_g": rw_lnx_g[j], "lnx_b": rw_lnx_b[j]}
            if j > 0:
                p["v0"] = rw_v0[j - 1]
                p["v1"] = rw_v1[j - 1]
                p["v2"] = rw_v2[j - 1]
            yc, yx, v_c, v_x = rwkv7_mixer(hc, hx, vf_c if j > 0 else None, vf_x if j > 0 else None,
                                           p, not last)
            if j == 0:
                vf_c, vf_x = v_c, v_x
        else:
            p = {"lam_re": s5_lam_re[j], "lam_im": s5_lam_im[j], "log_dt": s5_log_dt[j],
                 "b_re": s5_b_re[j], "b_im": s5_b_im[j], "c_re": s5_c_re[j], "c_im": s5_c_im[j],
                 "d": s5_d[j], "glu_w": s5_glu_w[j], "glu_b": s5_glu_b[j]}
            yc, yx = s5_mixer(hc, hx, p, not last)
        xl = layer_norm(ALPHA * xl + gt1_x * yx, ln_g[i, 0], ln_b[i, 0])
        xl = layer_norm(ALPHA * xl + gt2_x * sq_relu_mlp(modulate(xl, sh2_x, sc2_x), mlp_w1[i], mlp_w2[i]),
                        ln_g[i, 1], ln_b[i, 1])
        if not last:
            xc = layer_norm(ALPHA * xc + gt1_c * yc, ln_g[i, 0], ln_b[i, 0])
            xc = layer_norm(ALPHA * xc + gt2_c * sq_relu_mlp(modulate(xc, sh2_c, sc2_c), mlp_w1[i], mlp_w2[i]),
                            ln_g[i, 1], ln_b[i, 1])
    return xl
```

```python
import functools
import math

import jax
import jax.numpy as jnp
from jax import lax
from jax.experimental import pallas as pl
from jax.experimental.pallas import tpu as pltpu

F32 = jnp.float32
BF16 = jnp.bfloat16

GRID_W = 64
HEAD = 64
S5_GROUP = 16
S5_STATE = 64
LN_EPS = 1e-6
GN_EPS = 64e-5
LANE = 128
SUBLANE = 8
VMEM_LIMIT = 56 * 1024 * 1024


def _cparams(*sem):
    return pltpu.CompilerParams(dimension_semantics=sem, vmem_limit_bytes=VMEM_LIMIT)


def _pick(n, prefs):
    for p in prefs:
        if n % p == 0:
            return p
    return n


def _pad_to(a, axis, mult):
    n = a.shape[axis]
    r = (-n) % mult
    if r == 0:
        return a
    pad = [(0, 0)] * a.ndim
    pad[axis] = (0, r)
    return jnp.pad(a, pad)


def _mm_kernel(*refs, nk, act, has_bias, glu):
    refs = list(refs)
    x_ref = refs.pop(0)
    w_refs = [refs.pop(0) for _ in range(2 if glu else 1)]
    b_refs = [refs.pop(0) for _ in range((2 if glu else 1) if has_bias else 0)]
    o_ref = refs.pop(0)
    acc_refs = refs
    k = pl.program_id(2)

    def finish(zs):
        if has_bias:
            zs = [z + b[...] for z, b in zip(zs, b_refs)]
        if glu:
            z = zs[0] * jax.nn.sigmoid(zs[1])
        else:
            z = zs[0]
            if act == "tanh":
                z = jnp.tanh(z)
            elif act == "sigmoid":
                z = jax.nn.sigmoid(z)
            elif act == "relu2":
                z = jnp.square(jnp.maximum(z, 0.0))
        o_ref[...] = z.astype(o_ref.dtype)

    x = x_ref[...]
    parts = [jnp.dot(x, w[...], preferred_element_type=F32) for w in w_refs]
    if nk == 1:
        finish(parts)
        return

    @pl.when(k == 0)
    def _():
        for a, p in zip(acc_refs, parts):
            a[...] = p

    @pl.when(jnp.logical_and(k > 0, k < nk - 1))
    def _():
        for a, p in zip(acc_refs, parts):
            a[...] += p

    @pl.when(k == nk - 1)
    def _():
        finish([a[...] + p for a, p in zip(acc_refs, parts)])


def _mm(x, w, bias=None, act=None, out_dtype=F32, glu=False, tm=None, tn=None, tk=None, name="mm"):
    m, kdim = x.shape
    n = w.shape[1] // (2 if glu else 1)
    tm = tm or _pick(m, (1536, 1024, 768, 512, 256, 128, 64, 32, 16, 8))
    tn = tn or _pick(n, (1024, 512, 256, 128))
    tk = tk or _pick(kdim, (1024, 512, 256, 128))
    nk = kdim // tk
    nj = n // tn
    in_specs = [pl.BlockSpec((tm, tk), lambda i, j, k: (i, k)),
                pl.BlockSpec((tk, tn), lambda i, j, k: (k, j))]
    args = [x, w]
    if glu:
        in_specs.append(pl.BlockSpec((tk, tn), lambda i, j, k: (k, j + nj)))
        args.append(w)
    if bias is not None:
        b2 = bias.reshape(1, -1).astype(F32)
        in_specs.append(pl.BlockSpec((1, tn), lambda i, j, k: (0, j)))
        args.append(b2)
        if glu:
            in_specs.append(pl.BlockSpec((1, tn), lambda i, j, k: (0, j + nj)))
            args.append(b2)
    scratch = [] if nk == 1 else [pltpu.VMEM((tm, tn), F32) for _ in range(2 if glu else 1)]
    return pl.pallas_call(
        functools.partial(_mm_kernel, nk=nk, act=act, has_bias=bias is not None, glu=glu),
        grid=(m // tm, nj, nk),
        in_specs=in_specs,
        out_specs=pl.BlockSpec((tm, tn), lambda i, j, k: (i, j)),
        out_shape=jax.ShapeDtypeStruct((m, n), out_dtype),
        scratch_shapes=scratch,
        compiler_params=_cparams("parallel", "parallel", "arbitrary"),
        name=name,
    )(*args)


def _head_sum(x, seg_ref):
    cw = seg_ref.shape[0]
    outs = []
    for s in range(x.shape[1] // cw):
        outs.append(jnp.dot(x[:, s * cw:(s + 1) * cw], seg_ref[...],
                            preferred_element_type=F32, precision=lax.Precision.HIGHEST))
    return outs[0] if len(outs) == 1 else jnp.concatenate(outs, axis=1)


def _seg_ones(cw):
    idx = jnp.arange(cw) // HEAD
    return (idx[:, None] == idx[None, :]).astype(F32)


def _prep_kernel(xp_ref, xc_ref, xn_ref, mod_ref, mix_ref, *rest, tr, n_lat_tiles, l_img, l_ctx):
    o_refs = rest[:6]
    hbuf, sbuf = rest[6], rest[7]
    i = pl.program_id(0)
    j = pl.program_id(1)
    sh = mod_ref[0, 0:1, :]
    sc = 1.0 + mod_ref[0, 1:2, :]
    hbuf[0:GRID_W, :] = xp_ref[...] * sc + sh
    hbuf[GRID_W:GRID_W + tr, :] = xc_ref[...] * sc + sh
    hbuf[GRID_W + tr:GRID_W + tr + GRID_W, :] = xn_ref[...] * sc + sh
    row = lax.broadcasted_iota(jnp.int32, (tr, 1), 0)
    t_lat = (i * tr + row) % l_img
    is_lat = i < n_lat_tiles

    def shifted(s, keep):
        sbuf[...] = jnp.where(keep, hbuf[GRID_W + s:GRID_W + s + tr, :], 0.0)

    @pl.when(jnp.logical_and(is_lat, j == 0))
    def _():
        shifted(-1, t_lat % GRID_W != 0)

    @pl.when(jnp.logical_and(is_lat, j == 1))
    def _():
        shifted(1, t_lat % GRID_W != GRID_W - 1)

    @pl.when(jnp.logical_and(is_lat, j == 2))
    def _():
        shifted(-GRID_W, t_lat >= GRID_W)

    @pl.when(jnp.logical_and(is_lat, j == 3))
    def _():
        shifted(GRID_W, t_lat < l_img - GRID_W)

    t_ctx = row % l_ctx

    @pl.when(jnp.logical_and(jnp.logical_not(is_lat), j < 2))
    def _():
        shifted(-1, t_ctx != 0)

    @pl.when(jnp.logical_and(jnp.logical_not(is_lat), j >= 2))
    def _():
        shifted(1, t_ctx != l_ctx - 1)

    h = hbuf[GRID_W:GRID_W + tr, :]
    xx = sbuf[...] - h
    for m in range(6):
        o_refs[m][...] = (h + xx * mix_ref[m:m + 1, :]).astype(BF16)


def _rwkv_prep(x, mod, mix, dims):
    r, d = x.shape
    tr = dims["tr"]
    dc = d // 4
    nb = tr // GRID_W
    last = r // GRID_W - 1
    kern = functools.partial(_prep_kernel, tr=tr, n_lat_tiles=dims["n_lat"] // tr,
                             l_img=dims["L"], l_ctx=dims["Lc"])
    gmap = dims["gmap"]
    outs = pl.pallas_call(
        kern,
        grid=(r // tr, 4),
        in_specs=[
            pl.BlockSpec((GRID_W, dc), lambda i, j: (jnp.maximum(i * nb - 1, 0), j)),
            pl.BlockSpec((tr, dc), lambda i, j: (i, j)),
            pl.BlockSpec((GRID_W, dc), lambda i, j: (jnp.minimum(i * nb + nb, last), j)),
            pl.BlockSpec((1, 6, dc), lambda i, j: (gmap(i), 0, j)),
            pl.BlockSpec((6, dc), lambda i, j: (0, j)),
        ],
        out_specs=[pl.BlockSpec((tr, dc), lambda i, j: (i, j)) for _ in range(6)],
        out_shape=[jax.ShapeDtypeStruct((r, d), BF16) for _ in range(6)],
        scratch_shapes=[pltpu.VMEM((tr + 2 * GRID_W, dc), F32), pltpu.VMEM((tr, dc), F32)],
        compiler_params=_cparams("parallel", "parallel"),
        name="rwkv_prep",
    )(x, x, x, mod, mix)
    return outs


def _feat_kernel(*refs, has_vf, rw, ra):
    refs = list(refs)
    r_ref, k_ref, v_ref, tw_ref, ax_ref = [refs.pop(0) for _ in range(5)]
    w2_ref, a2_ref, w0_ref, a0_ref, vec_ref, seg_ref = [refs.pop(0) for _ in range(6)]
    if has_vf:
        vf_ref, xv_ref, v2_ref, v0_ref = [refs.pop(0) for _ in range(4)]
    kk_o, dec0_o, dec1_o, kd0_o, kd1_o, b0_o, b1_o, bonus_o = refs[:8]
    v_o = refs[8] if has_vf else None

    k = k_ref[...]
    r = r_ref[...]
    v = v_ref[...]
    k_k = vec_ref[0:1, :]
    k_a = vec_ref[1:2, :]
    r_k = vec_ref[2:3, :]
    kraw = k * k_k
    nrm = jnp.sqrt(_head_sum(kraw * kraw, seg_ref))
    kk = kraw / jnp.maximum(nrm, 1e-12)
    kk_o[...] = kk
    if has_vf:
        vl = v0_ref[...] + jnp.dot(xv_ref[...], v2_ref[...], preferred_element_type=F32)
        v = v + (vf_ref[...] - v) * jax.nn.sigmoid(vl)
        v_o[...] = v
    kd_sum = None
    for d, (dec_o, kd_o, b_o) in enumerate(((dec0_o, kd0_o, b0_o), (dec1_o, kd1_o, b1_o))):
        wl = w0_ref[d:d + 1, :] + jnp.dot(tw_ref[:, d * rw:(d + 1) * rw], w2_ref[d],
                                          preferred_element_type=F32)
        nwl = -wl
        softplus = jnp.maximum(nwl, 0.0) + jnp.log(1.0 + jnp.exp(-jnp.abs(nwl)))
        wlog = -softplus - 0.5
        dec_o[...] = jnp.exp(-jnp.exp(wlog))
        al = a0_ref[d:d + 1, :] + jnp.dot(ax_ref[:, d * ra:(d + 1) * ra], a2_ref[d],
                                          preferred_element_type=F32)
        a = jax.nn.sigmoid(al)
        kd = k * (1.0 + (a - 1.0) * k_a)
        kd_o[...] = kd
        b_o[...] = kk * a
        kd_sum = kd if kd_sum is None else kd_sum + kd
    bonus_o[...] = _head_sum(r * kd_sum * r_k, seg_ref) * v


def _rwkv_features(r, k, v, tw, ax, p, vf, xv, dims):
    rows, d = r.shape
    tr = _pick(rows, (256, 128, 64, 32, 16, 8))
    cb = _pick(d, (1024, 512, 256, 128))
    cw = min(cb, 256)
    has_vf = vf is not None
    rw = p["w2"].shape[1]
    ra = p["a2"].shape[1]
    tile = pl.BlockSpec((tr, cb), lambda i, j: (i, j))

    def full_rows(a):
        return pl.BlockSpec((tr, a.shape[1]), lambda i, j: (i, 0))

    vec = jnp.stack([p["k_k"], p["k_a"], p["r_k"]]).astype(F32)
    in_specs = [tile, tile, tile, full_rows(tw), full_rows(ax),
                pl.BlockSpec((2, rw, cb), lambda i, j: (0, 0, j)),
                pl.BlockSpec((2, ra, cb), lambda i, j: (0, 0, j)),
                pl.BlockSpec((2, cb), lambda i, j: (0, j)),
                pl.BlockSpec((2, cb), lambda i, j: (0, j)),
                pl.BlockSpec((3, cb), lambda i, j: (0, j)),
                pl.BlockSpec((cw, cw), lambda i, j: (0, 0))]
    args = [r, k, v, tw, ax, p["w2"], p["a2"], p["w0"], p["a0"], vec, _seg_ones(cw)]
    n_out = 8
    if has_vf:
        in_specs += [tile, full_rows(xv),
                     pl.BlockSpec((p["v2"].shape[0], cb), lambda i, j: (0, j)),
                     pl.BlockSpec((1, cb), lambda i, j: (0, j))]
        args += [vf, xv, p["v2"], p["v0"].reshape(1, -1)]
        n_out = 9
    outs = pl.pallas_call(
        functools.partial(_feat_kernel, has_vf=has_vf, rw=rw, ra=ra),
        grid=(rows // tr, d // cb),
        in_specs=in_specs,
        out_specs=[tile] * n_out,
        out_shape=[jax.ShapeDtypeStruct((rows, d), F32)] * n_out,
        compiler_params=_cparams("parallel", "parallel"),
        name="rwkv_features",
    )(*args)
    return outs


def _scan_kernel(*refs, tc, reverse, has_prev):
    refs = list(refs)
    r_ref, w_ref, k_ref, a_ref, b_ref, v_ref = [refs.pop(0) for _ in range(6)]
    yp_ref = refs.pop(0) if has_prev else None
    y_ref, s_ref = refs

    @pl.when(pl.program_id(0) == 0)
    def _():
        s_ref[...] = jnp.zeros_like(s_ref)

    def step(tt, carry):
        t = tc - 1 - tt if reverse else tt
        w_t = w_ref[t]
        kk_t = a_ref[t]
        b_t = b_ref[t]
        k_t = k_ref[t]
        r_t = r_ref[t]
        for vi in range(HEAD):
            s = s_ref[vi]
            sa = -jnp.sum(s * kk_t, axis=0, keepdims=True)
            vv = v_ref[t, vi:vi + 1, :]
            s = s * w_t + sa * b_t + vv * k_t
            s_ref[vi] = s
            y = jnp.sum(s * r_t, axis=0, keepdims=True)
            if has_prev:
                y = y + yp_ref[t, vi:vi + 1, :]
            y_ref[t, vi:vi + 1, :] = y
        return carry

    lax.fori_loop(0, tc, step, 0)


def _rwkv_scan(r, w, k, kk, b, v, y_prev, reverse, n_ctx_chunks, tc):
    t_total, hs, lanes = r.shape
    nc = t_total // tc
    if reverse:
        def cmap(c):
            return jnp.where(c < n_ctx_chunks, n_ctx_chunks - 1 - c, nc - 1 - (c - n_ctx_chunks))
    else:
        def cmap(c):
            return c
    blk = pl.BlockSpec((tc, hs, lanes), lambda c: (cmap(c), 0, 0))
    args = [r, w, k, kk, b, v]
    if y_prev is not None:
        args.append(y_prev)
    return pl.pallas_call(
        functools.partial(_scan_kernel, tc=tc, reverse=reverse, has_prev=y_prev is not None),
        grid=(nc,),
        in_specs=[blk] * len(args),
        out_specs=blk,
        out_shape=jax.ShapeDtypeStruct((t_total, hs, lanes), F32),
        scratch_shapes=[pltpu.VMEM((HEAD, hs, lanes), F32)],
        compiler_params=_cparams("arbitrary"),
        name="rwkv_scan_rev" if reverse else "rwkv_scan_fwd",
    )(*args)


def _readout_kernel(ys_ref, bonus_ref, sg_ref, g2_ref, lnx_ref, seg_ref, o_ref):
    ys = ys_ref[...]
    inv = 1.0 / HEAD
    mu = _head_sum(ys, seg_ref) * inv
    dlt = ys - mu
    var = _head_sum(dlt * dlt, seg_ref) * inv
    y = dlt * lax.rsqrt(var + GN_EPS) * lnx_ref[0:1, :] + lnx_ref[1:2, :] + bonus_ref[...]
    g = jnp.dot(sg_ref[...], g2_ref[...], preferred_element_type=F32)
    o_ref[...] = (y * g).astype(o_ref.dtype)


def _rwkv_readout(ys, bonus, sg, g2, lnx_g, lnx_b):
    rows, d = ys.shape
    tr = _pick(rows, (256, 128, 64, 32, 16, 8))
    cb = _pick(d, (1024, 512, 256, 128))
    cw = min(cb, 256)
    tile = pl.BlockSpec((tr, cb), lambda i, j: (i, j))
    lnx = jnp.stack([lnx_g, lnx_b]).astype(F32)
    return pl.pallas_call(
        _readout_kernel,
        grid=(rows // tr, d // cb),
        in_specs=[tile, tile,
                  pl.BlockSpec((tr, sg.shape[1]), lambda i, j: (i, 0)),
                  pl.BlockSpec((g2.shape[0], cb), lambda i, j: (0, j)),
                  pl.BlockSpec((2, cb), lambda i, j: (0, j)),
                  pl.BlockSpec((cw, cw), lambda i, j: (0, 0))],
        out_specs=tile,
        out_shape=jax.ShapeDtypeStruct((rows, d), BF16),
        compiler_params=_cparams("parallel", "parallel"),
        name="rwkv_readout",
    )(ys, bonus, sg, g2, lnx, _seg_ones(cw))


def _resid_ln_kernel(x_ref, y_ref, mod_ref, ln_ref, xo_ref, *rest, alpha, gate_row, mod_rows):
    gate = mod_ref[0, gate_row:gate_row + 1, :]
    z = alpha * x_ref[...] + gate * y_ref[...]
    mu = jnp.mean(z, axis=-1, keepdims=True)
    dz = z - mu
    var = jnp.mean(dz * dz, axis=-1, keepdims=True)
    zn = dz * lax.rsqrt(var + LN_EPS) * ln_ref[0:1, :] + ln_ref[1:2, :]
    xo_ref[...] = zn
    if mod_rows is not None:
        sh = mod_ref[0, mod_rows[0]:mod_rows[0] + 1, :]
        sc = mod_ref[0, mod_rows[1]:mod_rows[1] + 1, :]
        rest[0][...] = (zn * (1.0 + sc) + sh).astype(rest[0].dtype)


def _resid_ln(x, y, mod, ln_g, ln_b, alpha, gate_row, mod_rows, dims):
    rows, d = x.shape
    tr = _pick(dims["tr"], (256, 128, 64, 32, 16, 8))
    gmap = dims["gmap"]
    ratio = dims["tr"] // tr
    tile = pl.BlockSpec((tr, d), lambda i: (i, 0))
    ln = jnp.stack([ln_g, ln_b]).astype(F32)
    out_shape = [jax.ShapeDtypeStruct((rows, d), F32)]
    out_specs = [tile]
    if mod_rows is not None:
        out_shape.append(jax.ShapeDtypeStruct((rows, d), BF16))
        out_specs.append(tile)
    outs = pl.pallas_call(
        functools.partial(_resid_ln_kernel, alpha=alpha, gate_row=gate_row, mod_rows=mod_rows),
        grid=(rows // tr,),
        in_specs=[tile, tile,
                  pl.BlockSpec((1, 6, d), lambda i: (gmap(i // ratio), 0, 0)),
                  pl.BlockSpec((2, d), lambda i: (0, 0))],
        out_specs=out_specs,
        out_shape=out_shape,
        compiler_params=_cparams("parallel"),
        name="resid_ln",
    )(x, y, mod, ln)
    return outs


def _modulate_kernel(x_ref, mod_ref, o_ref):
    o_ref[...] = x_ref[...] * (1.0 + mod_ref[0, 1:2, :]) + mod_ref[0, 0:1, :]


def _modulate(x, mod, dims):
    rows, d = x.shape
    tr = dims["tr"]
    gmap = dims["gmap"]
    tile = pl.BlockSpec((tr, d // 4), lambda i, j: (i, j))
    return pl.pallas_call(
        _modulate_kernel,
        grid=(rows // tr, 4),
        in_specs=[tile, pl.BlockSpec((1, 6, d // 4), lambda i, j: (gmap(i), 0, j))],
        out_specs=tile,
        out_shape=jax.ShapeDtypeStruct((rows, d), F32),
        compiler_params=_cparams("parallel", "parallel"),
        name="modulate",
    )(x, mod)


def _s5_kernel(u_ref, bm_ref, cm_ref, pw_ref, y_ref, bu_ref, cr_ref, ci_ref, *, tcs, ns, reverse):
    @pl.when(pl.program_id(2) == 0)
    def _():
        cr_ref[...] = jnp.zeros_like(cr_ref)
        ci_ref[...] = jnp.zeros_like(ci_ref)

    bu_ref[...] = jnp.dot(u_ref[...].astype(BF16), bm_ref[0], preferred_element_type=F32)
    row = lax.broadcasted_iota(jnp.int32, (SUBLANE, 1), 0)
    n_slab = tcs // SUBLANE

    def slab(gi, carry):
        g = n_slab - 1 - gi if reverse else gi
        base = pl.multiple_of(g * SUBLANE, SUBLANE)
        xr = bu_ref[pl.ds(base, SUBLANE), 0:ns]
        xi = bu_ref[pl.ds(base, SUBLANE), ns:2 * ns]
        for sft in (1, 2, 4):
            lr = pw_ref[0, sft - 1:sft, 0:ns]
            li = pw_ref[0, sft - 1:sft, ns:2 * ns]
            if reverse:
                keep = row < SUBLANE - sft
                amt = SUBLANE - sft
            else:
                keep = row >= sft
                amt = sft
            sr = jnp.where(keep, pltpu.roll(xr, amt, 0), 0.0)
            si = jnp.where(keep, pltpu.roll(xi, amt, 0), 0.0)
            xr, xi = xr + lr * sr - li * si, xi + lr * si + li * sr
        if reverse:
            pr = pw_ref[0, SUBLANE:2 * SUBLANE, 0:ns]
            pi = pw_ref[0, SUBLANE:2 * SUBLANE, ns:2 * ns]
        else:
            pr = pw_ref[0, 0:SUBLANE, 0:ns]
            pi = pw_ref[0, 0:SUBLANE, ns:2 * ns]
        c_r = cr_ref[...]
        c_i = ci_ref[...]
        hr = xr + pr * c_r - pi * c_i
        hi = xi + pr * c_i + pi * c_r
        bu_ref[pl.ds(base, SUBLANE), 0:ns] = hr
        bu_ref[pl.ds(base, SUBLANE), ns:2 * ns] = hi
        edge = 0 if reverse else SUBLANE - 1
        cr_ref[...] = hr[edge:edge + 1, :]
        ci_ref[...] = hi[edge:edge + 1, :]
        return carry

    lax.fori_loop(0, n_slab, slab, 0)
    y_ref[...] = jnp.dot(bu_ref[...].astype(BF16), cm_ref[0], preferred_element_type=F32)


def _s5_scan(u, bm, cm, pw, reverse, dims):
    rows, d = u.shape
    tcs = dims["tcs"]
    cbw = bm.shape[1]
    ns2 = bm.shape[2]
    n_lat_blk = dims["L"] // tcs
    n_ctx_blk = dims["Lc"] // tcs
    n_lat_rows_blk = dims["n_lat"] // tcs
    nchunk = n_ctx_blk + n_lat_blk

    def rmap(b, c):
        if reverse:
            ctx_blk = n_lat_rows_blk + b * n_ctx_blk + (n_ctx_blk - 1 - c)
            lat_blk = b * n_lat_blk + (n_lat_blk - 1 - (c - n_ctx_blk))
        else:
            ctx_blk = n_lat_rows_blk + b * n_ctx_blk + c
            lat_blk = b * n_lat_blk + (c - n_ctx_blk)
        return jnp.where(c < n_ctx_blk, ctx_blk, lat_blk)

    return pl.pallas_call(
        functools.partial(_s5_kernel, tcs=tcs, ns=ns2 // 2, reverse=reverse),
        grid=(dims["B"], d // cbw, nchunk),
        in_specs=[pl.BlockSpec((tcs, cbw), lambda b, g, c: (rmap(b, c), g)),
                  pl.BlockSpec((1, cbw, ns2), lambda b, g, c: (g, 0, 0)),
                  pl.BlockSpec((1, ns2, cbw), lambda b, g, c: (g, 0, 0)),
                  pl.BlockSpec((1, 2 * SUBLANE, ns2), lambda b, g, c: (g, 0, 0))],
        out_specs=pl.BlockSpec((tcs, cbw), lambda b, g, c: (rmap(b, c), g)),
        out_shape=jax.ShapeDtypeStruct((rows, d), F32),
        scratch_shapes=[pltpu.VMEM((tcs, ns2), F32),
                        pltpu.VMEM((1, ns2 // 2), F32), pltpu.VMEM((1, ns2 // 2), F32)],
        compiler_params=_cparams("parallel", "parallel", "arbitrary"),
        name="s5_scan_rev" if reverse else "s5_scan_fwd",
    )(u, bm, cm, pw)


def _s5_finish_kernel(yf_ref, yr_ref, u_ref, d_ref, o_ref):
    y = yf_ref[...] + yr_ref[...] + d_ref[...] * u_ref[...]
    c = math.sqrt(2.0 / math.pi)
    g = 0.5 * y * (1.0 + jnp.tanh(c * (y + 0.044715 * (y * y * y))))
    o_ref[...] = g.astype(o_ref.dtype)


def _s5_finish(yf, yr, u, dvec):
    rows, d = u.shape
    tr = _pick(rows, (512, 256, 128, 64, 32, 16, 8))
    cb = _pick(d, (1024, 512, 256, 128))
    tile = pl.BlockSpec((tr, cb), lambda i, j: (i, j))
    return pl.pallas_call(
        _s5_finish_kernel,
        grid=(rows // tr, d // cb),
        in_specs=[tile, tile, tile, pl.BlockSpec((1, cb), lambda i, j: (0, j))],
        out_specs=tile,
        out_shape=jax.ShapeDtypeStruct((rows, d), BF16),
        compiler_params=_cparams("parallel", "parallel"),
        name="s5_finish",
    )(yf, yr, u, dvec.reshape(1, -1).astype(F32))


def _s5_params(lam_re, lam_im, log_dt, b_re, b_im, c_re, c_im, gpb):
    lam = lax.complex(lam_re.astype(F32), lam_im.astype(F32))
    dt = jnp.exp(log_dt.astype(F32))[..., None]
    lam_bar = jnp.exp(lam * dt)
    b_bar = ((lam_bar - 1.0) / lam)[..., None] * lax.complex(b_re.astype(F32), b_im.astype(F32))
    c_mat = lax.complex(c_re.astype(F32), c_im.astype(F32))
    ndir, g, p = lam_bar.shape
    i_sz = b_bar.shape[-1]
    nblk = g // gpb
    eye = jnp.eye(gpb, dtype=F32)

    def blockdiag_in(m):
        m = m.reshape(ndir, nblk, gpb, p, i_sz)
        out = jnp.einsum("dngpi,gh->dngihp", m, eye)
        return out.reshape(ndir, nblk, gpb * i_sz, gpb * p)

    def blockdiag_out(m):
        m = m.reshape(ndir, nblk, gpb, i_sz, p)
        out = jnp.einsum("dngip,gh->dngphi", m, eye)
        return out.reshape(ndir, nblk, gpb * p, gpb * i_sz)

    bm = jnp.concatenate([blockdiag_in(jnp.real(b_bar)), blockdiag_in(jnp.imag(b_bar))], axis=-1).astype(BF16)
    cm = jnp.concatenate([blockdiag_out(jnp.real(c_mat)), blockdiag_out(-jnp.imag(c_mat))], axis=-2).astype(BF16)
    pws = [lam_bar]
    for _ in range(SUBLANE - 1):
        pws.append(pws[-1] * lam_bar)
    pw = jnp.stack(pws, axis=1)
    pw = jnp.concatenate([pw, pw[:, ::-1]], axis=1)
    pw = pw.reshape(ndir, 2 * SUBLANE, nblk, gpb * p).transpose(0, 2, 1, 3)
    pw = jnp.concatenate([jnp.real(pw), jnp.imag(pw)], axis=-1).astype(F32)
    return bm, cm, pw


def _to_scan(a, dims):
    b, l, lc = dims["B"], dims["L"], dims["Lc"]
    h = a.shape[1] // HEAD
    lat = a[:b * l].reshape(b, l, h, HEAD).transpose(1, 3, 0, 2).reshape(l, HEAD, b * h)
    ctx = a[b * l:].reshape(b, lc, h, HEAD).transpose(1, 3, 0, 2).reshape(lc, HEAD, b * h)
    return jnp.concatenate([ctx, lat], axis=0)


def _from_scan(y, dims):
    b, l, lc = dims["B"], dims["L"], dims["Lc"]
    h = y.shape[2] // b
    ctx = y[:lc].reshape(lc, HEAD, b, h).transpose(2, 0, 3, 1).reshape(b * lc, h * HEAD)
    lat = y[lc:].reshape(l, HEAD, b, h).transpose(2, 0, 3, 1).reshape(b * l, h * HEAD)
    return jnp.concatenate([lat, ctx], axis=0)


def kernel(x, c, ctx, c_ctx, ada_down, ada_up, ada_bias, ln_g, ln_b, rw_mix, rw_wr, rw_wk, rw_wv, rw_wo, rw_w0, rw_w1, rw_w2, rw_a0, rw_a1, rw_a2, rw_v0, rw_v1, rw_v2, rw_g1, rw_g2, rw_kk, rw_ka, rw_rk, rw_lnx_g, rw_lnx_b, s5_lam_re, s5_lam_im, s5_log_dt, s5_b_re, s5_b_im, s5_c_re, s5_c_im, s5_d, s5_glu_w, s5_glu_b, mlp_w1, mlp_w2):
    bsz, seq, d = x.shape
    lc = ctx.shape[1]
    depth = ada_down.shape[0]
    alpha = (2 * depth) ** 0.25
    n_lat = bsz * seq
    rows = n_lat + bsz * lc
    tr = _pick(math.gcd(seq, bsz * lc), (512, 256, 128, 64))
    tiles_per_batch = seq // tr

    def gmap(i):
        return jnp.minimum(i // tiles_per_batch, bsz)

    tcs = _pick(math.gcd(seq, lc), (256, 128, 64, 32, 16, 8))
    dims = {"B": bsz, "L": seq, "Lc": lc, "n_lat": n_lat, "tr": tr, "gmap": gmap, "tcs": tcs}
    tc = _pick(math.gcd(seq, lc), (32, 16, 8))

    xs = jnp.concatenate([x.reshape(n_lat, d), ctx.reshape(bsz * lc, d)], axis=0).astype(F32)
    vf = None

    cvec = jnp.concatenate([c, c_ctx[None, :]], axis=0)
    cvec = _pad_to(jax.nn.silu(cvec), 0, 16).astype(BF16)

    for i in range(depth):
        low = _mm(cvec, ada_down[i].astype(BF16), out_dtype=BF16, name="adaln_down")
        mod = _mm(low, ada_up[i].astype(BF16), bias=ada_bias[i], name="adaln_up")
        mod = mod[:bsz + 1].reshape(bsz + 1, 6, d)
        j = i // 2
        if i % 2 == 0:
            rwd = rw_w1.shape[-1]
            rad = rw_a1.shape[-1]
            p = {"w2": _pad_to(rw_w2[j], 1, LANE).astype(BF16), "a2": _pad_to(rw_a2[j], 1, LANE).astype(BF16),
                 "w0": rw_w0[j], "a0": rw_a0[j], "k_k": rw_kk[j], "k_a": rw_ka[j],
                 "r_k": rw_rk[j].reshape(-1)}
            lerps = _rwkv_prep(xs, mod, rw_mix[j], dims)
            r = _mm(lerps[0], rw_wr[j].astype(BF16), name="rwkv_r")
            k = _mm(lerps[2], rw_wk[j].astype(BF16), name="rwkv_k")
            v = _mm(lerps[3], rw_wv[j].astype(BF16), name="rwkv_v")
            w1cat = jnp.concatenate([_pad_to(rw_w1[j, dd], 1, LANE) for dd in range(2)], axis=1)
            a1cat = jnp.concatenate([_pad_to(rw_a1[j, dd], 1, LANE) for dd in range(2)], axis=1)
            tw = _mm(lerps[1], w1cat.astype(BF16), act="tanh", out_dtype=BF16, name="rwkv_w1")
            ax = _mm(lerps[4], a1cat.astype(BF16), out_dtype=BF16, name="rwkv_a1")
            sg = _mm(lerps[5], _pad_to(rw_g1[j], 1, LANE).astype(BF16), act="sigmoid", out_dtype=BF16,
                     name="rwkv_g1")
            g2 = _pad_to(rw_g2[j], 0, LANE).astype(BF16)
            xv = None
            if j > 0:
                xv = _mm(lerps[3], _pad_to(rw_v1[j - 1], 1, LANE).astype(BF16), out_dtype=BF16, name="rwkv_v1")
                p["v2"] = _pad_to(rw_v2[j - 1], 0, LANE).astype(BF16)
                p["v0"] = rw_v0[j - 1]
            feats = _rwkv_features(r, k, v, tw, ax, p, vf if j > 0 else None, xv, dims)
            kk, dec0, dec1, kd0, kd1, b0, b1, bonus = feats[:8]
            if j > 0:
                v = feats[8]
            else:
                vf = v
            r_s, v_s, kk_s = _to_scan(r, dims), _to_scan(v, dims), _to_scan(kk, dims)
            n_ctx_chunks = lc // tc
            y = _rwkv_scan(r_s, _to_scan(dec0, dims), _to_scan(kd0, dims), kk_s, _to_scan(b0, dims), v_s,
                           None, False, n_ctx_chunks, tc)
            y = _rwkv_scan(r_s, _to_scan(dec1, dims), _to_scan(kd1, dims), kk_s, _to_scan(b1, dims), v_s,
                           y, True, n_ctx_chunks, tc)
            ys = _from_scan(y, dims)
            yg = _rwkv_readout(ys, bonus, sg, g2, rw_lnx_g[j], rw_lnx_b[j])
            mix_out = _mm(yg, rw_wo[j].astype(BF16), name="rwkv_o")
        else:
            gpb = min(256 // S5_GROUP, d // S5_GROUP)
            bm, cm, pw = _s5_params(s5_lam_re[j], s5_lam_im[j], s5_log_dt[j], s5_b_re[j], s5_b_im[j],
                                    s5_c_re[j], s5_c_im[j], gpb)
            h = _modulate(xs, mod, dims)
            yf = _s5_scan(h, bm[0], cm[0], pw[0], False, dims)
            yr = _s5_scan(h, bm[1], cm[1], pw[1], True, dims)
            gl = _s5_finish(yf, yr, h, s5_d[j])
            mix_out = _mm(gl, s5_glu_w[j].astype(BF16), bias=s5_glu_b[j], glu=True, name="s5_glu")
        xs, h2 = _resid_ln(xs, mix_out, mod, ln_g[i, 0], ln_b[i, 0], alpha, 2, (3, 4), dims)
        a1 = _mm(h2, mlp_w1[i].astype(BF16), act="relu2", out_dtype=BF16, name="mlp_w1")
        mlp_out = _mm(a1, mlp_w2[i].astype(BF16), name="mlp_w2")
        (xs,) = _resid_ln(xs, mlp_out, mod, ln_g[i, 1], ln_b[i, 1], alpha, 5, None, dims)
    return xs[:n_lat].reshape(bsz, seq, d).astype(x.dtype)
```

```python
import functools
import math

import jax
import jax.numpy as jnp
from jax import lax
from jax.experimental import pallas as pl
from jax.experimental.pallas import tpu as pltpu

F32 = jnp.float32
BF16 = jnp.bfloat16

GRID_W = 64
HEAD = 64
S5_GROUP = 16
S5_STATE = 64
S5_CH = 256
LN_EPS = 1e-6
GN_EPS = 64e-5
LANE = 128
SUBLANE = 8
PITCH = HEAD + SUBLANE
VMEM_LIMIT = 56 * 1024 * 1024


def _cparams(*sem):
    return pltpu.CompilerParams(dimension_semantics=sem, vmem_limit_bytes=VMEM_LIMIT)


def _pick(n, prefs):
    for p in prefs:
        if n % p == 0:
            return p
    return n


def _pad_to(a, axis, mult):
    n = a.shape[axis]
    r = (-n) % mult
    if r == 0:
        return a
    pad = [(0, 0)] * a.ndim
    pad[axis] = (0, r)
    return jnp.pad(a, pad)


def _mm_kernel(*refs, nk, act, has_bias, glu):
    refs = list(refs)
    x_ref = refs.pop(0)
    w_refs = [refs.pop(0) for _ in range(2 if glu else 1)]
    b_refs = [refs.pop(0) for _ in range((2 if glu else 1) if has_bias else 0)]
    o_ref = refs.pop(0)
    acc_refs = refs
    k = pl.program_id(2)

    def finish(zs):
        if has_bias:
            zs = [z + b[...] for z, b in zip(zs, b_refs)]
        if glu:
            z = zs[0] * jax.nn.sigmoid(zs[1])
        else:
            z = zs[0]
            if act == "tanh":
                z = jnp.tanh(z)
            elif act == "sigmoid":
                z = jax.nn.sigmoid(z)
            elif act == "relu2":
                z = jnp.square(jnp.maximum(z, 0.0))
        o_ref[...] = z.astype(o_ref.dtype)

    x = x_ref[...]
    parts = [jnp.dot(x, w[...], preferred_element_type=F32) for w in w_refs]
    if nk == 1:
        finish(parts)
        return

    @pl.when(k == 0)
    def _():
        for a in acc_refs:
            a[...] = jnp.zeros_like(a)

    for a, p in zip(acc_refs, parts):
        a[...] += p

    @pl.when(k == nk - 1)
    def _():
        finish([a[...] for a in acc_refs])


def _mm(x, w, bias=None, act=None, out_dtype=F32, glu=False, tm=None, tn=None, tk=None, name="mm"):
    m, kdim = x.shape
    n = w.shape[1] // (2 if glu else 1)
    tm = tm or _pick(m, (1536, 1024, 768, 512, 256, 128, 64, 32, 16, 8))
    tn = tn or _pick(n, (512, 256, 128))
    tk = tk or (kdim if kdim <= 4096 else _pick(kdim, (2048, 1024, 512, 256, 128)))
    nk = kdim // tk
    nj = n // tn
    in_specs = [pl.BlockSpec((tm, tk), lambda i, j, k: (i, k)),
                pl.BlockSpec((tk, tn), lambda i, j, k: (k, j))]
    args = [x, w]
    if glu:
        in_specs.append(pl.BlockSpec((tk, tn), lambda i, j, k: (k, j + nj)))
        args.append(w)
    if bias is not None:
        b2 = bias.reshape(1, -1).astype(F32)
        in_specs.append(pl.BlockSpec((1, tn), lambda i, j, k: (0, j)))
        args.append(b2)
        if glu:
            in_specs.append(pl.BlockSpec((1, tn), lambda i, j, k: (0, j + nj)))
            args.append(b2)
    scratch = [] if nk == 1 else [pltpu.VMEM((tm, tn), F32) for _ in range(2 if glu else 1)]
    return pl.pallas_call(
        functools.partial(_mm_kernel, nk=nk, act=act, has_bias=bias is not None, glu=glu),
        grid=(m // tm, nj, nk),
        in_specs=in_specs,
        out_specs=pl.BlockSpec((tm, tn), lambda i, j, k: (i, j)),
        out_shape=jax.ShapeDtypeStruct((m, n), out_dtype),
        scratch_shapes=scratch,
        compiler_params=_cparams("parallel", "parallel", "arbitrary"),
        name=name,
    )(*args)


def _prep_kernel(xp_ref, xc_ref, xn_ref, mod_ref, mix_ref, *rest, tr, n_lat, l_img, l_ctx):
    o_refs = rest[:6]
    hbuf, sbuf = rest[6], rest[7]
    i = pl.program_id(0)
    j = pl.program_id(1)
    sh = mod_ref[0, 0:1, :]
    sc = 1.0 + mod_ref[0, 1:2, :]
    hbuf[0:GRID_W, :] = xp_ref[...] * sc + sh
    hbuf[GRID_W:GRID_W + tr, :] = xc_ref[...] * sc + sh
    hbuf[GRID_W + tr:GRID_W + tr + GRID_W, :] = xn_ref[...] * sc + sh
    row = lax.broadcasted_iota(jnp.int32, (tr, 1), 0) + i * tr
    t_lat = row % l_img
    is_lat = i < n_lat // tr

    def shifted(s, keep):
        sbuf[...] = jnp.where(keep, hbuf[GRID_W + s:GRID_W + s + tr, :], 0.0)

    @pl.when(jnp.logical_and(is_lat, j == 0))
    def _():
        shifted(-1, t_lat % GRID_W != 0)

    @pl.when(jnp.logical_and(is_lat, j == 1))
    def _():
        shifted(1, t_lat % GRID_W != GRID_W - 1)

    @pl.when(jnp.logical_and(is_lat, j == 2))
    def _():
        shifted(-GRID_W, t_lat >= GRID_W)

    @pl.when(jnp.logical_and(is_lat, j == 3))
    def _():
        shifted(GRID_W, t_lat < l_img - GRID_W)

    t_ctx = (row - n_lat) % l_ctx

    @pl.when(jnp.logical_and(jnp.logical_not(is_lat), j < 2))
    def _():
        shifted(-1, t_ctx != 0)

    @pl.when(jnp.logical_and(jnp.logical_not(is_lat), j >= 2))
    def _():
        shifted(1, t_ctx != l_ctx - 1)

    h = hbuf[GRID_W:GRID_W + tr, :]
    xx = sbuf[...] - h
    for m in range(6):
        o_refs[m][...] = (h + xx * mix_ref[m:m + 1, :]).astype(BF16)


def _rwkv_prep(x, mod, mix, dims):
    r, d = x.shape
    tr = dims["tr"]
    dc = d // 4
    nb = tr // GRID_W
    last = r // GRID_W - 1
    kern = functools.partial(_prep_kernel, tr=tr, n_lat=dims["n_lat"], l_img=dims["L"], l_ctx=dims["Lc"])
    gmap = dims["gmap"]
    outs = pl.pallas_call(
        kern,
        grid=(r // tr, 4),
        in_specs=[
            pl.BlockSpec((GRID_W, dc), lambda i, j: (jnp.maximum(i * nb - 1, 0), j)),
            pl.BlockSpec((tr, dc), lambda i, j: (i, j)),
            pl.BlockSpec((GRID_W, dc), lambda i, j: (jnp.minimum(i * nb + nb, last), j)),
            pl.BlockSpec((1, 6, dc), lambda i, j: (gmap(i), 0, j)),
            pl.BlockSpec((6, dc), lambda i, j: (0, j)),
        ],
        out_specs=[pl.BlockSpec((tr, dc), lambda i, j: (i, j)) for _ in range(6)],
        out_shape=[jax.ShapeDtypeStruct((r, d), BF16) for _ in range(6)],
        scratch_shapes=[pltpu.VMEM((tr + 2 * GRID_W, dc), F32), pltpu.VMEM((tr, dc), F32)],
        compiler_params=_cparams("parallel", "parallel"),
        name="rwkv_prep",
    )(x, x, x, mod, mix)
    return outs


def _fold_heads(x, cl, h):
    acc = x[:, 0:cl]
    for m in range(1, x.shape[1] // cl):
        acc = acc + x[:, m * cl:(m + 1) * cl]
    return acc + pltpu.roll(acc, h, 1)


def _store_scan(o_ref, z0, z1, cl, h):
    tr = z0.shape[0]
    lo = lax.broadcasted_iota(jnp.int32, (tr, cl), 1) < h
    for m in range(z0.shape[1] // cl):
        c0 = z0[:, m * cl:(m + 1) * cl]
        c1 = z1[:, m * cl:(m + 1) * cl]
        o_ref[pl.ds(2 * m, tr, stride=PITCH), :] = jnp.where(lo, c0, pltpu.roll(c1, h, 1))
        o_ref[pl.ds(2 * m + 1, tr, stride=PITCH), :] = jnp.where(lo, pltpu.roll(c0, h, 1), c1)


def _feat_kernel(*refs, has_vf, rw, ra, tr, cl, h):
    refs = list(refs)

    def pair():
        return [refs.pop(0), refs.pop(0)]

    r_refs, k_refs, v_refs, tw_refs, ax_refs = pair(), pair(), pair(), pair(), pair()
    w2_ref, a2_ref, w0_ref, a0_ref, vec_ref = [refs.pop(0) for _ in range(5)]
    if has_vf:
        vf_refs, xv_refs = pair(), pair()
        v2_ref, v0_ref = refs.pop(0), refs.pop(0)
    kk_o, dec0_o, dec1_o, kd0_o, kd1_o, b0_o, b1_o, r_o, v_o, bonus_o = refs

    def both(pr):
        return jnp.concatenate([pr[0][...], pr[1][...]], axis=0)

    def put(o_ref, z):
        _store_scan(o_ref, z[:tr], z[tr:], cl, h)

    def tiled(s):
        return jnp.concatenate([s] * (k.shape[1] // cl), axis=1)

    k = both(k_refs)
    r = both(r_refs)
    v = both(v_refs)
    tw = both(tw_refs)
    ax = both(ax_refs)
    k_k = vec_ref[0:1, :]
    k_a = vec_ref[1:2, :]
    r_k = vec_ref[2:3, :]
    kraw = k * k_k
    nrm = jnp.sqrt(_fold_heads(kraw * kraw, cl, h))
    kk = kraw * tiled(1.0 / jnp.maximum(nrm, 1e-12))
    put(kk_o, kk)
    put(r_o, r)
    if has_vf:
        vl = v0_ref[...] + jnp.dot(both(xv_refs), v2_ref[...], preferred_element_type=F32)
        v = v + (both(vf_refs) - v) * jax.nn.sigmoid(vl)
    put(v_o, v)
    kd_sum = None
    for d, (dec_o, kd_o, b_o) in enumerate(((dec0_o, kd0_o, b0_o), (dec1_o, kd1_o, b1_o))):
        wl = w0_ref[d:d + 1, :] + jnp.dot(tw[:, d * rw:(d + 1) * rw], w2_ref[d], preferred_element_type=F32)
        nwl = -wl
        softplus = jnp.maximum(nwl, 0.0) + jnp.log(1.0 + jnp.exp(-jnp.abs(nwl)))
        put(dec_o, jnp.exp(-jnp.exp(-softplus - 0.5)))
        al = a0_ref[d:d + 1, :] + jnp.dot(ax[:, d * ra:(d + 1) * ra], a2_ref[d], preferred_element_type=F32)
        a = jax.nn.sigmoid(al)
        kd = k * (1.0 + (a - 1.0) * k_a)
        put(kd_o, kd)
        put(b_o, kk * a)
        kd_sum = kd if kd_sum is None else kd_sum + kd
    put(bonus_o, tiled(_fold_heads(r * kd_sum * r_k, cl, h)) * v)


def _rwkv_features(r, k, v, tw, ax, p, vf, xv, dims):
    rows, d = r.shape
    bsz, l, lc, n_lat = dims["B"], dims["L"], dims["Lc"], dims["n_lat"]
    tr = dims["tf"]
    t_total = l + lc
    h = d // HEAD
    cl = bsz * h
    has_vf = vf is not None
    rw = p["w2"].shape[1]
    ra = p["a2"].shape[1]
    nci = lc // tr

    def rblk(b):
        return lambda i: (jnp.where(i < nci, (n_lat + b * lc) // tr + i, b * (l // tr) + i - nci), 0)

    def per_batch(a):
        return [pl.BlockSpec((tr, a.shape[1]), rblk(b)) for b in range(bsz)]

    def whole(a):
        nd = a.ndim
        return pl.BlockSpec(a.shape, lambda i: (0,) * nd)

    vec = jnp.stack([p["k_k"], p["k_a"], p["r_k"]]).astype(F32)
    args, in_specs = [], []
    for a in (r, k, v, tw, ax):
        args += [a, a]
        in_specs += per_batch(a)
    for a in (p["w2"], p["a2"], p["w0"], p["a0"], vec):
        args.append(a)
        in_specs.append(whole(a))
    if has_vf:
        for a in (vf, xv):
            args += [a, a]
            in_specs += per_batch(a)
        v0 = p["v0"].reshape(1, -1)
        args += [p["v2"], v0]
        in_specs += [whole(p["v2"]), whole(v0)]
    out_blk = pl.BlockSpec((tr * PITCH, cl), lambda i: (i, 0))
    outs = pl.pallas_call(
        functools.partial(_feat_kernel, has_vf=has_vf, rw=rw, ra=ra, tr=tr, cl=cl, h=h),
        grid=(t_total // tr,),
        in_specs=in_specs,
        out_specs=[out_blk] * 10,
        out_shape=[jax.ShapeDtypeStruct((t_total * PITCH, cl), F32)] * 10,
        compiler_params=_cparams("parallel"),
        name="rwkv_features",
    )(*args)
    return [o.reshape(t_total, PITCH, cl) for o in outs]


def _scan_kernel(*refs, tc, reverse, has_prev):
    refs = list(refs)
    r_ref, w_ref, k_ref, a_ref, b_ref, v_ref = [refs.pop(0) for _ in range(6)]
    yp_ref = refs.pop(0) if has_prev else None
    y_ref, s_ref = refs

    @pl.when(pl.program_id(0) == 0)
    def _():
        s_ref[...] = jnp.zeros_like(s_ref)

    y_ref[:, HEAD:, :] = jnp.zeros((tc, y_ref.shape[1] - HEAD, y_ref.shape[2]), F32)

    def step(tt, carry):
        t = tc - 1 - tt if reverse else tt
        w_t = w_ref[t, 0:HEAD, :]
        kk_t = a_ref[t, 0:HEAD, :]
        b_t = b_ref[t, 0:HEAD, :]
        k_t = k_ref[t, 0:HEAD, :]
        r_t = r_ref[t, 0:HEAD, :]
        for vi in range(HEAD):
            s = s_ref[vi]
            sa = -jnp.sum(s * kk_t, axis=0, keepdims=True)
            vv = v_ref[t, vi:vi + 1, :]
            s = s * w_t + sa * b_t + vv * k_t
            s_ref[vi] = s
            y = jnp.sum(s * r_t, axis=0, keepdims=True)
            if has_prev:
                y = y + yp_ref[t, vi:vi + 1, :]
            y_ref[t, vi:vi + 1, :] = y
        return carry

    lax.fori_loop(0, tc, step, 0)


def _rwkv_scan(r, w, k, kk, b, v, y_prev, reverse, n_ctx_chunks, tc):
    t_total, pitch, lanes = r.shape
    nc = t_total // tc
    if reverse:
        def cmap(c):
            return jnp.where(c < n_ctx_chunks, n_ctx_chunks - 1 - c, nc - 1 - (c - n_ctx_chunks))
    else:
        def cmap(c):
            return c
    blk = pl.BlockSpec((tc, pitch, lanes), lambda c: (cmap(c), 0, 0))
    args = [r, w, k, kk, b, v]
    if y_prev is not None:
        args.append(y_prev)
    return pl.pallas_call(
        functools.partial(_scan_kernel, tc=tc, reverse=reverse, has_prev=y_prev is not None),
        grid=(nc,),
        in_specs=[blk] * len(args),
        out_specs=blk,
        out_shape=jax.ShapeDtypeStruct((t_total, pitch, lanes), F32),
        scratch_shapes=[pltpu.VMEM((HEAD, HEAD, lanes), F32)],
        compiler_params=_cparams("arbitrary"),
        name="rwkv_scan_rev" if reverse else "rwkv_scan_fwd",
    )(*args)


def _readout_kernel(y_ref, bonus_ref, sg_ref, g2_ref, lnx_ref, o_ref, *, tr, cl, h):
    is_b0 = pl.program_id(0) == 0
    nv = HEAD
    inv = 1.0 / nv

    def rows(ref, m):
        return ref[pl.ds(m, tr, stride=PITCH), :]

    acc = rows(y_ref, 0)
    for m in range(1, nv):
        acc = acc + rows(y_ref, m)
    mu = acc * inv
    acc = None
    for m in range(nv):
        dlt = rows(y_ref, m) - mu
        acc = dlt * dlt if acc is None else acc + dlt * dlt
    rs = lax.rsqrt(acc * inv + GN_EPS)
    g = jnp.dot(sg_ref[...], g2_ref[...], preferred_element_type=F32)
    lo = lax.broadcasted_iota(jnp.int32, (tr, cl), 1) < h

    def normed(m):
        return (rows(y_ref, m) - mu) * rs * lnx_ref[0:1, m * cl:(m + 1) * cl] \
            + lnx_ref[1:2, m * cl:(m + 1) * cl] + rows(bonus_ref, m)

    for m in range(nv // 2):
        ya = normed(2 * m)
        yb = normed(2 * m + 1)
        first = jnp.where(is_b0, ya, pltpu.roll(ya, h, 1))
        second = jnp.where(is_b0, pltpu.roll(yb, h, 1), yb)
        out = jnp.where(lo, first, second) * g[:, m * cl:(m + 1) * cl]
        o_ref[:, m * cl:(m + 1) * cl] = out.astype(o_ref.dtype)


def _rwkv_readout(y, bonus, sg, g2, lnx_g, lnx_b, dims):
    t_total, pitch, cl = y.shape
    bsz, l, lc, n_lat = dims["B"], dims["L"], dims["Lc"], dims["n_lat"]
    h = cl // bsz
    d = h * HEAD
    width = bsz * d
    tr = dims["to"]
    y = y.reshape(t_total * pitch, cl)
    bonus = bonus.reshape(t_total * pitch, cl)
    nci = lc // tr

    def rblk(b, i):
        return jnp.where(i < nci, (n_lat + b * lc) // tr + i, b * (l // tr) + i - nci)

    def spread(vec):
        return jnp.broadcast_to(vec.reshape(HEAD, 1, h), (HEAD, bsz, h)).reshape(1, width)

    lnx = jnp.concatenate([spread(lnx_g), spread(lnx_b)], axis=0).astype(F32)
    scan_blk = pl.BlockSpec((tr * pitch, cl), lambda b, i: (i, 0))
    return pl.pallas_call(
        functools.partial(_readout_kernel, tr=tr, cl=cl, h=h),
        grid=(bsz, t_total // tr),
        in_specs=[scan_blk, scan_blk,
                  pl.BlockSpec((tr, sg.shape[1]), lambda b, i: (rblk(b, i), 0)),
                  pl.BlockSpec(g2.shape, lambda b, i: (0, 0)),
                  pl.BlockSpec((2, width), lambda b, i: (0, 0))],
        out_specs=pl.BlockSpec((tr, d), lambda b, i: (rblk(b, i), 0)),
        out_shape=jax.ShapeDtypeStruct((bsz * t_total, d), BF16),
        compiler_params=_cparams("parallel", "parallel"),
        name="rwkv_readout",
    )(y, bonus, sg, g2, lnx)


def _resid_ln_kernel(x_ref, y_ref, mod_ref, ln_ref, xo_ref, *rest, alpha, gate_row, mod_rows):
    gate = mod_ref[0, gate_row:gate_row + 1, :]
    z = alpha * x_ref[...] + gate * y_ref[...]
    mu = jnp.mean(z, axis=-1, keepdims=True)
    dz = z - mu
    var = jnp.mean(dz * dz, axis=-1, keepdims=True)
    zn = dz * lax.rsqrt(var + LN_EPS) * ln_ref[0:1, :] + ln_ref[1:2, :]
    xo_ref[...] = zn
    if mod_rows is not None:
        sh = mod_ref[0, mod_rows[0]:mod_rows[0] + 1, :]
        sc = mod_ref[0, mod_rows[1]:mod_rows[1] + 1, :]
        rest[0][...] = (zn * (1.0 + sc) + sh).astype(rest[0].dtype)


def _resid_ln(x, y, mod, ln_g, ln_b, alpha, gate_row, mod_rows, dims):
    rows, d = x.shape
    tr = _pick(dims["tr"], (256, 128, 64, 32, 16, 8))
    gmap = dims["gmap"]
    ratio = dims["tr"] // tr
    tile = pl.BlockSpec((tr, d), lambda i: (i, 0))
    ln = jnp.stack([ln_g, ln_b]).astype(F32)
    out_shape = [jax.ShapeDtypeStruct((rows, d), F32)]
    out_specs = [tile]
    if mod_rows is not None:
        out_shape.append(jax.ShapeDtypeStruct((rows, d), BF16))
        out_specs.append(tile)
    outs = pl.pallas_call(
        functools.partial(_resid_ln_kernel, alpha=alpha, gate_row=gate_row, mod_rows=mod_rows),
        grid=(rows // tr,),
        in_specs=[tile, tile,
                  pl.BlockSpec((1, 6, d), lambda i: (gmap(i // ratio), 0, 0)),
                  pl.BlockSpec((2, d), lambda i: (0, 0))],
        out_specs=out_specs,
        out_shape=out_shape,
        compiler_params=_cparams("parallel"),
        name="resid_ln",
    )(x, y, mod, ln)
    return outs


def _modulate_kernel(x_ref, mod_ref, o_ref):
    o_ref[...] = x_ref[...] * (1.0 + mod_ref[0, 1:2, :]) + mod_ref[0, 0:1, :]


def _modulate(x, mod, dims):
    rows, d = x.shape
    tr = dims["tr"]
    gmap = dims["gmap"]
    tile = pl.BlockSpec((tr, d // 4), lambda i, j: (i, j))
    return pl.pallas_call(
        _modulate_kernel,
        grid=(rows // tr, 4),
        in_specs=[tile, pl.BlockSpec((1, 6, d // 4), lambda i, j: (gmap(i), 0, j))],
        out_specs=tile,
        out_shape=jax.ShapeDtypeStruct((rows, d), F32),
        compiler_params=_cparams("parallel", "parallel"),
        name="modulate",
    )(x, mod)


def _s5_kernel(u_ref, bm_ref, cm_ref, lam_ref, y_ref, hre_ref, him_ref, h2d_ref, cr_ref, ci_ref,
               *, tcs, nq, ns, reverse):
    nsl = ns // LANE

    @pl.when(pl.program_id(2) == 0)
    def _():
        cr_ref[...] = jnp.zeros_like(cr_ref)
        ci_ref[...] = jnp.zeros_like(ci_ref)

    for q in range(nq):
        ub = u_ref[:, q * S5_CH:(q + 1) * S5_CH].astype(BF16)
        bu = jnp.dot(ub, bm_ref[q], preferred_element_type=F32)
        for s in range(nsl):
            hre_ref[q, pl.ds(s, tcs, stride=nsl), :] = bu[:, s * LANE:(s + 1) * LANE]
            him_ref[q, pl.ds(s, tcs, stride=nsl), :] = bu[:, ns + s * LANE:ns + (s + 1) * LANE]

    lr = [lam_ref[q, 0] for q in range(nq)]
    li = [lam_ref[q, 1] for q in range(nq)]

    def step(tt, carry):
        t = tcs - 1 - tt if reverse else tt
        base = pl.multiple_of(t * nsl, nsl)
        out = []
        for q in range(nq):
            hr, hi = carry[2 * q], carry[2 * q + 1]
            nr = lr[q] * hr - li[q] * hi + hre_ref[q, pl.ds(base, nsl), :]
            ni = lr[q] * hi + li[q] * hr + him_ref[q, pl.ds(base, nsl), :]
            hre_ref[q, pl.ds(base, nsl), :] = nr
            him_ref[q, pl.ds(base, nsl), :] = ni
            out += [nr, ni]
        return tuple(out)

    init = []
    for q in range(nq):
        init += [cr_ref[q], ci_ref[q]]
    fin = lax.fori_loop(0, tcs, step, tuple(init), unroll=8)
    for q in range(nq):
        cr_ref[q] = fin[2 * q]
        ci_ref[q] = fin[2 * q + 1]

    for q in range(nq):
        for s in range(nsl):
            h2d_ref[:, s * LANE:(s + 1) * LANE] = hre_ref[q, pl.ds(s, tcs, stride=nsl), :].astype(BF16)
            h2d_ref[:, ns + s * LANE:ns + (s + 1) * LANE] = him_ref[q, pl.ds(s, tcs, stride=nsl), :].astype(BF16)
        y_ref[:, q * S5_CH:(q + 1) * S5_CH] = jnp.dot(h2d_ref[...], cm_ref[q], preferred_element_type=F32)


def _s5_scan(u, bm, cm, lam, reverse, dims):
    rows, d = u.shape
    tcs = dims["tcs"]
    ns2 = bm.shape[2]
    ns = ns2 // 2
    nq = _pick(d // S5_CH, (4, 2, 1))
    cbw = nq * S5_CH
    n_lat_blk = dims["L"] // tcs
    n_ctx_blk = dims["Lc"] // tcs
    n_lat_rows_blk = dims["n_lat"] // tcs
    nchunk = n_ctx_blk + n_lat_blk

    def rmap(b, c):
        if reverse:
            ctx_blk = n_lat_rows_blk + b * n_ctx_blk + (n_ctx_blk - 1 - c)
            lat_blk = b * n_lat_blk + (n_lat_blk - 1 - (c - n_ctx_blk))
        else:
            ctx_blk = n_lat_rows_blk + b * n_ctx_blk + c
            lat_blk = b * n_lat_blk + (c - n_ctx_blk)
        return jnp.where(c < n_ctx_blk, ctx_blk, lat_blk)

    return pl.pallas_call(
        functools.partial(_s5_kernel, tcs=tcs, nq=nq, ns=ns, reverse=reverse),
        grid=(dims["B"], d // cbw, nchunk),
        in_specs=[pl.BlockSpec((tcs, cbw), lambda b, g, c: (rmap(b, c), g)),
                  pl.BlockSpec((nq, S5_CH, ns2), lambda b, g, c: (g, 0, 0)),
                  pl.BlockSpec((nq, ns2, S5_CH), lambda b, g, c: (g, 0, 0)),
                  pl.BlockSpec((nq, 2, ns // LANE, LANE), lambda b, g, c: (g, 0, 0, 0))],
        out_specs=pl.BlockSpec((tcs, cbw), lambda b, g, c: (rmap(b, c), g)),
        out_shape=jax.ShapeDtypeStruct((rows, d), F32),
        scratch_shapes=[pltpu.VMEM((nq, tcs * ns // LANE, LANE), F32),
                        pltpu.VMEM((nq, tcs * ns // LANE, LANE), F32),
                        pltpu.VMEM((tcs, ns2), BF16),
                        pltpu.VMEM((nq, ns // LANE, LANE), F32),
                        pltpu.VMEM((nq, ns // LANE, LANE), F32)],
        compiler_params=_cparams("parallel", "parallel", "arbitrary"),
        name="s5_scan_rev" if reverse else "s5_scan_fwd",
    )(u, bm, cm, lam)


def _s5_finish_kernel(yf_ref, yr_ref, u_ref, d_ref, o_ref):
    y = yf_ref[...] + yr_ref[...] + d_ref[...] * u_ref[...]
    c = math.sqrt(2.0 / math.pi)
    g = 0.5 * y * (1.0 + jnp.tanh(c * (y + 0.044715 * (y * y * y))))
    o_ref[...] = g.astype(o_ref.dtype)


def _s5_finish(yf, yr, u, dvec):
    rows, d = u.shape
    tr = _pick(rows, (512, 256, 128, 64, 32, 16, 8))
    cb = _pick(d, (1024, 512, 256, 128))
    tile = pl.BlockSpec((tr, cb), lambda i, j: (i, j))
    return pl.pallas_call(
        _s5_finish_kernel,
        grid=(rows // tr, d // cb),
        in_specs=[tile, tile, tile, pl.BlockSpec((1, cb), lambda i, j: (0, j))],
        out_specs=tile,
        out_shape=jax.ShapeDtypeStruct((rows, d), BF16),
        compiler_params=_cparams("parallel", "parallel"),
        name="s5_finish",
    )(yf, yr, u, dvec.reshape(1, -1).astype(F32))


def _s5_params(lam_re, lam_im, log_dt, b_re, b_im, c_re, c_im, gpb):
    lam = lax.complex(lam_re.astype(F32), lam_im.astype(F32))
    dt = jnp.exp(log_dt.astype(F32))[..., None]
    lam_bar = jnp.exp(lam * dt)
    b_bar = ((lam_bar - 1.0) / lam)[..., None] * lax.complex(b_re.astype(F32), b_im.astype(F32))
    c_mat = lax.complex(c_re.astype(F32), c_im.astype(F32))
    ndir, g, p = lam_bar.shape
    i_sz = b_bar.shape[-1]
    nblk = g // gpb
    eye = jnp.eye(gpb, dtype=F32)

    def blockdiag_in(m):
        m = m.reshape(ndir, nblk, gpb, p, i_sz)
        out = jnp.einsum("dngpi,gh->dngihp", m, eye)
        return out.reshape(ndir, nblk, gpb * i_sz, gpb * p)

    def blockdiag_out(m):
        m = m.reshape(ndir, nblk, gpb, i_sz, p)
        out = jnp.einsum("dngip,gh->dngphi", m, eye)
        return out.reshape(ndir, nblk, gpb * p, gpb * i_sz)

    bm = jnp.concatenate([blockdiag_in(jnp.real(b_bar)), blockdiag_in(jnp.imag(b_bar))], axis=-1).astype(BF16)
    cm = jnp.concatenate([blockdiag_out(jnp.real(c_mat)), blockdiag_out(-jnp.imag(c_mat))], axis=-2).astype(BF16)
    lam_t = jnp.stack([jnp.real(lam_bar), jnp.imag(lam_bar)], axis=1)
    lam_t = lam_t.reshape(ndir, 2, nblk, gpb * p // LANE, LANE).transpose(0, 2, 1, 3, 4).astype(F32)
    return bm, cm, lam_t


def _key_major(a, axis):
    shp = a.shape
    hn = shp[axis] // HEAD
    a = a.reshape(shp[:axis] + (hn, HEAD) + shp[axis + 1:])
    a = jnp.swapaxes(a, axis, axis + 1)
    return a.reshape(shp)


def kernel(x, c, ctx, c_ctx, ada_down, ada_up, ada_bias, ln_g, ln_b, rw_mix, rw_wr, rw_wk, rw_wv, rw_wo, rw_w0, rw_w1, rw_w2, rw_a0, rw_a1, rw_a2, rw_v0, rw_v1, rw_v2, rw_g1, rw_g2, rw_kk, rw_ka, rw_rk, rw_lnx_g, rw_lnx_b, s5_lam_re, s5_lam_im, s5_log_dt, s5_b_re, s5_b_im, s5_c_re, s5_c_im, s5_d, s5_glu_w, s5_glu_b, mlp_w1, mlp_w2):
    bsz, seq, d = x.shape
    lc = ctx.shape[1]
    depth = ada_down.shape[0]
    assert bsz == 2 and d % S5_CH == 0
    alpha = (2 * depth) ** 0.25
    n_lat = bsz * seq
    tr = _pick(math.gcd(seq, bsz * lc), (512, 256, 128, 64))
    tiles_per_batch = seq // tr

    def gmap(i):
        return jnp.minimum(i // tiles_per_batch, bsz)

    g_t = math.gcd(seq, lc)
    dims = {"B": bsz, "L": seq, "Lc": lc, "n_lat": n_lat, "tr": tr, "gmap": gmap,
            "tcs": _pick(g_t, (256, 128, 64, 32, 16, 8)),
            "tf": _pick(g_t, (32, 16, 8)), "to": _pick(g_t, (64, 32, 16, 8))}
    tc = _pick(g_t, (32, 16, 8))

    xs = jnp.concatenate([x.reshape(n_lat, d), ctx.reshape(bsz * lc, d)], axis=0).astype(F32)
    vf = None

    cvec = jnp.concatenate([c, c_ctx[None, :]], axis=0)
    cvec = _pad_to(jax.nn.silu(cvec), 0, 16).astype(BF16)

    for i in range(depth):
        low = _mm(cvec, ada_down[i].astype(BF16), out_dtype=BF16, name="adaln_down")
        mod = _mm(low, ada_up[i].astype(BF16), bias=ada_bias[i], name="adaln_up")
        mod = mod[:bsz + 1].reshape(bsz + 1, 6, d)
        j = i // 2
        if i % 2 == 0:
            km = _key_major
            p = {"w2": km(_pad_to(rw_w2[j], 1, LANE).astype(BF16), 2),
                 "a2": km(_pad_to(rw_a2[j], 1, LANE).astype(BF16), 2),
                 "w0": km(rw_w0[j], 1), "a0": km(rw_a0[j], 1), "k_k": km(rw_kk[j], 0),
                 "k_a": km(rw_ka[j], 0), "r_k": km(rw_rk[j].reshape(-1), 0)}
            lerps = _rwkv_prep(xs, mod, rw_mix[j], dims)
            r = _mm(lerps[0], km(rw_wr[j].astype(BF16), 1), name="rwkv_r")
            k = _mm(lerps[2], km(rw_wk[j].astype(BF16), 1), name="rwkv_k")
            v = _mm(lerps[3], km(rw_wv[j].astype(BF16), 1), name="rwkv_v")
            w1cat = jnp.concatenate([_pad_to(rw_w1[j, dd], 1, LANE) for dd in range(2)], axis=1)
            a1cat = jnp.concatenate([_pad_to(rw_a1[j, dd], 1, LANE) for dd in range(2)], axis=1)
            tw = _mm(lerps[1], w1cat.astype(BF16), act="tanh", out_dtype=BF16, name="rwkv_w1")
            ax = _mm(lerps[4], a1cat.astype(BF16), out_dtype=BF16, name="rwkv_a1")
            sg = _mm(lerps[5], _pad_to(rw_g1[j], 1, LANE).astype(BF16), act="sigmoid", out_dtype=BF16,
                     name="rwkv_g1")
            g2 = km(_pad_to(rw_g2[j], 0, LANE).astype(BF16), 1)
            xv = None
            if j > 0:
                xv = _mm(lerps[3], _pad_to(rw_v1[j - 1], 1, LANE).astype(BF16), out_dtype=BF16, name="rwkv_v1")
                p["v2"] = km(_pad_to(rw_v2[j - 1], 0, LANE).astype(BF16), 1)
                p["v0"] = km(rw_v0[j - 1], 0)
            feats = _rwkv_features(r, k, v, tw, ax, p, vf if j > 0 else None, xv, dims)
            kk_s, dec0, dec1, kd0, kd1, b0, b1, r_s, v_s, bonus = feats
            if j == 0:
                vf = v
            n_ctx_chunks = lc // tc
            y = _rwkv_scan(r_s, dec0, kd0, kk_s, b0, v_s, None, False, n_ctx_chunks, tc)
            y = _rwkv_scan(r_s, dec1, kd1, kk_s, b1, v_s, y, True, n_ctx_chunks, tc)
            yg = _rwkv_readout(y, bonus, sg, g2, km(rw_lnx_g[j], 0), km(rw_lnx_b[j], 0), dims)
            mix_out = _mm(yg, km(rw_wo[j].astype(BF16), 0), name="rwkv_o")
        else:
            gpb = S5_CH // S5_GROUP
            bm, cm, lam_t = _s5_params(s5_lam_re[j], s5_lam_im[j], s5_log_dt[j], s5_b_re[j], s5_b_im[j],
                                       s5_c_re[j], s5_c_im[j], gpb)
            h = _modulate(xs, mod, dims)
            yf = _s5_scan(h, bm[0], cm[0], lam_t[0], False, dims)
            yr = _s5_scan(h, bm[1], cm[1], lam_t[1], True, dims)
            gl = _s5_finish(yf, yr, h, s5_d[j])
            mix_out = _mm(gl, s5_glu_w[j].astype(BF16), bias=s5_glu_b[j], glu=True,
                          tm=_pick(gl.shape[0], (768, 512, 256, 128, 64, 32, 16, 8)), name="s5_glu")
        xs, h2 = _resid_ln(xs, mix_out, mod, ln_g[i, 0], ln_b[i, 0], alpha, 2, (3, 4), dims)
        a1 = _mm(h2, mlp_w1[i].astype(BF16), act="relu2", out_dtype=BF16, name="mlp_w1")
        mlp_out = _mm(a1, mlp_w2[i].astype(BF16), tn=_pick(d, (1024, 512, 256, 128)), name="mlp_w2")
        (xs,) = _resid_ln(xs, mlp_out, mod, ln_g[i, 1], ln_b[i, 1], alpha, 5, None, dims)
    return xs[:n_lat].reshape(bsz, seq, d).astype(x.dtype)
```

```python
import functools
import math

import jax
import jax.numpy as jnp
from jax import lax
from jax.experimental import pallas as pl
from jax.experimental.pallas import tpu as pltpu

F32 = jnp.float32
BF16 = jnp.bfloat16

GRID_W = 64
HEAD = 64
S5_GROUP = 16
S5_STATE = 64
S5_CH = 256
LN_EPS = 1e-6
GN_EPS = 64e-5
LANE = 128
SUBLANE = 8
VROWS = 32
PITCH = HEAD + SUBLANE
VMEM_LIMIT = 56 * 1024 * 1024


def _cparams(*sem):
    return pltpu.CompilerParams(dimension_semantics=sem, vmem_limit_bytes=VMEM_LIMIT)


def _pick(n, prefs):
    for p in prefs:
        if n % p == 0:
            return p
    return n


def _pad_to(a, axis, mult):
    n = a.shape[axis]
    r = (-n) % mult
    if r == 0:
        return a
    pad = [(0, 0)] * a.ndim
    pad[axis] = (0, r)
    return jnp.pad(a, pad)


def _mm_kernel(*refs, nk, act, has_bias, glu):
    refs = list(refs)
    x_ref = refs.pop(0)
    w_refs = [refs.pop(0) for _ in range(2 if glu else 1)]
    b_refs = [refs.pop(0) for _ in range((2 if glu else 1) if has_bias else 0)]
    o_ref = refs.pop(0)
    acc_refs = refs
    k = pl.program_id(2)

    def finish(zs):
        if has_bias:
            zs = [z + b[...] for z, b in zip(zs, b_refs)]
        if glu:
            z = zs[0] * jax.nn.sigmoid(zs[1])
        else:
            z = zs[0]
            if act == "tanh":
                z = jnp.tanh(z)
            elif act == "sigmoid":
                z = jax.nn.sigmoid(z)
            elif act == "relu2":
                z = jnp.square(jnp.maximum(z, 0.0))
        o_ref[...] = z.astype(o_ref.dtype)

    x = x_ref[...]
    parts = [jnp.dot(x, w[...], preferred_element_type=F32) for w in w_refs]
    if nk == 1:
        finish(parts)
        return

    @pl.when(k == 0)
    def _():
        for a in acc_refs:
            a[...] = jnp.zeros_like(a)

    for a, p in zip(acc_refs, parts):
        a[...] += p

    @pl.when(k == nk - 1)
    def _():
        finish([a[...] for a in acc_refs])


def _mm(x, w, bias=None, act=None, out_dtype=F32, glu=False, tm=None, tn=None, tk=None, name="mm"):
    m, kdim = x.shape
    n = w.shape[1] // (2 if glu else 1)
    tm = tm or _pick(m, (1536, 1024, 768, 512, 256, 128, 64, 32, 16, 8))
    tn = tn or _pick(n, (512, 256, 128))
    tk = tk or (kdim if kdim <= 4096 else _pick(kdim, (2048, 1024, 512, 256, 128)))
    nk = kdim // tk
    nj = n // tn
    in_specs = [pl.BlockSpec((tm, tk), lambda i, j, k: (i, k)),
                pl.BlockSpec((tk, tn), lambda i, j, k: (k, j))]
    args = [x, w]
    if glu:
        in_specs.append(pl.BlockSpec((tk, tn), lambda i, j, k: (k, j + nj)))
        args.append(w)
    if bias is not None:
        b2 = bias.reshape(1, -1).astype(F32)
        in_specs.append(pl.BlockSpec((1, tn), lambda i, j, k: (0, j)))
        args.append(b2)
        if glu:
            in_specs.append(pl.BlockSpec((1, tn), lambda i, j, k: (0, j + nj)))
            args.append(b2)
    scratch = [] if nk == 1 else [pltpu.VMEM((tm, tn), F32) for _ in range(2 if glu else 1)]
    return pl.pallas_call(
        functools.partial(_mm_kernel, nk=nk, act=act, has_bias=bias is not None, glu=glu),
        grid=(m // tm, nj, nk),
        in_specs=in_specs,
        out_specs=pl.BlockSpec((tm, tn), lambda i, j, k: (i, j)),
        out_shape=jax.ShapeDtypeStruct((m, n), out_dtype),
        scratch_shapes=scratch,
        compiler_params=_cparams("parallel", "parallel", "arbitrary"),
        name=name,
    )(*args)


def _prep_kernel(xp_ref, xc_ref, xn_ref, mod_ref, mix_ref, *rest, tr, n_lat, l_img, l_ctx):
    o_refs = rest[:6]
    hbuf, sbuf = rest[6], rest[7]
    i = pl.program_id(0)
    j = pl.program_id(1)
    sh = mod_ref[0, 0:1, :]
    sc = 1.0 + mod_ref[0, 1:2, :]
    hbuf[0:GRID_W, :] = xp_ref[...] * sc + sh
    hbuf[GRID_W:GRID_W + tr, :] = xc_ref[...] * sc + sh
    hbuf[GRID_W + tr:GRID_W + tr + GRID_W, :] = xn_ref[...] * sc + sh
    row = lax.broadcasted_iota(jnp.int32, (tr, 1), 0) + i * tr
    t_lat = row % l_img
    is_lat = i < n_lat // tr

    def shifted(s, keep):
        sbuf[...] = jnp.where(keep, hbuf[GRID_W + s:GRID_W + s + tr, :], 0.0)

    @pl.when(jnp.logical_and(is_lat, j == 0))
    def _():
        shifted(-1, t_lat % GRID_W != 0)

    @pl.when(jnp.logical_and(is_lat, j == 1))
    def _():
        shifted(1, t_lat % GRID_W != GRID_W - 1)

    @pl.when(jnp.logical_and(is_lat, j == 2))
    def _():
        shifted(-GRID_W, t_lat >= GRID_W)

    @pl.when(jnp.logical_and(is_lat, j == 3))
    def _():
        shifted(GRID_W, t_lat < l_img - GRID_W)

    t_ctx = (row - n_lat) % l_ctx

    @pl.when(jnp.logical_and(jnp.logical_not(is_lat), j < 2))
    def _():
        shifted(-1, t_ctx != 0)

    @pl.when(jnp.logical_and(jnp.logical_not(is_lat), j >= 2))
    def _():
        shifted(1, t_ctx != l_ctx - 1)

    h = hbuf[GRID_W:GRID_W + tr, :]
    xx = sbuf[...] - h
    for m in range(6):
        o_refs[m][...] = (h + xx * mix_ref[m:m + 1, :]).astype(BF16)


def _rwkv_prep(x, mod, mix, dims):
    r, d = x.shape
    tr = dims["tr"]
    dc = d // 4
    nb = tr // GRID_W
    last = r // GRID_W - 1
    kern = functools.partial(_prep_kernel, tr=tr, n_lat=dims["n_lat"], l_img=dims["L"], l_ctx=dims["Lc"])
    gmap = dims["gmap"]
    outs = pl.pallas_call(
        kern,
        grid=(r // tr, 4),
        in_specs=[
            pl.BlockSpec((GRID_W, dc), lambda i, j: (jnp.maximum(i * nb - 1, 0), j)),
            pl.BlockSpec((tr, dc), lambda i, j: (i, j)),
            pl.BlockSpec((GRID_W, dc), lambda i, j: (jnp.minimum(i * nb + nb, last), j)),
            pl.BlockSpec((1, 6, dc), lambda i, j: (gmap(i), 0, j)),
            pl.BlockSpec((6, dc), lambda i, j: (0, j)),
        ],
        out_specs=[pl.BlockSpec((tr, dc), lambda i, j: (i, j)) for _ in range(6)],
        out_shape=[jax.ShapeDtypeStruct((r, d), BF16) for _ in range(6)],
        scratch_shapes=[pltpu.VMEM((tr + 2 * GRID_W, dc), F32), pltpu.VMEM((tr, dc), F32)],
        compiler_params=_cparams("parallel", "parallel"),
        name="rwkv_prep",
    )(x, x, x, mod, mix)
    return outs


def _fold_heads(x, cl, h):
    acc = x[:, 0:cl]
    for m in range(1, x.shape[1] // cl):
        acc = acc + x[:, m * cl:(m + 1) * cl]
    return acc + pltpu.roll(acc, h, 1)


def _store_scan(o_ref, z0, z1, cl, h):
    tr = z0.shape[0]
    lo = lax.broadcasted_iota(jnp.int32, (tr, cl), 1) < h
    for m in range(z0.shape[1] // cl):
        c0 = z0[:, m * cl:(m + 1) * cl]
        c1 = z1[:, m * cl:(m + 1) * cl]
        o_ref[pl.ds(2 * m, tr, stride=PITCH), :] = jnp.where(lo, c0, pltpu.roll(c1, h, 1))
        o_ref[pl.ds(2 * m + 1, tr, stride=PITCH), :] = jnp.where(lo, pltpu.roll(c0, h, 1), c1)


def _feat_kernel(*refs, has_vf, rw, ra, tr, cl, h):
    refs = list(refs)

    def pair():
        return [refs.pop(0), refs.pop(0)]

    r_refs, k_refs, v_refs, tw_refs, ax_refs = pair(), pair(), pair(), pair(), pair()
    w2_ref, a2_ref, w0_ref, a0_ref, vec_ref = [refs.pop(0) for _ in range(5)]
    if has_vf:
        vf_refs, xv_refs = pair(), pair()
        v2_ref, v0_ref = refs.pop(0), refs.pop(0)
    kk_o, dec0_o, dec1_o, kd0_o, kd1_o, b0_o, b1_o, r_o, v_o, bonus_o = refs

    def both(pr):
        return jnp.concatenate([pr[0][...], pr[1][...]], axis=0)

    def put(o_ref, z):
        _store_scan(o_ref, z[:tr], z[tr:], cl, h)

    def tiled(s):
        return jnp.concatenate([s] * (k.shape[1] // cl), axis=1)

    k = both(k_refs)
    r = both(r_refs)
    v = both(v_refs)
    tw = both(tw_refs)
    ax = both(ax_refs)
    k_k = vec_ref[0:1, :]
    k_a = vec_ref[1:2, :]
    r_k = vec_ref[2:3, :]
    kraw = k * k_k
    nrm = jnp.sqrt(_fold_heads(kraw * kraw, cl, h))
    kk = kraw * tiled(1.0 / jnp.maximum(nrm, 1e-12))
    put(kk_o, kk)
    put(r_o, r)
    if has_vf:
        vl = v0_ref[...] + jnp.dot(both(xv_refs), v2_ref[...], preferred_element_type=F32)
        v = v + (both(vf_refs) - v) * jax.nn.sigmoid(vl)
    put(v_o, v)
    kd_sum = None
    for d, (dec_o, kd_o, b_o) in enumerate(((dec0_o, kd0_o, b0_o), (dec1_o, kd1_o, b1_o))):
        wl = w0_ref[d:d + 1, :] + jnp.dot(tw[:, d * rw:(d + 1) * rw], w2_ref[d], preferred_element_type=F32)
        nwl = -wl
        softplus = jnp.maximum(nwl, 0.0) + jnp.log(1.0 + jnp.exp(-jnp.abs(nwl)))
        put(dec_o, jnp.exp(-jnp.exp(-softplus - 0.5)))
        al = a0_ref[d:d + 1, :] + jnp.dot(ax[:, d * ra:(d + 1) * ra], a2_ref[d], preferred_element_type=F32)
        a = jax.nn.sigmoid(al)
        kd = k * (1.0 + (a - 1.0) * k_a)
        put(kd_o, kd)
        put(b_o, -(kk * a))
        kd_sum = kd if kd_sum is None else kd_sum + kd
    put(bonus_o, tiled(_fold_heads(r * kd_sum * r_k, cl, h)) * v)


def _rwkv_features(r, k, v, tw, ax, p, vf, xv, dims):
    rows, d = r.shape
    bsz, l, lc, n_lat = dims["B"], dims["L"], dims["Lc"], dims["n_lat"]
    tr = dims["tf"]
    t_total = l + lc
    h = d // HEAD
    cl = bsz * h
    has_vf = vf is not None
    rw = p["w2"].shape[1]
    ra = p["a2"].shape[1]
    nci = lc // tr

    def rblk(b):
        return lambda i: (jnp.where(i < nci, (n_lat + b * lc) // tr + i, b * (l // tr) + i - nci), 0)

    def per_batch(a):
        return [pl.BlockSpec((tr, a.shape[1]), rblk(b)) for b in range(bsz)]

    def whole(a):
        nd = a.ndim
        return pl.BlockSpec(a.shape, lambda i: (0,) * nd)

    vec = jnp.stack([p["k_k"], p["k_a"], p["r_k"]]).astype(F32)
    args, in_specs = [], []
    for a in (r, k, v, tw, ax):
        args += [a, a]
        in_specs += per_batch(a)
    for a in (p["w2"], p["a2"], p["w0"], p["a0"], vec):
        args.append(a)
        in_specs.append(whole(a))
    if has_vf:
        for a in (vf, xv):
            args += [a, a]
            in_specs += per_batch(a)
        v0 = p["v0"].reshape(1, -1)
        args += [p["v2"], v0]
        in_specs += [whole(p["v2"]), whole(v0)]
    out_blk = pl.BlockSpec((tr * PITCH, cl), lambda i: (i, 0))
    outs = pl.pallas_call(
        functools.partial(_feat_kernel, has_vf=has_vf, rw=rw, ra=ra, tr=tr, cl=cl, h=h),
        grid=(t_total // tr,),
        in_specs=in_specs,
        out_specs=[out_blk] * 10,
        out_shape=[jax.ShapeDtypeStruct((t_total * PITCH, cl), F32)] * 10,
        compiler_params=_cparams("parallel"),
        name="rwkv_features",
    )(*args)
    return [o.reshape(t_total, PITCH, cl) for o in outs]


def _scan_kernel(*refs, tc, reverse, has_prev):
    refs = list(refs)
    r_ref, w_ref, k_ref, a_ref, nb_ref, v_ref = [refs.pop(0) for _ in range(6)]
    yp_ref = refs.pop(0) if has_prev else None
    y_ref, s_ref = refs
    lanes = y_ref.shape[2]

    @pl.when(pl.program_id(0) == 0)
    def _():
        s_ref[...] = jnp.zeros_like(s_ref)

    y_ref[:, HEAD:, :] = jnp.zeros((tc, y_ref.shape[1] - HEAD, lanes), F32)

    def t_of(tt):
        return tc - 1 - tt if reverse else tt

    def key_row(ref, t, ki):
        return jnp.broadcast_to(ref[t, ki:ki + 1, :], (VROWS, lanes))

    groups = range(0, HEAD, VROWS)

    sa0 = []
    for vs in groups:
        acc = None
        for ki in range(HEAD):
            p = s_ref[ki, vs:vs + VROWS, :] * key_row(a_ref, t_of(0), ki)
            acc = p if acc is None else acc + p
        sa0.append(acc)

    def step(tt, sa):
        t = t_of(tt)
        t_next = t_of(jnp.minimum(tt + 1, tc - 1))
        sa_next = []
        for g, vs in enumerate(groups):
            vt = v_ref[t, vs:vs + VROWS, :]
            y = yp_ref[t, vs:vs + VROWS, :] if has_prev else None
            acc = None
            for ki in range(HEAD):
                s = s_ref[ki, vs:vs + VROWS, :] * key_row(w_ref, t, ki) + sa[g] * key_row(nb_ref, t, ki) \
                    + vt * key_row(k_ref, t, ki)
                s_ref[ki, vs:vs + VROWS, :] = s
                q = s * key_row(r_ref, t, ki)
                y = q if y is None else y + q
                p = s * key_row(a_ref, t_next, ki)
                acc = p if acc is None else acc + p
            y_ref[t, vs:vs + VROWS, :] = y
            sa_next.append(acc)
        return tuple(sa_next)

    lax.fori_loop(0, tc, step, tuple(sa0))


def _rwkv_scan(r, w, k, kk, b, v, y_prev, reverse, n_ctx_chunks, tc):
    t_total, pitch, lanes = r.shape
    nc = t_total // tc
    if reverse:
        def cmap(c):
            return jnp.where(c < n_ctx_chunks, n_ctx_chunks - 1 - c, nc - 1 - (c - n_ctx_chunks))
    else:
        def cmap(c):
            return c
    blk = pl.BlockSpec((tc, pitch, lanes), lambda c: (cmap(c), 0, 0))
    args = [r, w, k, kk, b, v]
    if y_prev is not None:
        args.append(y_prev)
    return pl.pallas_call(
        functools.partial(_scan_kernel, tc=tc, reverse=reverse, has_prev=y_prev is not None),
        grid=(nc,),
        in_specs=[blk] * len(args),
        out_specs=blk,
        out_shape=jax.ShapeDtypeStruct((t_total, pitch, lanes), F32),
        scratch_shapes=[pltpu.VMEM((HEAD, HEAD, lanes), F32)],
        compiler_params=_cparams("arbitrary"),
        name="rwkv_scan_rev" if reverse else "rwkv_scan_fwd",
    )(*args)


def _readout_kernel(y_ref, bonus_ref, sg_ref, g2_ref, lnx_ref, o_ref, *, tr, cl, h):
    is_b0 = pl.program_id(0) == 0
    nv = HEAD
    inv = 1.0 / nv

    def rows(ref, m):
        return ref[pl.ds(m, tr, stride=PITCH), :]

    acc = rows(y_ref, 0)
    for m in range(1, nv):
        acc = acc + rows(y_ref, m)
    mu = acc * inv
    acc = None
    for m in range(nv):
        dlt = rows(y_ref, m) - mu
        acc = dlt * dlt if acc is None else acc + dlt * dlt
    rs = lax.rsqrt(acc * inv + GN_EPS)
    g = jnp.dot(sg_ref[...], g2_ref[...], preferred_element_type=F32)
    lo = lax.broadcasted_iota(jnp.int32, (tr, cl), 1) < h

    def normed(m):
        return (rows(y_ref, m) - mu) * rs * lnx_ref[0:1, m * cl:(m + 1) * cl] \
            + lnx_ref[1:2, m * cl:(m + 1) * cl] + rows(bonus_ref, m)

    for m in range(nv // 2):
        ya = normed(2 * m)
        yb = normed(2 * m + 1)
        first = jnp.where(is_b0, ya, pltpu.roll(ya, h, 1))
        second = jnp.where(is_b0, pltpu.roll(yb, h, 1), yb)
        out = jnp.where(lo, first, second) * g[:, m * cl:(m + 1) * cl]
        o_ref[:, m * cl:(m + 1) * cl] = out.astype(o_ref.dtype)


def _rwkv_readout(y, bonus, sg, g2, lnx_g, lnx_b, dims):
    t_total, pitch, cl = y.shape
    bsz, l, lc, n_lat = dims["B"], dims["L"], dims["Lc"], dims["n_lat"]
    h = cl // bsz
    d = h * HEAD
    width = bsz * d
    tr = dims["to"]
    y = y.reshape(t_total * pitch, cl)
    bonus = bonus.reshape(t_total * pitch, cl)
    nci = lc // tr

    def rblk(b, i):
        return jnp.where(i < nci, (n_lat + b * lc) // tr + i, b * (l // tr) + i - nci)

    def spread(vec):
        return jnp.broadcast_to(vec.reshape(HEAD, 1, h), (HEAD, bsz, h)).reshape(1, width)

    lnx = jnp.concatenate([spread(lnx_g), spread(lnx_b)], axis=0).astype(F32)
    scan_blk = pl.BlockSpec((tr * pitch, cl), lambda b, i: (i, 0))
    return pl.pallas_call(
        functools.partial(_readout_kernel, tr=tr, cl=cl, h=h),
        grid=(bsz, t_total // tr),
        in_specs=[scan_blk, scan_blk,
                  pl.BlockSpec((tr, sg.shape[1]), lambda b, i: (rblk(b, i), 0)),
                  pl.BlockSpec(g2.shape, lambda b, i: (0, 0)),
                  pl.BlockSpec((2, width), lambda b, i: (0, 0))],
        out_specs=pl.BlockSpec((tr, d), lambda b, i: (rblk(b, i), 0)),
        out_shape=jax.ShapeDtypeStruct((bsz * t_total, d), BF16),
        compiler_params=_cparams("parallel", "parallel"),
        name="rwkv_readout",
    )(y, bonus, sg, g2, lnx)


def _resid_ln_kernel(x_ref, y_ref, mod_ref, ln_ref, xo_ref, *rest, alpha, gate_row, mod_rows):
    gate = mod_ref[0, gate_row:gate_row + 1, :]
    z = alpha * x_ref[...] + gate * y_ref[...]
    mu = jnp.mean(z, axis=-1, keepdims=True)
    dz = z - mu
    var = jnp.mean(dz * dz, axis=-1, keepdims=True)
    zn = dz * lax.rsqrt(var + LN_EPS) * ln_ref[0:1, :] + ln_ref[1:2, :]
    xo_ref[...] = zn
    if mod_rows is not None:
        sh = mod_ref[0, mod_rows[0]:mod_rows[0] + 1, :]
        sc = mod_ref[0, mod_rows[1]:mod_rows[1] + 1, :]
        rest[0][...] = (zn * (1.0 + sc) + sh).astype(rest[0].dtype)


def _resid_ln(x, y, mod, ln_g, ln_b, alpha, gate_row, mod_rows, dims, rows=None):
    d = x.shape[1]
    rows = rows or x.shape[0]
    tr = _pick(dims["tr"], (256, 128, 64, 32, 16, 8))
    gmap = dims["gmap"]
    ratio = dims["tr"] // tr
    tile = pl.BlockSpec((tr, d), lambda i: (i, 0))
    ln = jnp.stack([ln_g, ln_b]).astype(F32)
    out_shape = [jax.ShapeDtypeStruct((rows, d), F32)]
    out_specs = [tile]
    if mod_rows is not None:
        out_shape.append(jax.ShapeDtypeStruct((rows, d), BF16))
        out_specs.append(tile)
    outs = pl.pallas_call(
        functools.partial(_resid_ln_kernel, alpha=alpha, gate_row=gate_row, mod_rows=mod_rows),
        grid=(rows // tr,),
        in_specs=[tile, tile,
                  pl.BlockSpec((1, 6, d), lambda i: (gmap(i // ratio), 0, 0)),
                  pl.BlockSpec((2, d), lambda i: (0, 0))],
        out_specs=out_specs,
        out_shape=out_shape,
        compiler_params=_cparams("parallel"),
        name="resid_ln",
    )(x, y, mod, ln)
    return outs


def _s5_kernel(*refs, tcs, nq, ns, finish):
    refs = list(refs)
    x_ref, mod_ref, bm_ref, cm_ref, lam_ref = [refs.pop(0) for _ in range(5)]
    if finish:
        yf_ref, d_ref = refs.pop(0), refs.pop(0)
    y_ref, hre_ref, him_ref, h2d_ref, cr_ref, ci_ref = refs
    reverse = finish
    nsl = ns // LANE

    @pl.when(pl.program_id(2) == 0)
    def _():
        cr_ref[...] = jnp.zeros_like(cr_ref)
        ci_ref[...] = jnp.zeros_like(ci_ref)

    def u_of(q):
        sl = slice(q * S5_CH, (q + 1) * S5_CH)
        return x_ref[:, sl] * (1.0 + mod_ref[0, 1:2, sl]) + mod_ref[0, 0:1, sl]

    for q in range(nq):
        bu = jnp.dot(u_of(q).astype(BF16), bm_ref[q], preferred_element_type=F32)
        for s in range(nsl):
            hre_ref[q, pl.ds(s, tcs, stride=nsl), :] = bu[:, s * LANE:(s + 1) * LANE]
            him_ref[q, pl.ds(s, tcs, stride=nsl), :] = bu[:, ns + s * LANE:ns + (s + 1) * LANE]

    lr = [lam_ref[q, 0] for q in range(nq)]
    li = [lam_ref[q, 1] for q in range(nq)]

    tb = 8
    ntrip = tcs // tb

    def trip(gi, carry):
        g = ntrip - 1 - gi if reverse else gi
        base = pl.multiple_of(g * (tb * nsl), tb * nsl)
        carry = list(carry)
        for j in range(tb):
            off = base + (tb - 1 - j if reverse else j) * nsl
            for q in range(nq):
                hr, hi = carry[2 * q], carry[2 * q + 1]
                nr = lr[q] * hr - li[q] * hi + hre_ref[q, pl.ds(off, nsl), :]
                ni = lr[q] * hi + li[q] * hr + him_ref[q, pl.ds(off, nsl), :]
                hre_ref[q, pl.ds(off, nsl), :] = nr
                him_ref[q, pl.ds(off, nsl), :] = ni
                carry[2 * q], carry[2 * q + 1] = nr, ni
        return tuple(carry)

    init = []
    for q in range(nq):
        init += [cr_ref[q], ci_ref[q]]
    fin = lax.fori_loop(0, ntrip, trip, tuple(init))
    for q in range(nq):
        cr_ref[q] = fin[2 * q]
        ci_ref[q] = fin[2 * q + 1]

    for q in range(nq):
        for s in range(nsl):
            h2d_ref[:, s * LANE:(s + 1) * LANE] = hre_ref[q, pl.ds(s, tcs, stride=nsl), :].astype(BF16)
            h2d_ref[:, ns + s * LANE:ns + (s + 1) * LANE] = him_ref[q, pl.ds(s, tcs, stride=nsl), :].astype(BF16)
        sl = slice(q * S5_CH, (q + 1) * S5_CH)
        y = jnp.dot(h2d_ref[...], cm_ref[q], preferred_element_type=F32)
        if finish:
            y = y + yf_ref[:, sl] + d_ref[:, sl] * u_of(q)
            c = math.sqrt(2.0 / math.pi)
            y = 0.5 * y * (1.0 + jnp.tanh(c * (y + 0.044715 * (y * y * y))))
        y_ref[:, sl] = y.astype(y_ref.dtype)


def _s5_scan(x, mod, bm, cm, lam, y_fwd, dvec, dims):
    reverse = y_fwd is not None
    rows, d = x.shape
    tcs = dims["tcs"]
    ns2 = bm.shape[2]
    ns = ns2 // 2
    nq = _pick(d // S5_CH, (4, 2, 1))
    cbw = nq * S5_CH
    n_lat_blk = dims["L"] // tcs
    n_ctx_blk = dims["Lc"] // tcs
    n_lat_rows_blk = dims["n_lat"] // tcs
    nchunk = n_ctx_blk + n_lat_blk

    def rmap(b, c):
        if reverse:
            ctx_blk = n_lat_rows_blk + b * n_ctx_blk + (n_ctx_blk - 1 - c)
            lat_blk = b * n_lat_blk + (n_lat_blk - 1 - (c - n_ctx_blk))
        else:
            ctx_blk = n_lat_rows_blk + b * n_ctx_blk + c
            lat_blk = b * n_lat_blk + (c - n_ctx_blk)
        return jnp.where(c < n_ctx_blk, ctx_blk, lat_blk)

    bsz = dims["B"]
    tile = pl.BlockSpec((tcs, cbw), lambda b, g, c: (rmap(b, c), g))
    in_specs = [tile,
                pl.BlockSpec((1, 6, cbw), lambda b, g, c: (jnp.where(c < n_ctx_blk, bsz, b), 0, g)),
                pl.BlockSpec((nq, S5_CH, ns2), lambda b, g, c: (g, 0, 0)),
                pl.BlockSpec((nq, ns2, S5_CH), lambda b, g, c: (g, 0, 0)),
                pl.BlockSpec((nq, 2, ns // LANE, LANE), lambda b, g, c: (g, 0, 0, 0))]
    args = [x, mod, bm, cm, lam]
    if reverse:
        in_specs += [tile, pl.BlockSpec((1, cbw), lambda b, g, c: (0, g))]
        args += [y_fwd, dvec.reshape(1, -1).astype(F32)]
    return pl.pallas_call(
        functools.partial(_s5_kernel, tcs=tcs, nq=nq, ns=ns, finish=reverse),
        grid=(bsz, d // cbw, nchunk),
        in_specs=in_specs,
        out_specs=tile,
        out_shape=jax.ShapeDtypeStruct((rows, d), BF16 if reverse else F32),
        scratch_shapes=[pltpu.VMEM((nq, tcs * ns // LANE, LANE), F32),
                        pltpu.VMEM((nq, tcs * ns // LANE, LANE), F32),
                        pltpu.VMEM((tcs, ns2), BF16),
                        pltpu.VMEM((nq, ns // LANE, LANE), F32),
                        pltpu.VMEM((nq, ns // LANE, LANE), F32)],
        compiler_params=_cparams("parallel", "parallel", "arbitrary"),
        name="s5_scan_rev" if reverse else "s5_scan_fwd",
    )(*args)


def _s5_params(lam_re, lam_im, log_dt, b_re, b_im, c_re, c_im, gpb):
    lam = lax.complex(lam_re.astype(F32), lam_im.astype(F32))
    dt = jnp.exp(log_dt.astype(F32))[..., None]
    lam_bar = jnp.exp(lam * dt)
    b_bar = ((lam_bar - 1.0) / lam)[..., None] * lax.complex(b_re.astype(F32), b_im.astype(F32))
    c_mat = lax.complex(c_re.astype(F32), c_im.astype(F32))
    ndir, g, p = lam_bar.shape
    i_sz = b_bar.shape[-1]
    nblk = g // gpb
    eye = jnp.eye(gpb, dtype=F32)

    def blockdiag_in(m):
        m = m.reshape(ndir, nblk, gpb, p, i_sz)
        out = jnp.einsum("dngpi,gh->dngihp", m, eye)
        return out.reshape(ndir, nblk, gpb * i_sz, gpb * p)

    def blockdiag_out(m):
        m = m.reshape(ndir, nblk, gpb, i_sz, p)
        out = jnp.einsum("dngip,gh->dngphi", m, eye)
        return out.reshape(ndir, nblk, gpb * p, gpb * i_sz)

    bm = jnp.concatenate([blockdiag_in(jnp.real(b_bar)), blockdiag_in(jnp.imag(b_bar))], axis=-1).astype(BF16)
    cm = jnp.concatenate([blockdiag_out(jnp.real(c_mat)), blockdiag_out(-jnp.imag(c_mat))], axis=-2).astype(BF16)
    lam_t = jnp.stack([jnp.real(lam_bar), jnp.imag(lam_bar)], axis=1)
    lam_t = lam_t.reshape(ndir, 2, nblk, gpb * p // LANE, LANE).transpose(0, 2, 1, 3, 4).astype(F32)
    return bm, cm, lam_t


def _key_major(a, axis):
    shp = a.shape
    hn = shp[axis] // HEAD
    a = a.reshape(shp[:axis] + (hn, HEAD) + shp[axis + 1:])
    a = jnp.swapaxes(a, axis, axis + 1)
    return a.reshape(shp)


def kernel(x, c, ctx, c_ctx, ada_down, ada_up, ada_bias, ln_g, ln_b, rw_mix, rw_wr, rw_wk, rw_wv, rw_wo, rw_w0, rw_w1, rw_w2, rw_a0, rw_a1, rw_a2, rw_v0, rw_v1, rw_v2, rw_g1, rw_g2, rw_kk, rw_ka, rw_rk, rw_lnx_g, rw_lnx_b, s5_lam_re, s5_lam_im, s5_log_dt, s5_b_re, s5_b_im, s5_c_re, s5_c_im, s5_d, s5_glu_w, s5_glu_b, mlp_w1, mlp_w2):
    bsz, seq, d = x.shape
    lc = ctx.shape[1]
    depth = ada_down.shape[0]
    assert bsz == 2 and d % S5_CH == 0
    alpha = (2 * depth) ** 0.25
    n_lat = bsz * seq
    tr = _pick(math.gcd(seq, bsz * lc), (512, 256, 128, 64))
    tiles_per_batch = seq // tr

    def gmap(i):
        return jnp.minimum(i // tiles_per_batch, bsz)

    g_t = math.gcd(seq, lc)
    dims = {"B": bsz, "L": seq, "Lc": lc, "n_lat": n_lat, "tr": tr, "gmap": gmap,
            "tcs": _pick(g_t, (256, 128, 64, 32, 16, 8)),
            "tf": _pick(g_t, (32, 16, 8)), "to": _pick(g_t, (64, 32, 16, 8))}
    tc = _pick(g_t, (32, 16, 8))

    xs = jnp.concatenate([x.reshape(n_lat, d), ctx.reshape(bsz * lc, d)], axis=0).astype(F32)
    vf = None

    cvec = jnp.concatenate([c, c_ctx[None, :]], axis=0)
    cvec = _pad_to(jax.nn.silu(cvec), 0, 16).astype(BF16)

    for i in range(depth):
        low = _mm(cvec, ada_down[i].astype(BF16), out_dtype=BF16, name="adaln_down")
        mod = _mm(low, ada_up[i].astype(BF16), bias=ada_bias[i], name="adaln_up")
        mod = mod[:bsz + 1].reshape(bsz + 1, 6, d)
        j = i // 2
        if i % 2 == 0:
            km = _key_major
            p = {"w2": km(_pad_to(rw_w2[j], 1, LANE).astype(BF16), 2),
                 "a2": km(_pad_to(rw_a2[j], 1, LANE).astype(BF16), 2),
                 "w0": km(rw_w0[j], 1), "a0": km(rw_a0[j], 1), "k_k": km(rw_kk[j], 0),
                 "k_a": km(rw_ka[j], 0), "r_k": km(rw_rk[j].reshape(-1), 0)}
            lerps = _rwkv_prep(xs, mod, rw_mix[j], dims)
            r = _mm(lerps[0], km(rw_wr[j].astype(BF16), 1), name="rwkv_r")
            k = _mm(lerps[2], km(rw_wk[j].astype(BF16), 1), name="rwkv_k")
            v = _mm(lerps[3], km(rw_wv[j].astype(BF16), 1), name="rwkv_v")
            w1cat = jnp.concatenate([_pad_to(rw_w1[j, dd], 1, LANE) for dd in range(2)], axis=1)
            a1cat = jnp.concatenate([_pad_to(rw_a1[j, dd], 1, LANE) for dd in range(2)], axis=1)
            tw = _mm(lerps[1], w1cat.astype(BF16), act="tanh", out_dtype=BF16, name="rwkv_w1")
            ax = _mm(lerps[4], a1cat.astype(BF16), out_dtype=BF16, name="rwkv_a1")
            sg = _mm(lerps[5], _pad_to(rw_g1[j], 1, LANE).astype(BF16), act="sigmoid", out_dtype=BF16,
                     name="rwkv_g1")
            g2 = km(_pad_to(rw_g2[j], 0, LANE).astype(BF16), 1)
            xv = None
            if j > 0:
                xv = _mm(lerps[3], _pad_to(rw_v1[j - 1], 1, LANE).astype(BF16), out_dtype=BF16, name="rwkv_v1")
                p["v2"] = km(_pad_to(rw_v2[j - 1], 0, LANE).astype(BF16), 1)
                p["v0"] = km(rw_v0[j - 1], 0)
            feats = _rwkv_features(r, k, v, tw, ax, p, vf if j > 0 else None, xv, dims)
            kk_s, dec0, dec1, kd0, kd1, b0, b1, r_s, v_s, bonus = feats
            if j == 0:
                vf = v
            n_ctx_chunks = lc // tc
            y = _rwkv_scan(r_s, dec0, kd0, kk_s, b0, v_s, None, False, n_ctx_chunks, tc)
            y = _rwkv_scan(r_s, dec1, kd1, kk_s, b1, v_s, y, True, n_ctx_chunks, tc)
            yg = _rwkv_readout(y, bonus, sg, g2, km(rw_lnx_g[j], 0), km(rw_lnx_b[j], 0), dims)
            mix_out = _mm(yg, km(rw_wo[j].astype(BF16), 0), name="rwkv_o")
        else:
            gpb = S5_CH // S5_GROUP
            bm, cm, lam_t = _s5_params(s5_lam_re[j], s5_lam_im[j], s5_log_dt[j], s5_b_re[j], s5_b_im[j],
                                       s5_c_re[j], s5_c_im[j], gpb)
            yf = _s5_scan(xs, mod, bm[0], cm[0], lam_t[0], None, None, dims)
            gl = _s5_scan(xs, mod, bm[1], cm[1], lam_t[1], yf, s5_d[j], dims)
            mix_out = _mm(gl, s5_glu_w[j].astype(BF16), bias=s5_glu_b[j], glu=True,
                          tm=_pick(gl.shape[0], (768, 512, 256, 128, 64, 32, 16, 8)), name="s5_glu")
        xs, h2 = _resid_ln(xs, mix_out, mod, ln_g[i, 0], ln_b[i, 0], alpha, 2, (3, 4), dims)
        a1 = _mm(h2, mlp_w1[i].astype(BF16), act="relu2", out_dtype=BF16, name="mlp_w1")
        mlp_out = _mm(a1, mlp_w2[i].astype(BF16), tm=_pick(a1.shape[0], (768, 512, 256, 128, 64, 32, 16, 8)),
                      tn=_pick(d, (1024, 512, 256, 128)), tk=_pick(a1.shape[1], (4096, 2048, 1024, 512, 256, 128)),
                      name="mlp_w2")
        (xs,) = _resid_ln(xs, mlp_out, mod, ln_g[i, 1], ln_b[i, 1], alpha, 5, None, dims,
                          rows=n_lat if i == depth - 1 else None)
    return xs.reshape(bsz, seq, d).astype(x.dtype)
```

```python
import functools
import math

import jax
import jax.numpy as jnp
from jax import lax
from jax.experimental import pallas as pl
from jax.experimental.pallas import tpu as pltpu

F32 = jnp.float32
BF16 = jnp.bfloat16

GRID_W = 64
HEAD = 64
S5_GROUP = 16
S5_STATE = 64
S5_CH = 256
LN_EPS = 1e-6
GN_EPS = 64e-5
LANE = 128
SUBLANE = 8
MXU = 256
HPG = MXU // HEAD
GROUPS_PER_TRIP = 16
CHUNK = 64
VMEM_LIMIT = 56 * 1024 * 1024


def _cparams(*sem):
    return pltpu.CompilerParams(dimension_semantics=sem, vmem_limit_bytes=VMEM_LIMIT)


def _pick(n, prefs):
    for p in prefs:
        if n % p == 0:
            return p
    return n


def _pad_to(a, axis, mult):
    n = a.shape[axis]
    r = (-n) % mult
    if r == 0:
        return a
    pad = [(0, 0)] * a.ndim
    pad[axis] = (0, r)
    return jnp.pad(a, pad)


def _mm_kernel(*refs, nk, act, has_bias, glu):
    refs = list(refs)
    x_ref = refs.pop(0)
    w_refs = [refs.pop(0) for _ in range(2 if glu else 1)]
    b_refs = [refs.pop(0) for _ in range((2 if glu else 1) if has_bias else 0)]
    o_ref = refs.pop(0)
    acc_refs = refs
    k = pl.program_id(2)

    def finish(zs):
        if has_bias:
            zs = [z + b[...] for z, b in zip(zs, b_refs)]
        if glu:
            z = zs[0] * jax.nn.sigmoid(zs[1])
        else:
            z = zs[0]
            if act == "tanh":
                z = jnp.tanh(z)
            elif act == "sigmoid":
                z = jax.nn.sigmoid(z)
            elif act == "relu2":
                z = jnp.square(jnp.maximum(z, 0.0))
        o_ref[...] = z.astype(o_ref.dtype)

    x = x_ref[...]
    parts = [jnp.dot(x, w[...], preferred_element_type=F32) for w in w_refs]
    if nk == 1:
        finish(parts)
        return

    @pl.when(k == 0)
    def _():
        for a in acc_refs:
            a[...] = jnp.zeros_like(a)

    for a, p in zip(acc_refs, parts):
        a[...] += p

    @pl.when(k == nk - 1)
    def _():
        finish([a[...] for a in acc_refs])


def _mm(x, w, bias=None, act=None, out_dtype=F32, glu=False, tm=None, tn=None, tk=None, name="mm"):
    m, kdim = x.shape
    n = w.shape[1] // (2 if glu else 1)
    tm = tm or _pick(m, (1536, 1024, 768, 512, 256, 128, 64, 32, 16, 8))
    tn = tn or _pick(n, (512, 256, 128))
    tk = tk or (kdim if kdim <= 4096 else _pick(kdim, (2048, 1024, 512, 256, 128)))
    nk = kdim // tk
    nj = n // tn
    in_specs = [pl.BlockSpec((tm, tk), lambda i, j, k: (i, k)),
                pl.BlockSpec((tk, tn), lambda i, j, k: (k, j))]
    args = [x, w]
    if glu:
        in_specs.append(pl.BlockSpec((tk, tn), lambda i, j, k: (k, j + nj)))
        args.append(w)
    if bias is not None:
        b2 = bias.reshape(1, -1).astype(F32)
        in_specs.append(pl.BlockSpec((1, tn), lambda i, j, k: (0, j)))
        args.append(b2)
        if glu:
            in_specs.append(pl.BlockSpec((1, tn), lambda i, j, k: (0, j + nj)))
            args.append(b2)
    scratch = [] if nk == 1 else [pltpu.VMEM((tm, tn), F32) for _ in range(2 if glu else 1)]
    return pl.pallas_call(
        functools.partial(_mm_kernel, nk=nk, act=act, has_bias=bias is not None, glu=glu),
        grid=(m // tm, nj, nk),
        in_specs=in_specs,
        out_specs=pl.BlockSpec((tm, tn), lambda i, j, k: (i, j)),
        out_shape=jax.ShapeDtypeStruct((m, n), out_dtype),
        scratch_shapes=scratch,
        compiler_params=_cparams("parallel", "parallel", "arbitrary"),
        name=name,
    )(*args)


def _head_sum(x, seg_ref):
    cw = seg_ref.shape[0]
    outs = []
    for s in range(x.shape[1] // cw):
        outs.append(jnp.dot(x[:, s * cw:(s + 1) * cw], seg_ref[...],
                            preferred_element_type=F32, precision=lax.Precision.HIGHEST))
    return outs[0] if len(outs) == 1 else jnp.concatenate(outs, axis=1)


def _seg_ones(cw):
    idx = jnp.arange(cw) // HEAD
    return (idx[:, None] == idx[None, :]).astype(F32)


def _prep_kernel(xp_ref, xc_ref, xn_ref, mod_ref, mix_ref, *rest, tr, n_lat, l_img, l_ctx):
    o_refs = rest[:6]
    hbuf, sbuf = rest[6], rest[7]
    i = pl.program_id(0)
    j = pl.program_id(1)
    sh = mod_ref[0, 0:1, :]
    sc = 1.0 + mod_ref[0, 1:2, :]
    hbuf[0:GRID_W, :] = xp_ref[...] * sc + sh
    hbuf[GRID_W:GRID_W + tr, :] = xc_ref[...] * sc + sh
    hbuf[GRID_W + tr:GRID_W + tr + GRID_W, :] = xn_ref[...] * sc + sh
    row = lax.broadcasted_iota(jnp.int32, (tr, 1), 0) + i * tr
    t_lat = row % l_img
    is_lat = i < n_lat // tr

    def shifted(s, keep):
        sbuf[...] = jnp.where(keep, hbuf[GRID_W + s:GRID_W + s + tr, :], 0.0)

    @pl.when(jnp.logical_and(is_lat, j == 0))
    def _():
        shifted(-1, t_lat % GRID_W != 0)

    @pl.when(jnp.logical_and(is_lat, j == 1))
    def _():
        shifted(1, t_lat % GRID_W != GRID_W - 1)

    @pl.when(jnp.logical_and(is_lat, j == 2))
    def _():
        shifted(-GRID_W, t_lat >= GRID_W)

    @pl.when(jnp.logical_and(is_lat, j == 3))
    def _():
        shifted(GRID_W, t_lat < l_img - GRID_W)

    t_ctx = (row - n_lat) % l_ctx

    @pl.when(jnp.logical_and(jnp.logical_not(is_lat), j < 2))
    def _():
        shifted(-1, t_ctx != 0)

    @pl.when(jnp.logical_and(jnp.logical_not(is_lat), j >= 2))
    def _():
        shifted(1, t_ctx != l_ctx - 1)

    h = hbuf[GRID_W:GRID_W + tr, :]
    xx = sbuf[...] - h
    for m in range(6):
        o_refs[m][...] = (h + xx * mix_ref[m:m + 1, :]).astype(BF16)


def _rwkv_prep(x, mod, mix, dims):
    r, d = x.shape
    tr = dims["tr"]
    dc = d // 4
    nb = tr // GRID_W
    last = r // GRID_W - 1
    kern = functools.partial(_prep_kernel, tr=tr, n_lat=dims["n_lat"], l_img=dims["L"], l_ctx=dims["Lc"])
    gmap = dims["gmap"]
    outs = pl.pallas_call(
        kern,
        grid=(r // tr, 4),
        in_specs=[
            pl.BlockSpec((GRID_W, dc), lambda i, j: (jnp.maximum(i * nb - 1, 0), j)),
            pl.BlockSpec((tr, dc), lambda i, j: (i, j)),
            pl.BlockSpec((GRID_W, dc), lambda i, j: (jnp.minimum(i * nb + nb, last), j)),
            pl.BlockSpec((1, 6, dc), lambda i, j: (gmap(i), 0, j)),
            pl.BlockSpec((6, dc), lambda i, j: (0, j)),
        ],
        out_specs=[pl.BlockSpec((tr, dc), lambda i, j: (i, j)) for _ in range(6)],
        out_shape=[jax.ShapeDtypeStruct((r, d), BF16) for _ in range(6)],
        scratch_shapes=[pltpu.VMEM((tr + 2 * GRID_W, dc), F32), pltpu.VMEM((tr, dc), F32)],
        compiler_params=_cparams("parallel", "parallel"),
        name="rwkv_prep",
    )(x, x, x, mod, mix)
    return outs


def _feat_kernel(*refs, has_vf, rw, ra):
    refs = list(refs)
    r_ref, k_ref, v_ref, tw_ref, ax_ref = [refs.pop(0) for _ in range(5)]
    w2_ref, a2_ref, w0_ref, a0_ref, vec_ref, seg_ref = [refs.pop(0) for _ in range(6)]
    if has_vf:
        vf_ref, xv_ref, v2_ref, v0_ref = [refs.pop(0) for _ in range(4)]
    kk_o, lw0_o, lw1_o, kd0_o, kd1_o, nb0_o, nb1_o, bonus_o = refs[:8]
    v_o = refs[8] if has_vf else None

    k = k_ref[...]
    r = r_ref[...]
    v = v_ref[...]
    k_k = vec_ref[0:1, :]
    k_a = vec_ref[1:2, :]
    r_k = vec_ref[2:3, :]
    kraw = k * k_k
    nrm = jnp.sqrt(_head_sum(kraw * kraw, seg_ref))
    kk = kraw / jnp.maximum(nrm, 1e-12)
    kk_o[...] = kk
    if has_vf:
        vl = v0_ref[...] + jnp.dot(xv_ref[...], v2_ref[...], preferred_element_type=F32)
        v = v + (vf_ref[...] - v) * jax.nn.sigmoid(vl)
        v_o[...] = v
    kd_sum = None
    for d, (lw_o, kd_o, nb_o) in enumerate(((lw0_o, kd0_o, nb0_o), (lw1_o, kd1_o, nb1_o))):
        wl = w0_ref[d:d + 1, :] + jnp.dot(tw_ref[:, d * rw:(d + 1) * rw], w2_ref[d],
                                          preferred_element_type=F32)
        nwl = -wl
        softplus = jnp.maximum(nwl, 0.0) + jnp.log(1.0 + jnp.exp(-jnp.abs(nwl)))
        lw_o[...] = -jnp.exp(-softplus - 0.5)
        al = a0_ref[d:d + 1, :] + jnp.dot(ax_ref[:, d * ra:(d + 1) * ra], a2_ref[d],
                                          preferred_element_type=F32)
        a = jax.nn.sigmoid(al)
        kd = k * (1.0 + (a - 1.0) * k_a)
        kd_o[...] = kd
        nb_o[...] = -(kk * a)
        kd_sum = kd if kd_sum is None else kd_sum + kd
    bonus_o[...] = _head_sum(r * kd_sum * r_k, seg_ref) * v


def _rwkv_features(r, k, v, tw, ax, p, vf, xv):
    rows, d = r.shape
    tr = _pick(rows, (256, 128, 64, 32, 16, 8))
    cb = _pick(d, (1024, 512, 256, 128))
    cw = min(cb, MXU)
    has_vf = vf is not None
    rw = p["w2"].shape[1]
    ra = p["a2"].shape[1]
    tile = pl.BlockSpec((tr, cb), lambda i, j: (i, j))

    def full_rows(a):
        return pl.BlockSpec((tr, a.shape[1]), lambda i, j: (i, 0))

    vec = jnp.stack([p["k_k"], p["k_a"], p["r_k"]]).astype(F32)
    in_specs = [tile, tile, tile, full_rows(tw), full_rows(ax),
                pl.BlockSpec((2, rw, cb), lambda i, j: (0, 0, j)),
                pl.BlockSpec((2, ra, cb), lambda i, j: (0, 0, j)),
                pl.BlockSpec((2, cb), lambda i, j: (0, j)),
                pl.BlockSpec((2, cb), lambda i, j: (0, j)),
                pl.BlockSpec((3, cb), lambda i, j: (0, j)),
                pl.BlockSpec((cw, cw), lambda i, j: (0, 0))]
    args = [r, k, v, tw, ax, p["w2"], p["a2"], p["w0"], p["a0"], vec, _seg_ones(cw)]
    n_out = 8
    if has_vf:
        in_specs += [tile, full_rows(xv),
                     pl.BlockSpec((p["v2"].shape[0], cb), lambda i, j: (0, j)),
                     pl.BlockSpec((1, cb), lambda i, j: (0, j))]
        args += [vf, xv, p["v2"], p["v0"].reshape(1, -1)]
        n_out = 9
    return pl.pallas_call(
        functools.partial(_feat_kernel, has_vf=has_vf, rw=rw, ra=ra),
        grid=(rows // tr, d // cb),
        in_specs=in_specs,
        out_specs=[tile] * n_out,
        out_shape=[jax.ShapeDtypeStruct((rows, d), F32)] * n_out,
        compiler_params=_cparams("parallel", "parallel"),
        name="rwkv_features",
    )(*args)


def _scan_kernel(*refs, reverse, has_prev, prec):
    refs = list(refs)
    r_ref, lw_ref, kd_ref, kk_ref, nb_ref, v_ref = [refs.pop(0) for _ in range(6)]
    yp_ref = refs.pop(0) if has_prev else None
    y_ref, s_ref = refs
    tt = CHUNK
    gw = HPG * HEAD
    n_groups = y_ref.shape[1] // gw

    @pl.when(pl.program_id(1) == 0)
    def _():
        s_ref[...] = jnp.zeros_like(s_ref)

    row_t = lax.broadcasted_iota(jnp.int32, (tt, gw), 0)
    lane = lax.broadcasted_iota(jnp.int32, (tt, gw), 1)
    pos = lane % HEAD
    if reverse:
        strict, incl = pos > row_t, pos >= row_t
    else:
        strict, incl = pos < row_t, pos <= row_t
    eye = (pos == row_t).astype(F32)
    ci = lax.broadcasted_iota(jnp.int32, (tt, tt), 0)
    cj = lax.broadcasted_iota(jnp.int32, (tt, tt), 1)
    csum = ((cj >= ci) if reverse else (cj <= ci)).astype(BF16)
    head_of_row = lax.broadcasted_iota(jnp.int32, (gw, gw), 0) // HEAD
    head_of_lane = lax.broadcasted_iota(jnp.int32, (gw, gw), 1) // HEAD
    same_head = head_of_row == head_of_lane
    last = 0 if reverse else tt - 1

    def mm(a, b):
        return jnp.dot(a.astype(BF16) if prec is None else a, b.astype(BF16) if prec is None else b,
                       preferred_element_type=F32, precision=prec)

    def mm_nt(a, b):
        a, b = (a.astype(BF16), b.astype(BF16)) if prec is None else (a, b)
        return lax.dot_general(a, b, (((1,), (1,)), ((), ())), preferred_element_type=F32, precision=prec)

    def mm_tn(a, b):
        a, b = (a.astype(BF16), b.astype(BF16)) if prec is None else (a, b)
        return lax.dot_general(a, b, (((0,), (0,)), ((), ())), preferred_element_type=F32, precision=prec)

    def blockdiag(m):
        return jnp.where(same_head, jnp.concatenate([m] * HPG, axis=0), 0.0)

    def group(sl, s0):
        lw = lw_ref[:, sl]
        lw_hi = lw.astype(BF16)
        lw_mid = (lw - lw_hi.astype(F32)).astype(BF16)
        lw_lo = (lw - lw_hi.astype(F32) - lw_mid.astype(F32)).astype(BF16)
        c_in = (jnp.dot(csum, lw_hi, preferred_element_type=F32)
                + jnp.dot(csum, lw_mid, preferred_element_type=F32)
                + jnp.dot(csum, lw_lo, preferred_element_type=F32))
        c_tot = c_in[last:last + 1, :]
        g_inv = jnp.exp(-c_in)
        at = kk_ref[:, sl] * jnp.exp(c_in - lw)
        rt = r_ref[:, sl] * jnp.exp(c_in)
        nb = nb_ref[:, sl]
        kd = kd_ref[:, sl]
        v = v_ref[:, sl]
        lr = jnp.concatenate([at, rt], axis=0)
        yield
        ar = mm_nt(lr, s0)
        g_nb = mm_nt(lr, blockdiag(nb * g_inv))
        g_k = mm_nt(lr, blockdiag(kd * g_inv))
        yield
        n_mat = jnp.where(strict, g_nb[:tt], 0.0)
        m_ak = jnp.where(strict, g_k[:tt], 0.0)
        m_rb = jnp.where(incl, g_nb[tt:], 0.0)
        m_rk = jnp.where(incl, g_k[tt:], 0.0)
        v_bd = blockdiag(v)
        x = ar[:tt] + mm(m_ak, v_bd)
        yield
        p = mm(n_mat, blockdiag(n_mat))
        t_inv = eye + n_mat
        step = 2
        while step < tt:
            yield
            if 2 * step < tt:
                both = mm(jnp.concatenate([t_inv, p], axis=0), blockdiag(p))
                t_inv = t_inv + both[:tt]
                p = both[tt:]
            else:
                t_inv = t_inv + mm(t_inv, blockdiag(p))
            step *= 2
        yield
        u = mm(t_inv, blockdiag(x))
        yield
        y = ar[tt:] + mm(m_rb, blockdiag(u)) + mm(m_rk, v_bd)
        if has_prev:
            y = y + yp_ref[:, sl]
        rem = jnp.exp(c_tot - c_in)
        upd = mm_tn(jnp.concatenate([u, v], axis=0), jnp.concatenate([nb * rem, kd * rem], axis=0))
        return y, s0 * jnp.exp(c_tot) + jnp.where(same_head, upd, 0.0)

    par = _pick(n_groups, (GROUPS_PER_TRIP, 4, 2, 1))

    def trip(i, carry):
        gs = [i * par + u for u in range(par)]
        sls = [pl.ds(pl.multiple_of(g * gw, gw), gw) for g in gs]
        chains = [group(sl, s_ref[g]) for g, sl in zip(gs, sls)]
        outs = [None] * par
        while any(o is None for o in outs):
            for u, chain in enumerate(chains):
                if outs[u] is None:
                    try:
                        next(chain)
                    except StopIteration as done:
                        outs[u] = done.value
        for g, sl, (y, s_new) in zip(gs, sls, outs):
            y_ref[:, sl] = y
            s_ref[g] = s_new
        return carry

    lax.fori_loop(0, n_groups // par, trip, 0)


def _rwkv_scan(r, lw, kd, kk, nb, v, y_prev, reverse, dims, prec=None):
    rows, d = r.shape
    bsz, l, lc, n_lat = dims["B"], dims["L"], dims["Lc"], dims["n_lat"]
    tt = CHUNK
    n_ctx, n_latc = lc // tt, l // tt
    nc = n_ctx + n_latc

    def rblk(b, c):
        if reverse:
            ctx_blk = (n_lat + b * lc) // tt + (n_ctx - 1 - c)
            lat_blk = b * n_latc + (n_latc - 1 - (c - n_ctx))
        else:
            ctx_blk = (n_lat + b * lc) // tt + c
            lat_blk = b * n_latc + (c - n_ctx)
        return jnp.where(c < n_ctx, ctx_blk, lat_blk)

    blk = pl.BlockSpec((tt, d), lambda b, c: (rblk(b, c), 0))
    args = [r, lw, kd, kk, nb, v]
    if y_prev is not None:
        args.append(y_prev)
    gw = HPG * HEAD
    return pl.pallas_call(
        functools.partial(_scan_kernel, reverse=reverse, has_prev=y_prev is not None, prec=prec),
        grid=(bsz, nc),
        in_specs=[blk] * len(args),
        out_specs=blk,
        out_shape=jax.ShapeDtypeStruct((rows, d), F32),
        scratch_shapes=[pltpu.VMEM((d // gw, gw, gw), F32)],
        compiler_params=_cparams("parallel", "arbitrary"),
        name="rwkv_scan_rev" if reverse else "rwkv_scan_fwd",
    )(*args)


def _readout_kernel(ys_ref, bonus_ref, sg_ref, g2_ref, lnx_ref, seg_ref, o_ref):
    ys = ys_ref[...]
    inv = 1.0 / HEAD
    mu = _head_sum(ys, seg_ref) * inv
    dlt = ys - mu
    var = _head_sum(dlt * dlt, seg_ref) * inv
    y = dlt * lax.rsqrt(var + GN_EPS) * lnx_ref[0:1, :] + lnx_ref[1:2, :] + bonus_ref[...]
    g = jnp.dot(sg_ref[...], g2_ref[...], preferred_element_type=F32)
    o_ref[...] = (y * g).astype(o_ref.dtype)


def _rwkv_readout(ys, bonus, sg, g2, lnx_g, lnx_b):
    rows, d = ys.shape
    tr = _pick(rows, (256, 128, 64, 32, 16, 8))
    cb = _pick(d, (1024, 512, 256, 128))
    cw = min(cb, MXU)
    tile = pl.BlockSpec((tr, cb), lambda i, j: (i, j))
    lnx = jnp.stack([lnx_g, lnx_b]).astype(F32)
    return pl.pallas_call(
        _readout_kernel,
        grid=(rows // tr, d // cb),
        in_specs=[tile, tile,
                  pl.BlockSpec((tr, sg.shape[1]), lambda i, j: (i, 0)),
                  pl.BlockSpec((g2.shape[0], cb), lambda i, j: (0, j)),
                  pl.BlockSpec((2, cb), lambda i, j: (0, j)),
                  pl.BlockSpec((cw, cw), lambda i, j: (0, 0))],
        out_specs=tile,
        out_shape=jax.ShapeDtypeStruct((rows, d), BF16),
        compiler_params=_cparams("parallel", "parallel"),
        name="rwkv_readout",
    )(ys, bonus, sg, g2, lnx, _seg_ones(cw))


def _resid_ln_kernel(x_ref, y_ref, mod_ref, ln_ref, xo_ref, *rest, alpha, gate_row, mod_rows):
    gate = mod_ref[0, gate_row:gate_row + 1, :]
    z = alpha * x_ref[...] + gate * y_ref[...]
    mu = jnp.mean(z, axis=-1, keepdims=True)
    dz = z - mu
    var = jnp.mean(dz * dz, axis=-1, keepdims=True)
    zn = dz * lax.rsqrt(var + LN_EPS) * ln_ref[0:1, :] + ln_ref[1:2, :]
    xo_ref[...] = zn
    if mod_rows is not None:
        sh = mod_ref[0, mod_rows[0]:mod_rows[0] + 1, :]
        sc = mod_ref[0, mod_rows[1]:mod_rows[1] + 1, :]
        rest[0][...] = (zn * (1.0 + sc) + sh).astype(rest[0].dtype)


def _resid_ln(x, y, mod, ln_g, ln_b, alpha, gate_row, mod_rows, dims, rows=None):
    d = x.shape[1]
    rows = rows or x.shape[0]
    tr = _pick(dims["tr"], (256, 128, 64, 32, 16, 8))
    gmap = dims["gmap"]
    ratio = dims["tr"] // tr
    tile = pl.BlockSpec((tr, d), lambda i: (i, 0))
    ln = jnp.stack([ln_g, ln_b]).astype(F32)
    out_shape = [jax.ShapeDtypeStruct((rows, d), F32)]
    out_specs = [tile]
    if mod_rows is not None:
        out_shape.append(jax.ShapeDtypeStruct((rows, d), BF16))
        out_specs.append(tile)
    outs = pl.pallas_call(
        functools.partial(_resid_ln_kernel, alpha=alpha, gate_row=gate_row, mod_rows=mod_rows),
        grid=(rows // tr,),
        in_specs=[tile, tile,
                  pl.BlockSpec((1, 6, d), lambda i: (gmap(i // ratio), 0, 0)),
                  pl.BlockSpec((2, d), lambda i: (0, 0))],
        out_specs=out_specs,
        out_shape=out_shape,
        compiler_params=_cparams("parallel"),
        name="resid_ln",
    )(x, y, mod, ln)
    return outs


def _s5_kernel(*refs, tcs, nq, ns, finish):
    refs = list(refs)
    x_ref, mod_ref, bm_ref, cm_ref, lam_ref = [refs.pop(0) for _ in range(5)]
    if finish:
        yf_ref, d_ref = refs.pop(0), refs.pop(0)
    y_ref, hre_ref, him_ref, h2d_ref, cr_ref, ci_ref = refs
    reverse = finish
    nsl = ns // LANE

    @pl.when(pl.program_id(2) == 0)
    def _():
        cr_ref[...] = jnp.zeros_like(cr_ref)
        ci_ref[...] = jnp.zeros_like(ci_ref)

    def u_of(q):
        sl = slice(q * S5_CH, (q + 1) * S5_CH)
        return x_ref[:, sl] * (1.0 + mod_ref[0, 1:2, sl]) + mod_ref[0, 0:1, sl]

    for q in range(nq):
        bu = jnp.dot(u_of(q).astype(BF16), bm_ref[q], preferred_element_type=F32)
        for s in range(nsl):
            hre_ref[q, pl.ds(s, tcs, stride=nsl), :] = bu[:, s * LANE:(s + 1) * LANE]
            him_ref[q, pl.ds(s, tcs, stride=nsl), :] = bu[:, ns + s * LANE:ns + (s + 1) * LANE]

    lr = [lam_ref[q, 0] for q in range(nq)]
    li = [lam_ref[q, 1] for q in range(nq)]

    tb = 8
    ntrip = tcs // tb

    def trip(gi, carry):
        g = ntrip - 1 - gi if reverse else gi
        base = pl.multiple_of(g * (tb * nsl), tb * nsl)
        carry = list(carry)
        for j in range(tb):
            off = base + (tb - 1 - j if reverse else j) * nsl
            for q in range(nq):
                hr, hi = carry[2 * q], carry[2 * q + 1]
                nr = lr[q] * hr - li[q] * hi + hre_ref[q, pl.ds(off, nsl), :]
                ni = lr[q] * hi + li[q] * hr + him_ref[q, pl.ds(off, nsl), :]
                hre_ref[q, pl.ds(off, nsl), :] = nr
                him_ref[q, pl.ds(off, nsl), :] = ni
                carry[2 * q], carry[2 * q + 1] = nr, ni
        return tuple(carry)

    init = []
    for q in range(nq):
        init += [cr_ref[q], ci_ref[q]]
    fin = lax.fori_loop(0, ntrip, trip, tuple(init))
    for q in range(nq):
        cr_ref[q] = fin[2 * q]
        ci_ref[q] = fin[2 * q + 1]

    for q in range(nq):
        for s in range(nsl):
            h2d_ref[:, s * LANE:(s + 1) * LANE] = hre_ref[q, pl.ds(s, tcs, stride=nsl), :].astype(BF16)
            h2d_ref[:, ns + s * LANE:ns + (s + 1) * LANE] = him_ref[q, pl.ds(s, tcs, stride=nsl), :].astype(BF16)
        sl = slice(q * S5_CH, (q + 1) * S5_CH)
        y = jnp.dot(h2d_ref[...], cm_ref[q], preferred_element_type=F32)
        if finish:
            y = y + yf_ref[:, sl] + d_ref[:, sl] * u_of(q)
            c = math.sqrt(2.0 / math.pi)
            y = 0.5 * y * (1.0 + jnp.tanh(c * (y + 0.044715 * (y * y * y))))
        y_ref[:, sl] = y.astype(y_ref.dtype)


def _s5_scan(x, mod, bm, cm, lam, y_fwd, dvec, dims):
    reverse = y_fwd is not None
    rows, d = x.shape
    tcs = dims["tcs"]
    ns2 = bm.shape[2]
    ns = ns2 // 2
    nq = _pick(d // S5_CH, (4, 2, 1))
    cbw = nq * S5_CH
    n_lat_blk = dims["L"] // tcs
    n_ctx_blk = dims["Lc"] // tcs
    n_lat_rows_blk = dims["n_lat"] // tcs
    nchunk = n_ctx_blk + n_lat_blk

    def rmap(b, c):
        if reverse:
            ctx_blk = n_lat_rows_blk + b * n_ctx_blk + (n_ctx_blk - 1 - c)
            lat_blk = b * n_lat_blk + (n_lat_blk - 1 - (c - n_ctx_blk))
        else:
            ctx_blk = n_lat_rows_blk + b * n_ctx_blk + c
            lat_blk = b * n_lat_blk + (c - n_ctx_blk)
        return jnp.where(c < n_ctx_blk, ctx_blk, lat_blk)

    bsz = dims["B"]
    tile = pl.BlockSpec((tcs, cbw), lambda b, g, c: (rmap(b, c), g))
    in_specs = [tile,
                pl.BlockSpec((1, 6, cbw), lambda b, g, c: (jnp.where(c < n_ctx_blk, bsz, b), 0, g)),
                pl.BlockSpec((nq, S5_CH, ns2), lambda b, g, c: (g, 0, 0)),
                pl.BlockSpec((nq, ns2, S5_CH), lambda b, g, c: (g, 0, 0)),
                pl.BlockSpec((nq, 2, ns // LANE, LANE), lambda b, g, c: (g, 0, 0, 0))]
    args = [x, mod, bm, cm, lam]
    if reverse:
        in_specs += [tile, pl.BlockSpec((1, cbw), lambda b, g, c: (0, g))]
        args += [y_fwd, dvec.reshape(1, -1).astype(F32)]
    return pl.pallas_call(
        functools.partial(_s5_kernel, tcs=tcs, nq=nq, ns=ns, finish=reverse),
        grid=(bsz, d // cbw, nchunk),
        in_specs=in_specs,
        out_specs=tile,
        out_shape=jax.ShapeDtypeStruct((rows, d), BF16 if reverse else F32),
        scratch_shapes=[pltpu.VMEM((nq, tcs * ns // LANE, LANE), F32),
                        pltpu.VMEM((nq, tcs * ns // LANE, LANE), F32),
                        pltpu.VMEM((tcs, ns2), BF16),
                        pltpu.VMEM((nq, ns // LANE, LANE), F32),
                        pltpu.VMEM((nq, ns // LANE, LANE), F32)],
        compiler_params=_cparams("parallel", "parallel", "arbitrary"),
        name="s5_scan_rev" if reverse else "s5_scan_fwd",
    )(*args)


def _s5_params(lam_re, lam_im, log_dt, b_re, b_im, c_re, c_im, gpb):
    lam = lax.complex(lam_re.astype(F32), lam_im.astype(F32))
    dt = jnp.exp(log_dt.astype(F32))[..., None]
    lam_bar = jnp.exp(lam * dt)
    b_bar = ((lam_bar - 1.0) / lam)[..., None] * lax.complex(b_re.astype(F32), b_im.astype(F32))
    c_mat = lax.complex(c_re.astype(F32), c_im.astype(F32))
    ndir, g, p = lam_bar.shape
    i_sz = b_bar.shape[-1]
    nblk = g // gpb
    eye = jnp.eye(gpb, dtype=F32)

    def blockdiag_in(m):
        m = m.reshape(ndir, nblk, gpb, p, i_sz)
        out = jnp.einsum("dngpi,gh->dngihp", m, eye)
        return out.reshape(ndir, nblk, gpb * i_sz, gpb * p)

    def blockdiag_out(m):
        m = m.reshape(ndir, nblk, gpb, i_sz, p)
        out = jnp.einsum("dngip,gh->dngphi", m, eye)
        return out.reshape(ndir, nblk, gpb * p, gpb * i_sz)

    bm = jnp.concatenate([blockdiag_in(jnp.real(b_bar)), blockdiag_in(jnp.imag(b_bar))], axis=-1).astype(BF16)
    cm = jnp.concatenate([blockdiag_out(jnp.real(c_mat)), blockdiag_out(-jnp.imag(c_mat))], axis=-2).astype(BF16)
    lam_t = jnp.stack([jnp.real(lam_bar), jnp.imag(lam_bar)], axis=1)
    lam_t = lam_t.reshape(ndir, 2, nblk, gpb * p // LANE, LANE).transpose(0, 2, 1, 3, 4).astype(F32)
    return bm, cm, lam_t


def kernel(x, c, ctx, c_ctx, ada_down, ada_up, ada_bias, ln_g, ln_b, rw_mix, rw_wr, rw_wk, rw_wv, rw_wo, rw_w0, rw_w1, rw_w2, rw_a0, rw_a1, rw_a2, rw_v0, rw_v1, rw_v2, rw_g1, rw_g2, rw_kk, rw_ka, rw_rk, rw_lnx_g, rw_lnx_b, s5_lam_re, s5_lam_im, s5_log_dt, s5_b_re, s5_b_im, s5_c_re, s5_c_im, s5_d, s5_glu_w, s5_glu_b, mlp_w1, mlp_w2):
    bsz, seq, d = x.shape
    lc = ctx.shape[1]
    depth = ada_down.shape[0]
    assert d % S5_CH == 0 and seq % CHUNK == 0 and lc % CHUNK == 0
    alpha = (2 * depth) ** 0.25
    n_lat = bsz * seq
    tr = _pick(math.gcd(seq, bsz * lc), (512, 256, 128, 64))
    tiles_per_batch = seq // tr

    def gmap(i):
        return jnp.minimum(i // tiles_per_batch, bsz)

    g_t = math.gcd(seq, lc)
    dims = {"B": bsz, "L": seq, "Lc": lc, "n_lat": n_lat, "tr": tr, "gmap": gmap,
            "tcs": _pick(g_t, (256, 128, 64, 32, 16, 8))}

    xs = jnp.concatenate([x.reshape(n_lat, d), ctx.reshape(bsz * lc, d)], axis=0).astype(F32)
    vf = None

    cvec = jnp.concatenate([c, c_ctx[None, :]], axis=0)
    cvec = _pad_to(jax.nn.silu(cvec), 0, 16).astype(BF16)

    for i in range(depth):
        low = _mm(cvec, ada_down[i].astype(BF16), out_dtype=BF16, name="adaln_down")
        mod = _mm(low, ada_up[i].astype(BF16), bias=ada_bias[i], name="adaln_up")
        mod = mod[:bsz + 1].reshape(bsz + 1, 6, d)
        j = i // 2
        if i % 2 == 0:
            p = {"w2": _pad_to(rw_w2[j], 1, LANE).astype(BF16), "a2": _pad_to(rw_a2[j], 1, LANE).astype(BF16),
                 "w0": rw_w0[j], "a0": rw_a0[j], "k_k": rw_kk[j], "k_a": rw_ka[j],
                 "r_k": rw_rk[j].reshape(-1)}
            lerps = _rwkv_prep(xs, mod, rw_mix[j], dims)
            r = _mm(lerps[0], rw_wr[j].astype(BF16), name="rwkv_r")
            k = _mm(lerps[2], rw_wk[j].astype(BF16), name="rwkv_k")
            v = _mm(lerps[3], rw_wv[j].astype(BF16), name="rwkv_v")
            w1cat = jnp.concatenate([_pad_to(rw_w1[j, dd], 1, LANE) for dd in range(2)], axis=1)
            a1cat = jnp.concatenate([_pad_to(rw_a1[j, dd], 1, LANE) for dd in range(2)], axis=1)
            tw = _mm(lerps[1], w1cat.astype(BF16), act="tanh", out_dtype=BF16, name="rwkv_w1")
            ax = _mm(lerps[4], a1cat.astype(BF16), out_dtype=BF16, name="rwkv_a1")
            sg = _mm(lerps[5], _pad_to(rw_g1[j], 1, LANE).astype(BF16), act="sigmoid", out_dtype=BF16,
                     name="rwkv_g1")
            g2 = _pad_to(rw_g2[j], 0, LANE).astype(BF16)
            xv = None
            if j > 0:
                xv = _mm(lerps[3], _pad_to(rw_v1[j - 1], 1, LANE).astype(BF16), out_dtype=BF16, name="rwkv_v1")
                p["v2"] = _pad_to(rw_v2[j - 1], 0, LANE).astype(BF16)
                p["v0"] = rw_v0[j - 1]
            feats = _rwkv_features(r, k, v, tw, ax, p, vf if j > 0 else None, xv)
            kk, lw0, lw1, kd0, kd1, nb0, nb1, bonus = feats[:8]
            if j > 0:
                v = feats[8]
            else:
                vf = v
            y = _rwkv_scan(r, lw0, kd0, kk, nb0, v, None, False, dims)
            y = _rwkv_scan(r, lw1, kd1, kk, nb1, v, y, True, dims)
            yg = _rwkv_readout(y, bonus, sg, g2, rw_lnx_g[j], rw_lnx_b[j])
            mix_out = _mm(yg, rw_wo[j].astype(BF16), name="rwkv_o")
        else:
            gpb = S5_CH // S5_GROUP
            bm, cm, lam_t = _s5_params(s5_lam_re[j], s5_lam_im[j], s5_log_dt[j], s5_b_re[j], s5_b_im[j],
                                       s5_c_re[j], s5_c_im[j], gpb)
            yf = _s5_scan(xs, mod, bm[0], cm[0], lam_t[0], None, None, dims)
            gl = _s5_scan(xs, mod, bm[1], cm[1], lam_t[1], yf, s5_d[j], dims)
            mix_out = _mm(gl, s5_glu_w[j].astype(BF16), bias=s5_glu_b[j], glu=True,
                          tm=_pick(gl.shape[0], (768, 512, 256, 128, 64, 32, 16, 8)), name="s5_glu")
        xs, h2 = _resid_ln(xs, mix_out, mod, ln_g[i, 0], ln_b[i, 0], alpha, 2, (3, 4), dims)
        a1 = _mm(h2, mlp_w1[i].astype(BF16), act="relu2", out_dtype=BF16, name="mlp_w1")
        mlp_out = _mm(a1, mlp_w2[i].astype(BF16), tm=_pick(a1.shape[0], (768, 512, 256, 128, 64, 32, 16, 8)),
                      tn=_pick(d, (1024, 512, 256, 128)), tk=_pick(a1.shape[1], (4096, 2048, 1024, 512, 256, 128)),
                      name="mlp_w2")
        (xs,) = _resid_ln(xs, mlp_out, mod, ln_g[i, 1], ln_b[i, 1], alpha, 5, None, dims,
                          rows=n_lat if i == depth - 1 else None)
    return xs.reshape(bsz, seq, d).astype(x.dtype)
```

```python
import functools
import math

import jax
import jax.numpy as jnp
from jax import lax
from jax.experimental import pallas as pl
from jax.experimental.pallas import tpu as pltpu

F32 = jnp.float32
BF16 = jnp.bfloat16

GRID_W = 64
HEAD = 64
S5_GROUP = 16
S5_STATE = 64
S5_CH = 256
LN_EPS = 1e-6
GN_EPS = 64e-5
LANE = 128
SUBLANE = 8
MXU = 256
HPG = MXU // HEAD
GROUPS_PER_TRIP = 16
CHUNK = 64
VMEM_LIMIT = 56 * 1024 * 1024


def _cparams(*sem):
    return pltpu.CompilerParams(dimension_semantics=sem, vmem_limit_bytes=VMEM_LIMIT)


def _pick(n, prefs):
    for p in prefs:
        if n % p == 0:
            return p
    return n


def _pad_to(a, axis, mult):
    n = a.shape[axis]
    r = (-n) % mult
    if r == 0:
        return a
    pad = [(0, 0)] * a.ndim
    pad[axis] = (0, r)
    return jnp.pad(a, pad)


def _mm_kernel(*refs, nk, act, has_bias, glu):
    refs = list(refs)
    x_ref = refs.pop(0)
    w_refs = [refs.pop(0) for _ in range(2 if glu else 1)]
    b_refs = [refs.pop(0) for _ in range((2 if glu else 1) if has_bias else 0)]
    o_ref = refs.pop(0)
    acc_refs = refs
    k = pl.program_id(2)

    def finish(zs):
        if has_bias:
            zs = [z + b[...] for z, b in zip(zs, b_refs)]
        if glu:
            z = zs[0] * jax.nn.sigmoid(zs[1])
        else:
            z = zs[0]
            if act == "tanh":
                z = jnp.tanh(z)
            elif act == "sigmoid":
                z = jax.nn.sigmoid(z)
            elif act == "relu2":
                z = jnp.square(jnp.maximum(z, 0.0))
        o_ref[...] = z.astype(o_ref.dtype)

    x = x_ref[...]
    parts = [jnp.dot(x, w[...], preferred_element_type=F32) for w in w_refs]
    if nk == 1:
        finish(parts)
        return

    @pl.when(k == 0)
    def _():
        for a in acc_refs:
            a[...] = jnp.zeros_like(a)

    for a, p in zip(acc_refs, parts):
        a[...] += p

    @pl.when(k == nk - 1)
    def _():
        finish([a[...] for a in acc_refs])


def _mm(x, w, bias=None, act=None, out_dtype=F32, glu=False, tm=None, tn=None, tk=None, layer=None,
        name="mm"):
    m, kdim = x.shape
    n = w.shape[-1] // (2 if glu else 1)
    tm = tm or _pick(m, (1536, 1024, 768, 512, 256, 128, 64, 32, 16, 8))
    tn = tn or _pick(n, (512, 256, 128))
    tk = tk or (kdim if kdim <= 4096 else _pick(kdim, (2048, 1024, 512, 256, 128)))
    nk = kdim // tk
    nj = n // tn

    def w_spec(off):
        if layer is None:
            return pl.BlockSpec((tk, tn), lambda i, j, k: (k, j + off))
        return pl.BlockSpec((None, tk, tn), lambda i, j, k: (layer, k, j + off))

    in_specs = [pl.BlockSpec((tm, tk), lambda i, j, k: (i, k)), w_spec(0)]
    args = [x, w]
    if glu:
        in_specs.append(w_spec(nj))
        args.append(w)
    if bias is not None:
        b2 = bias.reshape(1, -1).astype(F32)
        in_specs.append(pl.BlockSpec((1, tn), lambda i, j, k: (0, j)))
        args.append(b2)
        if glu:
            in_specs.append(pl.BlockSpec((1, tn), lambda i, j, k: (0, j + nj)))
            args.append(b2)
    scratch = [] if nk == 1 else [pltpu.VMEM((tm, tn), F32) for _ in range(2 if glu else 1)]
    return pl.pallas_call(
        functools.partial(_mm_kernel, nk=nk, act=act, has_bias=bias is not None, glu=glu),
        grid=(m // tm, nj, nk),
        in_specs=in_specs,
        out_specs=pl.BlockSpec((tm, tn), lambda i, j, k: (i, j)),
        out_shape=jax.ShapeDtypeStruct((m, n), out_dtype),
        scratch_shapes=scratch,
        compiler_params=_cparams("parallel", "parallel", "arbitrary"),
        name=name,
    )(*args)


def _head_sum(x, seg_ref):
    cw = seg_ref.shape[0]
    outs = []
    for s in range(x.shape[1] // cw):
        outs.append(jnp.dot(x[:, s * cw:(s + 1) * cw], seg_ref[...],
                            preferred_element_type=F32, precision=lax.Precision.HIGHEST))
    return outs[0] if len(outs) == 1 else jnp.concatenate(outs, axis=1)


def _seg_ones(cw):
    idx = jnp.arange(cw) // HEAD
    return (idx[:, None] == idx[None, :]).astype(F32)


def _prep_kernel(xp_ref, xc_ref, xn_ref, mod_ref, mix_ref, *rest, tr, n_lat, l_img, l_ctx):
    o_refs = rest[:6]
    hbuf, sbuf = rest[6], rest[7]
    i = pl.program_id(0)
    j = pl.program_id(1)
    sh = mod_ref[0, 0:1, :]
    sc = 1.0 + mod_ref[0, 1:2, :]
    hbuf[0:GRID_W, :] = xp_ref[...] * sc + sh
    hbuf[GRID_W:GRID_W + tr, :] = xc_ref[...] * sc + sh
    hbuf[GRID_W + tr:GRID_W + tr + GRID_W, :] = xn_ref[...] * sc + sh
    row = lax.broadcasted_iota(jnp.int32, (tr, 1), 0) + i * tr
    t_lat = row % l_img
    is_lat = i < n_lat // tr

    def shifted(s, keep):
        sbuf[...] = jnp.where(keep, hbuf[GRID_W + s:GRID_W + s + tr, :], 0.0)

    @pl.when(jnp.logical_and(is_lat, j == 0))
    def _():
        shifted(-1, t_lat % GRID_W != 0)

    @pl.when(jnp.logical_and(is_lat, j == 1))
    def _():
        shifted(1, t_lat % GRID_W != GRID_W - 1)

    @pl.when(jnp.logical_and(is_lat, j == 2))
    def _():
        shifted(-GRID_W, t_lat >= GRID_W)

    @pl.when(jnp.logical_and(is_lat, j == 3))
    def _():
        shifted(GRID_W, t_lat < l_img - GRID_W)

    t_ctx = (row - n_lat) % l_ctx

    @pl.when(jnp.logical_and(jnp.logical_not(is_lat), j < 2))
    def _():
        shifted(-1, t_ctx != 0)

    @pl.when(jnp.logical_and(jnp.logical_not(is_lat), j >= 2))
    def _():
        shifted(1, t_ctx != l_ctx - 1)

    h = hbuf[GRID_W:GRID_W + tr, :]
    xx = sbuf[...] - h
    for m in range(6):
        o_refs[m][...] = (h + xx * mix_ref[m:m + 1, :]).astype(BF16)


def _rwkv_prep(x, mod, mix, dims):
    r, d = x.shape
    tr = dims["tr"]
    dc = d // 4
    nb = tr // GRID_W
    last = r // GRID_W - 1
    kern = functools.partial(_prep_kernel, tr=tr, n_lat=dims["n_lat"], l_img=dims["L"], l_ctx=dims["Lc"])
    gmap = dims["gmap"]
    outs = pl.pallas_call(
        kern,
        grid=(r // tr, 4),
        in_specs=[
            pl.BlockSpec((GRID_W, dc), lambda i, j: (jnp.maximum(i * nb - 1, 0), j)),
            pl.BlockSpec((tr, dc), lambda i, j: (i, j)),
            pl.BlockSpec((GRID_W, dc), lambda i, j: (jnp.minimum(i * nb + nb, last), j)),
            pl.BlockSpec((1, 6, dc), lambda i, j: (gmap(i), 0, j)),
            pl.BlockSpec((6, dc), lambda i, j: (0, j)),
        ],
        out_specs=[pl.BlockSpec((tr, dc), lambda i, j: (i, j)) for _ in range(6)],
        out_shape=[jax.ShapeDtypeStruct((r, d), BF16) for _ in range(6)],
        scratch_shapes=[pltpu.VMEM((tr + 2 * GRID_W, dc), F32), pltpu.VMEM((tr, dc), F32)],
        compiler_params=_cparams("parallel", "parallel"),
        name="rwkv_prep",
    )(x, x, x, mod, mix)
    return outs


def _feat_kernel(*refs, has_vf, rw, ra):
    refs = list(refs)
    r_ref, k_ref, v_ref, tw_ref, ax_ref = [refs.pop(0) for _ in range(5)]
    w2_ref, a2_ref, w0_ref, a0_ref, vec_ref, seg_ref = [refs.pop(0) for _ in range(6)]
    if has_vf:
        vf_ref, xv_ref, v2_ref, v0_ref = [refs.pop(0) for _ in range(4)]
    kk_o, lw0_o, lw1_o, kd0_o, kd1_o, nb0_o, nb1_o, bonus_o = refs[:8]
    v_o = refs[8] if has_vf else None

    k = k_ref[...]
    r = r_ref[...]
    v = v_ref[...]
    k_k = vec_ref[0:1, :]
    k_a = vec_ref[1:2, :]
    r_k = vec_ref[2:3, :]
    kraw = k * k_k
    nrm = jnp.sqrt(_head_sum(kraw * kraw, seg_ref))
    kk = kraw / jnp.maximum(nrm, 1e-12)
    kk_o[...] = kk
    if has_vf:
        vl = v0_ref[...] + jnp.dot(xv_ref[...], v2_ref[...], preferred_element_type=F32)
        v = v + (vf_ref[...] - v) * jax.nn.sigmoid(vl)
        v_o[...] = v
    kd_sum = None
    for d, (lw_o, kd_o, nb_o) in enumerate(((lw0_o, kd0_o, nb0_o), (lw1_o, kd1_o, nb1_o))):
        wl = w0_ref[d:d + 1, :] + jnp.dot(tw_ref[:, d * rw:(d + 1) * rw], w2_ref[d],
                                          preferred_element_type=F32)
        nwl = -wl
        softplus = jnp.maximum(nwl, 0.0) + jnp.log(1.0 + jnp.exp(-jnp.abs(nwl)))
        lw_o[...] = -jnp.exp(-softplus - 0.5)
        al = a0_ref[d:d + 1, :] + jnp.dot(ax_ref[:, d * ra:(d + 1) * ra], a2_ref[d],
                                          preferred_element_type=F32)
        a = jax.nn.sigmoid(al)
        kd = k * (1.0 + (a - 1.0) * k_a)
        kd_o[...] = kd
        nb_o[...] = -(kk * a)
        kd_sum = kd if kd_sum is None else kd_sum + kd
    bonus_o[...] = _head_sum(r * kd_sum * r_k, seg_ref) * v


def _rwkv_features(r, k, v, tw, ax, p, vf, xv):
    rows, d = r.shape
    tr = _pick(rows, (256, 128, 64, 32, 16, 8))
    cb = _pick(d, (1024, 512, 256, 128))
    cw = min(cb, MXU)
    has_vf = vf is not None
    rw = p["w2"].shape[1]
    ra = p["a2"].shape[1]
    tile = pl.BlockSpec((tr, cb), lambda i, j: (i, j))

    def full_rows(a):
        return pl.BlockSpec((tr, a.shape[1]), lambda i, j: (i, 0))

    vec = jnp.stack([p["k_k"], p["k_a"], p["r_k"]]).astype(F32)
    in_specs = [tile, tile, tile, full_rows(tw), full_rows(ax),
                pl.BlockSpec((2, rw, cb), lambda i, j: (0, 0, j)),
                pl.BlockSpec((2, ra, cb), lambda i, j: (0, 0, j)),
                pl.BlockSpec((2, cb), lambda i, j: (0, j)),
                pl.BlockSpec((2, cb), lambda i, j: (0, j)),
                pl.BlockSpec((3, cb), lambda i, j: (0, j)),
                pl.BlockSpec((cw, cw), lambda i, j: (0, 0))]
    args = [r, k, v, tw, ax, p["w2"], p["a2"], p["w0"], p["a0"], vec, _seg_ones(cw)]
    n_out = 8
    if has_vf:
        in_specs += [tile, full_rows(xv),
                     pl.BlockSpec((p["v2"].shape[0], cb), lambda i, j: (0, j)),
                     pl.BlockSpec((1, cb), lambda i, j: (0, j))]
        args += [vf, xv, p["v2"], p["v0"].reshape(1, -1)]
        n_out = 9
    return pl.pallas_call(
        functools.partial(_feat_kernel, has_vf=has_vf, rw=rw, ra=ra),
        grid=(rows // tr, d // cb),
        in_specs=in_specs,
        out_specs=[tile] * n_out,
        out_shape=[jax.ShapeDtypeStruct((rows, d), F32)] * n_out,
        compiler_params=_cparams("parallel", "parallel"),
        name="rwkv_features",
    )(*args)


def _scan_kernel(*refs, reverse, has_prev, prec):
    refs = list(refs)
    r_ref, lw_ref, kd_ref, kk_ref, nb_ref, v_ref = [refs.pop(0) for _ in range(6)]
    yp_ref = refs.pop(0) if has_prev else None
    y_ref, s_ref = refs
    tt = CHUNK
    gw = HPG * HEAD
    n_groups = y_ref.shape[1] // gw

    @pl.when(pl.program_id(1) == 0)
    def _():
        s_ref[...] = jnp.zeros_like(s_ref)

    row_t = lax.broadcasted_iota(jnp.int32, (tt, gw), 0)
    lane = lax.broadcasted_iota(jnp.int32, (tt, gw), 1)
    pos = lane % HEAD
    if reverse:
        strict, incl = pos > row_t, pos >= row_t
    else:
        strict, incl = pos < row_t, pos <= row_t
    eye = (pos == row_t).astype(F32)
    ci = lax.broadcasted_iota(jnp.int32, (tt, tt), 0)
    cj = lax.broadcasted_iota(jnp.int32, (tt, tt), 1)
    csum = ((cj >= ci) if reverse else (cj <= ci)).astype(BF16)
    head_of_row = lax.broadcasted_iota(jnp.int32, (gw, gw), 0) // HEAD
    head_of_lane = lax.broadcasted_iota(jnp.int32, (gw, gw), 1) // HEAD
    same_head = head_of_row == head_of_lane
    last = 0 if reverse else tt - 1

    def mm(a, b):
        return jnp.dot(a.astype(BF16) if prec is None else a, b.astype(BF16) if prec is None else b,
                       preferred_element_type=F32, precision=prec)

    def mm_nt(a, b):
        a, b = (a.astype(BF16), b.astype(BF16)) if prec is None else (a, b)
        return lax.dot_general(a, b, (((1,), (1,)), ((), ())), preferred_element_type=F32, precision=prec)

    def mm_tn(a, b):
        a, b = (a.astype(BF16), b.astype(BF16)) if prec is None else (a, b)
        return lax.dot_general(a, b, (((0,), (0,)), ((), ())), preferred_element_type=F32, precision=prec)

    def blockdiag(m):
        return jnp.where(same_head, jnp.concatenate([m] * HPG, axis=0), 0.0)

    def group(sl, s0):
        lw = lw_ref[:, sl]
        lw_hi = lw.astype(BF16)
        lw_mid = (lw - lw_hi.astype(F32)).astype(BF16)
        lw_lo = (lw - lw_hi.astype(F32) - lw_mid.astype(F32)).astype(BF16)
        c_in = (jnp.dot(csum, lw_hi, preferred_element_type=F32)
                + jnp.dot(csum, lw_mid, preferred_element_type=F32)
                + jnp.dot(csum, lw_lo, preferred_element_type=F32))
        c_tot = c_in[last:last + 1, :]
        g_inv = jnp.exp(-c_in)
        at = kk_ref[:, sl] * jnp.exp(c_in - lw)
        rt = r_ref[:, sl] * jnp.exp(c_in)
        nb = nb_ref[:, sl]
        kd = kd_ref[:, sl]
        v = v_ref[:, sl]
        lr = jnp.concatenate([at, rt], axis=0)
        yield
        ar = mm_nt(lr, s0)
        g_nb = mm_nt(lr, blockdiag(nb * g_inv))
        g_k = mm_nt(lr, blockdiag(kd * g_inv))
        yield
        n_mat = jnp.where(strict, g_nb[:tt], 0.0)
        m_ak = jnp.where(strict, g_k[:tt], 0.0)
        m_rb = jnp.where(incl, g_nb[tt:], 0.0)
        m_rk = jnp.where(incl, g_k[tt:], 0.0)
        v_bd = blockdiag(v)
        x = ar[:tt] + mm(m_ak, v_bd)
        yield
        p = mm(n_mat, blockdiag(n_mat))
        t_inv = eye + n_mat
        step = 2
        while step < tt:
            yield
            if 2 * step < tt:
                both = mm(jnp.concatenate([t_inv, p], axis=0), blockdiag(p))
                t_inv = t_inv + both[:tt]
                p = both[tt:]
            else:
                t_inv = t_inv + mm(t_inv, blockdiag(p))
            step *= 2
        yield
        u = mm(t_inv, blockdiag(x))
        yield
        y = ar[tt:] + mm(m_rb, blockdiag(u)) + mm(m_rk, v_bd)
        if has_prev:
            y = y + yp_ref[:, sl]
        rem = jnp.exp(c_tot - c_in)
        upd = mm_tn(jnp.concatenate([u, v], axis=0), jnp.concatenate([nb * rem, kd * rem], axis=0))
        return y, s0 * jnp.exp(c_tot) + jnp.where(same_head, upd, 0.0)

    par = _pick(n_groups, (GROUPS_PER_TRIP, 4, 2, 1))

    def trip(i, carry):
        gs = [i * par + u for u in range(par)]
        sls = [pl.ds(pl.multiple_of(g * gw, gw), gw) for g in gs]
        chains = [group(sl, s_ref[g]) for g, sl in zip(gs, sls)]
        outs = [None] * par
        while any(o is None for o in outs):
            for u, chain in enumerate(chains):
                if outs[u] is None:
                    try:
                        next(chain)
                    except StopIteration as done:
                        outs[u] = done.value
        for g, sl, (y, s_new) in zip(gs, sls, outs):
            y_ref[:, sl] = y
            s_ref[g] = s_new
        return carry

    lax.fori_loop(0, n_groups // par, trip, 0)


def _rwkv_scan(r, lw, kd, kk, nb, v, y_prev, reverse, dims, prec=None):
    rows, d = r.shape
    bsz, l, lc, n_lat = dims["B"], dims["L"], dims["Lc"], dims["n_lat"]
    tt = CHUNK
    n_ctx, n_latc = lc // tt, l // tt
    nc = n_ctx + n_latc

    def rblk(b, c):
        if reverse:
            ctx_blk = (n_lat + b * lc) // tt + (n_ctx - 1 - c)
            lat_blk = b * n_latc + (n_latc - 1 - (c - n_ctx))
        else:
            ctx_blk = (n_lat + b * lc) // tt + c
            lat_blk = b * n_latc + (c - n_ctx)
        return jnp.where(c < n_ctx, ctx_blk, lat_blk)

    blk = pl.BlockSpec((tt, d), lambda b, c: (rblk(b, c), 0))
    args = [r, lw, kd, kk, nb, v]
    if y_prev is not None:
        args.append(y_prev)
    gw = HPG * HEAD
    return pl.pallas_call(
        functools.partial(_scan_kernel, reverse=reverse, has_prev=y_prev is not None, prec=prec),
        grid=(bsz, nc),
        in_specs=[blk] * len(args),
        out_specs=blk,
        out_shape=jax.ShapeDtypeStruct((rows, d), F32),
        scratch_shapes=[pltpu.VMEM((d // gw, gw, gw), F32)],
        compiler_params=_cparams("parallel", "arbitrary"),
        name="rwkv_scan_rev" if reverse else "rwkv_scan_fwd",
    )(*args)


def _readout_kernel(ys_ref, bonus_ref, sg_ref, g2_ref, lnx_ref, seg_ref, o_ref):
    ys = ys_ref[...]
    inv = 1.0 / HEAD
    mu = _head_sum(ys, seg_ref) * inv
    dlt = ys - mu
    var = _head_sum(dlt * dlt, seg_ref) * inv
    y = dlt * lax.rsqrt(var + GN_EPS) * lnx_ref[0:1, :] + lnx_ref[1:2, :] + bonus_ref[...]
    g = jnp.dot(sg_ref[...], g2_ref[...], preferred_element_type=F32)
    o_ref[...] = (y * g).astype(o_ref.dtype)


def _rwkv_readout(ys, bonus, sg, g2, lnx_g, lnx_b):
    rows, d = ys.shape
    tr = _pick(rows, (256, 128, 64, 32, 16, 8))
    cb = _pick(d, (1024, 512, 256, 128))
    cw = min(cb, MXU)
    tile = pl.BlockSpec((tr, cb), lambda i, j: (i, j))
    lnx = jnp.stack([lnx_g, lnx_b]).astype(F32)
    return pl.pallas_call(
        _readout_kernel,
        grid=(rows // tr, d // cb),
        in_specs=[tile, tile,
                  pl.BlockSpec((tr, sg.shape[1]), lambda i, j: (i, 0)),
                  pl.BlockSpec((g2.shape[0], cb), lambda i, j: (0, j)),
                  pl.BlockSpec((2, cb), lambda i, j: (0, j)),
                  pl.BlockSpec((cw, cw), lambda i, j: (0, 0))],
        out_specs=tile,
        out_shape=jax.ShapeDtypeStruct((rows, d), BF16),
        compiler_params=_cparams("parallel", "parallel"),
        name="rwkv_readout",
    )(ys, bonus, sg, g2, lnx, _seg_ones(cw))


def _resid_ln_kernel(x_ref, y_ref, mod_ref, ln_ref, xo_ref, *rest, alpha, gate_row, mod_rows):
    gate = mod_ref[0, gate_row:gate_row + 1, :]
    z = alpha * x_ref[...] + gate * y_ref[...]
    mu = jnp.mean(z, axis=-1, keepdims=True)
    dz = z - mu
    var = jnp.mean(dz * dz, axis=-1, keepdims=True)
    zn = dz * lax.rsqrt(var + LN_EPS) * ln_ref[0:1, :] + ln_ref[1:2, :]
    xo_ref[...] = zn
    if mod_rows is not None:
        sh = mod_ref[0, mod_rows[0]:mod_rows[0] + 1, :]
        sc = mod_ref[0, mod_rows[1]:mod_rows[1] + 1, :]
        rest[0][...] = (zn * (1.0 + sc) + sh).astype(rest[0].dtype)


def _resid_ln(x, y, mod, ln_g, ln_b, alpha, gate_row, mod_rows, dims, rows=None):
    d = x.shape[1]
    rows = rows or x.shape[0]
    tr = _pick(dims["tr"], (256, 128, 64, 32, 16, 8))
    gmap = dims["gmap"]
    ratio = dims["tr"] // tr
    tile = pl.BlockSpec((tr, d), lambda i: (i, 0))
    ln = jnp.stack([ln_g, ln_b]).astype(F32)
    out_shape = [jax.ShapeDtypeStruct((rows, d), F32)]
    out_specs = [tile]
    if mod_rows is not None:
        out_shape.append(jax.ShapeDtypeStruct((rows, d), BF16))
        out_specs.append(tile)
    outs = pl.pallas_call(
        functools.partial(_resid_ln_kernel, alpha=alpha, gate_row=gate_row, mod_rows=mod_rows),
        grid=(rows // tr,),
        in_specs=[tile, tile,
                  pl.BlockSpec((1, 6, d), lambda i: (gmap(i // ratio), 0, 0)),
                  pl.BlockSpec((2, d), lambda i: (0, 0))],
        out_specs=out_specs,
        out_shape=out_shape,
        compiler_params=_cparams("parallel"),
        name="resid_ln",
    )(x, y, mod, ln)
    return outs


def _s5_kernel(*refs, tcs, nq, ns, finish):
    refs = list(refs)
    x_ref, mod_ref, bm_ref, cm_ref, lam_ref = [refs.pop(0) for _ in range(5)]
    if finish:
        yf_ref, d_ref = refs.pop(0), refs.pop(0)
    y_ref = refs.pop(0)
    hre_refs = [refs.pop(0) for _ in range(nq)]
    him_refs = [refs.pop(0) for _ in range(nq)]
    h2d_refs = [refs.pop(0) for _ in range(nq)]
    cr_ref, ci_ref = refs
    reverse = finish
    nsl = ns // LANE

    @pl.when(pl.program_id(2) == 0)
    def _():
        cr_ref[...] = jnp.zeros_like(cr_ref)
        ci_ref[...] = jnp.zeros_like(ci_ref)

    def u_of(q):
        sl = slice(q * S5_CH, (q + 1) * S5_CH)
        return x_ref[:, sl] * (1.0 + mod_ref[0, 1:2, sl]) + mod_ref[0, 0:1, sl]

    tb = 8
    ntrip = tcs // tb
    npiece = 1

    def stream(qs):
        proj_in, scan, proj_out = [], [], []
        state = {}

        def in_piece(q, pc):
            def run():
                cw = 2 * ns // npiece
                bu = jnp.dot(u_of(q).astype(BF16), bm_ref[q, :, pc * cw:(pc + 1) * cw],
                             preferred_element_type=F32)
                for j in range(cw // LANE):
                    col = pc * cw + j * LANE
                    ref, s = (hre_refs[q], col // LANE) if col < ns else (him_refs[q], (col - ns) // LANE)
                    ref[pl.ds(s, tcs, stride=nsl), :] = bu[:, j * LANE:(j + 1) * LANE]
            return run

        def scan_piece(gi):
            def run():
                if gi == 0:
                    for q in qs:
                        state[q] = (cr_ref[q], ci_ref[q], lam_ref[q, 0], lam_ref[q, 1])
                g = ntrip - 1 - gi if reverse else gi
                for j in range(tb):
                    off = (g * tb + (tb - 1 - j if reverse else j)) * nsl
                    for q in qs:
                        hr, hi, lr, li = state[q]
                        nr = lr * hr - li * hi + hre_refs[q][pl.ds(off, nsl), :]
                        ni = lr * hi + li * hr + him_refs[q][pl.ds(off, nsl), :]
                        hre_refs[q][pl.ds(off, nsl), :] = nr
                        him_refs[q][pl.ds(off, nsl), :] = ni
                        state[q] = (nr, ni, lr, li)
                if gi == ntrip - 1:
                    for q in qs:
                        cr_ref[q] = state[q][0]
                        ci_ref[q] = state[q][1]
            return run

        def out_piece(q, pc):
            def run():
                for s in range(pc * nsl // npiece, (pc + 1) * nsl // npiece):
                    h2d_refs[q][:, s * LANE:(s + 1) * LANE] = \
                        hre_refs[q][pl.ds(s, tcs, stride=nsl), :].astype(BF16)
                    h2d_refs[q][:, ns + s * LANE:ns + (s + 1) * LANE] = \
                        him_refs[q][pl.ds(s, tcs, stride=nsl), :].astype(BF16)
            return run

        def out_dot(q):
            def run():
                sl = slice(q * S5_CH, (q + 1) * S5_CH)
                y = jnp.dot(h2d_refs[q][...], cm_ref[q], preferred_element_type=F32)
                if finish:
                    y = y + yf_ref[:, sl] + d_ref[:, sl] * u_of(q)
                    c = math.sqrt(2.0 / math.pi)
                    y = 0.5 * y * (1.0 + jnp.tanh(c * (y + 0.044715 * (y * y * y))))
                y_ref[:, sl] = y.astype(y_ref.dtype)
            return run

        for q in qs:
            proj_in += [in_piece(q, pc) for pc in range(npiece)]
            proj_out += [out_piece(q, pc) for pc in range(npiece)] + [out_dot(q)]
        scan += [scan_piece(gi) for gi in range(ntrip)]
        return proj_in, scan, proj_out

    s_in, s_scan, s_out = stream(list(range(nq)))
    for piece in s_in + s_scan + s_out:
        piece()


def _s5_scan(x, mod, bm, cm, lam, y_fwd, dvec, dims):
    reverse = y_fwd is not None
    rows, d = x.shape
    tcs = dims["tcs"]
    ns2 = bm.shape[2]
    ns = ns2 // 2
    nq = _pick(d // S5_CH, (4, 2, 1))
    cbw = nq * S5_CH
    n_lat_blk = dims["L"] // tcs
    n_ctx_blk = dims["Lc"] // tcs
    n_lat_rows_blk = dims["n_lat"] // tcs
    nchunk = n_ctx_blk + n_lat_blk

    def rmap(b, c):
        if reverse:
            ctx_blk = n_lat_rows_blk + b * n_ctx_blk + (n_ctx_blk - 1 - c)
            lat_blk = b * n_lat_blk + (n_lat_blk - 1 - (c - n_ctx_blk))
        else:
            ctx_blk = n_lat_rows_blk + b * n_ctx_blk + c
            lat_blk = b * n_lat_blk + (c - n_ctx_blk)
        return jnp.where(c < n_ctx_blk, ctx_blk, lat_blk)

    bsz = dims["B"]
    tile = pl.BlockSpec((tcs, cbw), lambda b, g, c: (rmap(b, c), g))
    in_specs = [tile,
                pl.BlockSpec((1, 6, cbw), lambda b, g, c: (jnp.where(c < n_ctx_blk, bsz, b), 0, g)),
                pl.BlockSpec((nq, S5_CH, ns2), lambda b, g, c: (g, 0, 0)),
                pl.BlockSpec((nq, ns2, S5_CH), lambda b, g, c: (g, 0, 0)),
                pl.BlockSpec((nq, 2, ns // LANE, LANE), lambda b, g, c: (g, 0, 0, 0))]
    args = [x, mod, bm, cm, lam]
    if reverse:
        in_specs += [tile, pl.BlockSpec((1, cbw), lambda b, g, c: (0, g))]
        args += [y_fwd, dvec.reshape(1, -1).astype(F32)]
    return pl.pallas_call(
        functools.partial(_s5_kernel, tcs=tcs, nq=nq, ns=ns, finish=reverse),
        grid=(bsz, d // cbw, nchunk),
        in_specs=in_specs,
        out_specs=tile,
        out_shape=jax.ShapeDtypeStruct((rows, d), BF16 if reverse else F32),
        scratch_shapes=([pltpu.VMEM((tcs * ns // LANE, LANE), F32)] * (2 * nq)
                        + [pltpu.VMEM((tcs, ns2), BF16)] * nq
                        + [pltpu.VMEM((nq, ns // LANE, LANE), F32)] * 2),
        compiler_params=_cparams("parallel", "parallel", "arbitrary"),
        name="s5_scan_rev" if reverse else "s5_scan_fwd",
    )(*args)


def _s5_params(lam_re, lam_im, log_dt, b_re, b_im, c_re, c_im, gpb):
    lam = lax.complex(lam_re.astype(F32), lam_im.astype(F32))
    dt = jnp.exp(log_dt.astype(F32))[..., None]
    lam_bar = jnp.exp(lam * dt)
    b_bar = ((lam_bar - 1.0) / lam)[..., None] * lax.complex(b_re.astype(F32), b_im.astype(F32))
    c_mat = lax.complex(c_re.astype(F32), c_im.astype(F32))
    ndir, g, p = lam_bar.shape
    i_sz = b_bar.shape[-1]
    nblk = g // gpb
    eye = jnp.eye(gpb, dtype=F32)

    def blockdiag_in(m):
        m = m.reshape(ndir, nblk, gpb, p, i_sz)
        out = jnp.einsum("dngpi,gh->dngihp", m, eye)
        return out.reshape(ndir, nblk, gpb * i_sz, gpb * p)

    def blockdiag_out(m):
        m = m.reshape(ndir, nblk, gpb, i_sz, p)
        out = jnp.einsum("dngip,gh->dngphi", m, eye)
        return out.reshape(ndir, nblk, gpb * p, gpb * i_sz)

    bm = jnp.concatenate([blockdiag_in(jnp.real(b_bar)), blockdiag_in(jnp.imag(b_bar))], axis=-1).astype(BF16)
    cm = jnp.concatenate([blockdiag_out(jnp.real(c_mat)), blockdiag_out(-jnp.imag(c_mat))], axis=-2).astype(BF16)
    lam_t = jnp.stack([jnp.real(lam_bar), jnp.imag(lam_bar)], axis=1)
    lam_t = lam_t.reshape(ndir, 2, nblk, gpb * p // LANE, LANE).transpose(0, 2, 1, 3, 4).astype(F32)
    return bm, cm, lam_t


def kernel(x, c, ctx, c_ctx, ada_down, ada_up, ada_bias, ln_g, ln_b, rw_mix, rw_wr, rw_wk, rw_wv, rw_wo, rw_w0, rw_w1, rw_w2, rw_a0, rw_a1, rw_a2, rw_v0, rw_v1, rw_v2, rw_g1, rw_g2, rw_kk, rw_ka, rw_rk, rw_lnx_g, rw_lnx_b, s5_lam_re, s5_lam_im, s5_log_dt, s5_b_re, s5_b_im, s5_c_re, s5_c_im, s5_d, s5_glu_w, s5_glu_b, mlp_w1, mlp_w2):
    bsz, seq, d = x.shape
    lc = ctx.shape[1]
    depth = ada_down.shape[0]
    assert d % S5_CH == 0 and seq % CHUNK == 0 and lc % CHUNK == 0
    alpha = (2 * depth) ** 0.25
    n_lat = bsz * seq
    tr = _pick(math.gcd(seq, bsz * lc), (512, 256, 128, 64))
    tiles_per_batch = seq // tr

    def gmap(i):
        return jnp.minimum(i // tiles_per_batch, bsz)

    g_t = math.gcd(seq, lc)
    dims = {"B": bsz, "L": seq, "Lc": lc, "n_lat": n_lat, "tr": tr, "gmap": gmap,
            "tcs": _pick(g_t, (256, 128, 64, 32, 16, 8))}

    xs = jnp.concatenate([x.reshape(n_lat, d), ctx.reshape(bsz * lc, d)], axis=0).astype(F32)
    vf = None

    cvec = jnp.concatenate([c, c_ctx[None, :]], axis=0)
    cvec = _pad_to(jax.nn.silu(cvec), 0, 16).astype(BF16)
    wr_b, wk_b, wv_b, wo_b = (w.astype(BF16) for w in (rw_wr, rw_wk, rw_wv, rw_wo))
    glu_b, w1_b, w2_b = (w.astype(BF16) for w in (s5_glu_w, mlp_w1, mlp_w2))

    for i in range(depth):
        low = _mm(cvec, ada_down[i].astype(BF16), out_dtype=BF16, name="adaln_down")
        mod = _mm(low, ada_up[i].astype(BF16), bias=ada_bias[i], name="adaln_up")
        mod = mod[:bsz + 1].reshape(bsz + 1, 6, d)
        j = i // 2
        if i % 2 == 0:
            p = {"w2": _pad_to(rw_w2[j], 1, LANE).astype(BF16), "a2": _pad_to(rw_a2[j], 1, LANE).astype(BF16),
                 "w0": rw_w0[j], "a0": rw_a0[j], "k_k": rw_kk[j], "k_a": rw_ka[j],
                 "r_k": rw_rk[j].reshape(-1)}
            lerps = _rwkv_prep(xs, mod, rw_mix[j], dims)
            r = _mm(lerps[0], wr_b, layer=j, name="rwkv_r")
            k = _mm(lerps[2], wk_b, layer=j, name="rwkv_k")
            v = _mm(lerps[3], wv_b, layer=j, name="rwkv_v")
            w1cat = jnp.concatenate([_pad_to(rw_w1[j, dd], 1, LANE) for dd in range(2)], axis=1)
            a1cat = jnp.concatenate([_pad_to(rw_a1[j, dd], 1, LANE) for dd in range(2)], axis=1)
            tw = _mm(lerps[1], w1cat.astype(BF16), act="tanh", out_dtype=BF16, name="rwkv_w1")
            ax = _mm(lerps[4], a1cat.astype(BF16), out_dtype=BF16, name="rwkv_a1")
            sg = _mm(lerps[5], _pad_to(rw_g1[j], 1, LANE).astype(BF16), act="sigmoid", out_dtype=BF16,
                     name="rwkv_g1")
            g2 = _pad_to(rw_g2[j], 0, LANE).astype(BF16)
            xv = None
            if j > 0:
                xv = _mm(lerps[3], _pad_to(rw_v1[j - 1], 1, LANE).astype(BF16), out_dtype=BF16, name="rwkv_v1")
                p["v2"] = _pad_to(rw_v2[j - 1], 0, LANE).astype(BF16)
                p["v0"] = rw_v0[j - 1]
            feats = _rwkv_features(r, k, v, tw, ax, p, vf if j > 0 else None, xv)
            kk, lw0, lw1, kd0, kd1, nb0, nb1, bonus = feats[:8]
            if j > 0:
                v = feats[8]
            else:
                vf = v
            y = _rwkv_scan(r, lw0, kd0, kk, nb0, v, None, False, dims)
            y = _rwkv_scan(r, lw1, kd1, kk, nb1, v, y, True, dims)
            yg = _rwkv_readout(y, bonus, sg, g2, rw_lnx_g[j], rw_lnx_b[j])
            mix_out = _mm(yg, wo_b, layer=j, name="rwkv_o")
        else:
            gpb = S5_CH // S5_GROUP
            bm, cm, lam_t = _s5_params(s5_lam_re[j], s5_lam_im[j], s5_log_dt[j], s5_b_re[j], s5_b_im[j],
                                       s5_c_re[j], s5_c_im[j], gpb)
            yf = _s5_scan(xs, mod, bm[0], cm[0], lam_t[0], None, None, dims)
            gl = _s5_scan(xs, mod, bm[1], cm[1], lam_t[1], yf, s5_d[j], dims)
            mix_out = _mm(gl, glu_b, layer=j, bias=s5_glu_b[j], glu=True,
                          tm=_pick(gl.shape[0], (768, 512, 256, 128, 64, 32, 16, 8)), name="s5_glu")
        xs, h2 = _resid_ln(xs, mix_out, mod, ln_g[i, 0], ln_b[i, 0], alpha, 2, (3, 4), dims)
        a1 = _mm(h2, w1_b, layer=i, act="relu2", out_dtype=BF16, name="mlp_w1")
        mlp_out = _mm(a1, w2_b, layer=i, tm=_pick(a1.shape[0], (768, 512, 256, 128, 64, 32, 16, 8)),
                      tn=_pick(d, (1024, 512, 256, 128)), tk=_pick(a1.shape[1], (4096, 2048, 1024, 512, 256, 128)),
                      name="mlp_w2")
        (xs,) = _resid_ln(xs, mlp_out, mod, ln_g[i, 1], ln_b[i, 1], alpha, 5, None, dims,
                          rows=n_lat if i == depth - 1 else None)
    return xs.reshape(bsz, seq, d).astype(x.dtype)
```

```python
import functools
import math

import jax
import jax.numpy as jnp
from jax import lax
from jax.experimental import pallas as pl
from jax.experimental.pallas import tpu as pltpu

F32 = jnp.float32
BF16 = jnp.bfloat16

GRID_W = 64
HEAD = 64
S5_GROUP = 16
S5_STATE = 64
S5_CH = 256
LN_EPS = 1e-6
GN_EPS = 64e-5
LANE = 128
SUBLANE = 8
MXU = 256
HPG = MXU // HEAD
INV_BASE = 4
GROUPS_PER_TRIP = 16
CHUNK = 64
VMEM_LIMIT = 56 * 1024 * 1024


def _cparams(*sem):
    return pltpu.CompilerParams(dimension_semantics=sem, vmem_limit_bytes=VMEM_LIMIT)


def _pick(n, prefs):
    for p in prefs:
        if n % p == 0:
            return p
    return n


def _pad_to(a, axis, mult):
    n = a.shape[axis]
    r = (-n) % mult
    if r == 0:
        return a
    pad = [(0, 0)] * a.ndim
    pad[axis] = (0, r)
    return jnp.pad(a, pad)


def _mm_kernel(*refs, nk, act, has_bias, glu):
    refs = list(refs)
    x_ref = refs.pop(0)
    w_refs = [refs.pop(0) for _ in range(2 if glu else 1)]
    b_refs = [refs.pop(0) for _ in range((2 if glu else 1) if has_bias else 0)]
    o_ref = refs.pop(0)
    acc_refs = refs
    k = pl.program_id(2)

    def finish(zs):
        if has_bias:
            zs = [z + b[...] for z, b in zip(zs, b_refs)]
        if glu:
            z = zs[0] * jax.nn.sigmoid(zs[1])
        else:
            z = zs[0]
            if act == "tanh":
                z = jnp.tanh(z)
            elif act == "sigmoid":
                z = jax.nn.sigmoid(z)
            elif act == "relu2":
                z = jnp.square(jnp.maximum(z, 0.0))
        o_ref[...] = z.astype(o_ref.dtype)

    x = x_ref[...]
    parts = [jnp.dot(x, w[...], preferred_element_type=F32) for w in w_refs]
    if nk == 1:
        finish(parts)
        return

    @pl.when(k == 0)
    def _():
        for a in acc_refs:
            a[...] = jnp.zeros_like(a)

    for a, p in zip(acc_refs, parts):
        a[...] += p

    @pl.when(k == nk - 1)
    def _():
        finish([a[...] for a in acc_refs])


def _mm(x, w, bias=None, act=None, out_dtype=F32, glu=False, tm=None, tn=None, tk=None, layer=None,
        name="mm"):
    m, kdim = x.shape
    n = w.shape[-1] // (2 if glu else 1)
    tm = tm or _pick(m, (1536, 1024, 768, 512, 256, 128, 64, 32, 16, 8))
    tn = tn or _pick(n, (512, 256, 128))
    tk = tk or (kdim if kdim <= 4096 else _pick(kdim, (2048, 1024, 512, 256, 128)))
    nk = kdim // tk
    nj = n // tn

    def w_spec(off):
        if layer is None:
            return pl.BlockSpec((tk, tn), lambda i, j, k: (k, j + off))
        return pl.BlockSpec((None, tk, tn), lambda i, j, k: (layer, k, j + off))

    in_specs = [pl.BlockSpec((tm, tk), lambda i, j, k: (i, k)), w_spec(0)]
    args = [x, w]
    if glu:
        in_specs.append(w_spec(nj))
        args.append(w)
    if bias is not None:
        b2 = bias.reshape(1, -1).astype(F32)
        in_specs.append(pl.BlockSpec((1, tn), lambda i, j, k: (0, j)))
        args.append(b2)
        if glu:
            in_specs.append(pl.BlockSpec((1, tn), lambda i, j, k: (0, j + nj)))
            args.append(b2)
    scratch = [] if nk == 1 else [pltpu.VMEM((tm, tn), F32) for _ in range(2 if glu else 1)]
    return pl.pallas_call(
        functools.partial(_mm_kernel, nk=nk, act=act, has_bias=bias is not None, glu=glu),
        grid=(m // tm, nj, nk),
        in_specs=in_specs,
        out_specs=pl.BlockSpec((tm, tn), lambda i, j, k: (i, j)),
        out_shape=jax.ShapeDtypeStruct((m, n), out_dtype),
        scratch_shapes=scratch,
        compiler_params=_cparams("parallel", "parallel", "arbitrary"),
        name=name,
    )(*args)


def _head_sum(x, seg_ref):
    cw = seg_ref.shape[0]
    hi = x.astype(BF16)
    rest = x - hi.astype(F32)
    mid = rest.astype(BF16)
    lo = (rest - mid.astype(F32)).astype(BF16)
    outs = []
    for s in range(x.shape[1] // cw):
        sl = slice(s * cw, (s + 1) * cw)
        outs.append(jnp.dot(hi[:, sl], seg_ref[...], preferred_element_type=F32)
                    + jnp.dot(mid[:, sl], seg_ref[...], preferred_element_type=F32)
                    + jnp.dot(lo[:, sl], seg_ref[...], preferred_element_type=F32))
    return outs[0] if len(outs) == 1 else jnp.concatenate(outs, axis=1)


def _seg_ones(cw):
    idx = jnp.arange(cw) // HEAD
    return (idx[:, None] == idx[None, :]).astype(BF16)


def _prep_kernel(xp_ref, xc_ref, xn_ref, mod_ref, mix_ref, *rest, tr, n_lat, l_img, l_ctx):
    o_refs = rest[:6]
    hbuf, sbuf = rest[6], rest[7]
    i = pl.program_id(0)
    j = pl.program_id(1)
    sh = mod_ref[0, 0:1, :]
    sc = 1.0 + mod_ref[0, 1:2, :]
    hbuf[0:GRID_W, :] = xp_ref[...] * sc + sh
    hbuf[GRID_W:GRID_W + tr, :] = xc_ref[...] * sc + sh
    hbuf[GRID_W + tr:GRID_W + tr + GRID_W, :] = xn_ref[...] * sc + sh
    row = lax.broadcasted_iota(jnp.int32, (tr, 1), 0) + i * tr
    t_lat = row % l_img
    is_lat = i < n_lat // tr

    def shifted(s, keep):
        sbuf[...] = jnp.where(keep, hbuf[GRID_W + s:GRID_W + s + tr, :], 0.0)

    @pl.when(jnp.logical_and(is_lat, j == 0))
    def _():
        shifted(-1, t_lat % GRID_W != 0)

    @pl.when(jnp.logical_and(is_lat, j == 1))
    def _():
        shifted(1, t_lat % GRID_W != GRID_W - 1)

    @pl.when(jnp.logical_and(is_lat, j == 2))
    def _():
        shifted(-GRID_W, t_lat >= GRID_W)

    @pl.when(jnp.logical_and(is_lat, j == 3))
    def _():
        shifted(GRID_W, t_lat < l_img - GRID_W)

    t_ctx = (row - n_lat) % l_ctx

    @pl.when(jnp.logical_and(jnp.logical_not(is_lat), j < 2))
    def _():
        shifted(-1, t_ctx != 0)

    @pl.when(jnp.logical_and(jnp.logical_not(is_lat), j >= 2))
    def _():
        shifted(1, t_ctx != l_ctx - 1)

    h = hbuf[GRID_W:GRID_W + tr, :]
    xx = sbuf[...] - h
    for m in range(6):
        o_refs[m][...] = (h + xx * mix_ref[m:m + 1, :]).astype(BF16)


def _rwkv_prep(x, mod, mix, dims):
    r, d = x.shape
    tr = dims["tr"]
    dc = d // 4
    nb = tr // GRID_W
    last = r // GRID_W - 1
    kern = functools.partial(_prep_kernel, tr=tr, n_lat=dims["n_lat"], l_img=dims["L"], l_ctx=dims["Lc"])
    gmap = dims["gmap"]
    outs = pl.pallas_call(
        kern,
        grid=(r // tr, 4),
        in_specs=[
            pl.BlockSpec((GRID_W, dc), lambda i, j: (jnp.maximum(i * nb - 1, 0), j)),
            pl.BlockSpec((tr, dc), lambda i, j: (i, j)),
            pl.BlockSpec((GRID_W, dc), lambda i, j: (jnp.minimum(i * nb + nb, last), j)),
            pl.BlockSpec((1, 6, dc), lambda i, j: (gmap(i), 0, j)),
            pl.BlockSpec((6, dc), lambda i, j: (0, j)),
        ],
        out_specs=[pl.BlockSpec((tr, dc), lambda i, j: (i, j)) for _ in range(6)],
        out_shape=[jax.ShapeDtypeStruct((r, d), BF16) for _ in range(6)],
        scratch_shapes=[pltpu.VMEM((tr + 2 * GRID_W, dc), F32), pltpu.VMEM((tr, dc), F32)],
        compiler_params=_cparams("parallel", "parallel"),
        name="rwkv_prep",
    )(x, x, x, mod, mix)
    return outs


def _feat_kernel(*refs, has_vf, rw, ra):
    refs = list(refs)
    r_ref, k_ref, v_ref, tw_ref, ax_ref = [refs.pop(0) for _ in range(5)]
    w2_ref, a2_ref, w0_ref, a0_ref, vec_ref, seg_ref = [refs.pop(0) for _ in range(6)]
    if has_vf:
        vf_ref, xv_ref, v2_ref, v0_ref = [refs.pop(0) for _ in range(4)]
    kk_o, lw0_o, lw1_o, kd0_o, kd1_o, nb0_o, nb1_o, bonus_o = refs[:8]
    v_o = refs[8] if has_vf else None

    k = k_ref[...]
    r = r_ref[...]
    v = v_ref[...]
    k_k = vec_ref[0:1, :]
    k_a = vec_ref[1:2, :]
    r_k = vec_ref[2:3, :]
    kraw = k * k_k
    kk = kraw * lax.rsqrt(jnp.maximum(_head_sum(kraw * kraw, seg_ref), 1e-24))
    kk_o[...] = kk
    if has_vf:
        vl = v0_ref[...] + jnp.dot(xv_ref[...], v2_ref[...], preferred_element_type=F32)
        v = v + (vf_ref[...] - v) * jax.nn.sigmoid(vl)
        v_o[...] = v
    kd_sum = None
    for d, (lw_o, kd_o, nb_o) in enumerate(((lw0_o, kd0_o, nb0_o), (lw1_o, kd1_o, nb1_o))):
        wl = w0_ref[d:d + 1, :] + jnp.dot(tw_ref[:, d * rw:(d + 1) * rw], w2_ref[d],
                                          preferred_element_type=F32)
        lw_o[...] = -math.exp(-0.5) * jax.nn.sigmoid(wl)
        al = a0_ref[d:d + 1, :] + jnp.dot(ax_ref[:, d * ra:(d + 1) * ra], a2_ref[d],
                                          preferred_element_type=F32)
        a = jax.nn.sigmoid(al)
        kd = k * (1.0 + (a - 1.0) * k_a)
        kd_o[...] = kd
        nb_o[...] = -(kk * a)
        kd_sum = kd if kd_sum is None else kd_sum + kd
    bonus_o[...] = _head_sum(r * kd_sum * r_k, seg_ref) * v


def _rwkv_features(r, k, v, tw, ax, p, vf, xv):
    rows, d = r.shape
    tr = _pick(rows, (256, 128, 64, 32, 16, 8))
    cb = _pick(d, (1024, 512, 256, 128))
    cw = min(cb, MXU)
    has_vf = vf is not None
    rw = p["w2"].shape[1]
    ra = p["a2"].shape[1]
    tile = pl.BlockSpec((tr, cb), lambda i, j: (i, j))

    def full_rows(a):
        return pl.BlockSpec((tr, a.shape[1]), lambda i, j: (i, 0))

    vec = jnp.stack([p["k_k"], p["k_a"], p["r_k"]]).astype(F32)
    in_specs = [tile, tile, tile, full_rows(tw), full_rows(ax),
                pl.BlockSpec((2, rw, cb), lambda i, j: (0, 0, j)),
                pl.BlockSpec((2, ra, cb), lambda i, j: (0, 0, j)),
                pl.BlockSpec((2, cb), lambda i, j: (0, j)),
                pl.BlockSpec((2, cb), lambda i, j: (0, j)),
                pl.BlockSpec((3, cb), lambda i, j: (0, j)),
                pl.BlockSpec((cw, cw), lambda i, j: (0, 0))]
    args = [r, k, v, tw, ax, p["w2"], p["a2"], p["w0"], p["a0"], vec, _seg_ones(cw)]
    n_out = 8
    if has_vf:
        in_specs += [tile, full_rows(xv),
                     pl.BlockSpec((p["v2"].shape[0], cb), lambda i, j: (0, j)),
                     pl.BlockSpec((1, cb), lambda i, j: (0, j))]
        args += [vf, xv, p["v2"], p["v0"].reshape(1, -1)]
        n_out = 9
    return pl.pallas_call(
        functools.partial(_feat_kernel, has_vf=has_vf, rw=rw, ra=ra),
        grid=(rows // tr, d // cb),
        in_specs=in_specs,
        out_specs=[tile] * n_out,
        out_shape=[jax.ShapeDtypeStruct((rows, d), F32)] * n_out,
        compiler_params=_cparams("parallel", "parallel"),
        name="rwkv_features",
    )(*args)


def _scan_kernel(*refs, reverse, has_prev, prec):
    refs = list(refs)
    r_ref, lw_ref, kd_ref, kk_ref, nb_ref, v_ref = [refs.pop(0) for _ in range(6)]
    yp_ref = refs.pop(0) if has_prev else None
    y_ref, s_ref = refs
    tt = CHUNK
    gw = HPG * HEAD
    n_groups = y_ref.shape[1] // gw

    @pl.when(pl.program_id(1) == 0)
    def _():
        s_ref[...] = jnp.zeros_like(s_ref)

    row_t = lax.broadcasted_iota(jnp.int32, (tt, gw), 0)
    lane = lax.broadcasted_iota(jnp.int32, (tt, gw), 1)
    pos = lane % HEAD
    if reverse:
        strict, incl = pos > row_t, pos >= row_t
    else:
        strict, incl = pos < row_t, pos <= row_t
    eye = (pos == row_t).astype(F32)
    near = {}
    m = INV_BASE
    while m <= tt:
        near[m] = (pos // m) == (row_t // m)
        m *= 2
    ci = lax.broadcasted_iota(jnp.int32, (tt, tt), 0)
    cj = lax.broadcasted_iota(jnp.int32, (tt, tt), 1)
    csum = ((cj >= ci) if reverse else (cj <= ci)).astype(BF16)
    head_of_row = lax.broadcasted_iota(jnp.int32, (gw, gw), 0) // HEAD
    head_of_lane = lax.broadcasted_iota(jnp.int32, (gw, gw), 1) // HEAD
    same_head = head_of_row == head_of_lane
    last = 0 if reverse else tt - 1

    def mm(a, b):
        return jnp.dot(a.astype(BF16) if prec is None else a, b.astype(BF16) if prec is None else b,
                       preferred_element_type=F32, precision=prec)

    def mm_nt(a, b):
        a, b = (a.astype(BF16), b.astype(BF16)) if prec is None else (a, b)
        return lax.dot_general(a, b, (((1,), (1,)), ((), ())), preferred_element_type=F32, precision=prec)

    def mm_tn(a, b):
        a, b = (a.astype(BF16), b.astype(BF16)) if prec is None else (a, b)
        return lax.dot_general(a, b, (((0,), (0,)), ((), ())), preferred_element_type=F32, precision=prec)

    def blockdiag(m):
        return jnp.where(same_head, jnp.concatenate([m] * HPG, axis=0), 0.0)

    def group(sl, s0):
        lw = lw_ref[:, sl]
        lw_hi = lw.astype(BF16)
        lw_mid = (lw - lw_hi.astype(F32)).astype(BF16)
        lw_lo = (lw - lw_hi.astype(F32) - lw_mid.astype(F32)).astype(BF16)
        c_in = (jnp.dot(csum, lw_hi, preferred_element_type=F32)
                + jnp.dot(csum, lw_mid, preferred_element_type=F32)
                + jnp.dot(csum, lw_lo, preferred_element_type=F32))
        c_tot = c_in[last:last + 1, :]
        g_inv = jnp.exp(-c_in)
        at = kk_ref[:, sl] * jnp.exp(c_in - lw)
        rt = r_ref[:, sl] * jnp.exp(c_in)
        nb = nb_ref[:, sl]
        kd = kd_ref[:, sl]
        v = v_ref[:, sl]
        lr = jnp.concatenate([at, rt], axis=0)
        yield
        ar = mm_nt(lr, s0)
        g_nb = mm_nt(lr, blockdiag(nb * g_inv))
        g_k = mm_nt(lr, blockdiag(kd * g_inv))
        yield
        n_mat = jnp.where(strict, g_nb[:tt], 0.0)
        m_ak = jnp.where(strict, g_k[:tt], 0.0)
        m_rb = jnp.where(incl, g_nb[tt:], 0.0)
        m_rk = jnp.where(incl, g_k[tt:], 0.0)
        mv = mm(jnp.concatenate([m_ak, m_rk], axis=0), blockdiag(v))
        x = ar[:tt] + mv[:tt]
        n_d = jnp.where(near[INV_BASE], n_mat, 0.0)
        yield
        p = mm(n_d, blockdiag(n_d))
        t_inv = eye + n_d
        yield
        t_inv = t_inv + mm(t_inv, blockdiag(p))
        m = INV_BASE
        while m < tt:
            n_off = jnp.where(jnp.logical_and(near[2 * m], jnp.logical_not(near[m])), n_mat, 0.0)
            t_bd = blockdiag(t_inv)
            yield
            e = mm(n_off, t_bd)
            yield
            t_inv = t_inv + mm(t_inv, blockdiag(e))
            m *= 2
        yield
        u = mm(t_inv, blockdiag(x))
        yield
        y = ar[tt:] + mm(m_rb, blockdiag(u)) + mv[tt:]
        if has_prev:
            y = y + yp_ref[:, sl]
        rem = jnp.exp(c_tot - c_in)
        upd = mm_tn(jnp.concatenate([u, v], axis=0), jnp.concatenate([nb * rem, kd * rem], axis=0))
        return y, s0 * jnp.exp(c_tot) + jnp.where(same_head, upd, 0.0)

    par = _pick(n_groups, (GROUPS_PER_TRIP, 4, 2, 1))

    def trip(i, carry):
        gs = [i * par + u for u in range(par)]
        sls = [pl.ds(pl.multiple_of(g * gw, gw), gw) for g in gs]
        chains = [group(sl, s_ref[g]) for g, sl in zip(gs, sls)]
        outs = [None] * par
        while any(o is None for o in outs):
            for u, chain in enumerate(chains):
                if outs[u] is None:
                    try:
                        next(chain)
                    except StopIteration as done:
                        outs[u] = done.value
        for g, sl, (y, s_new) in zip(gs, sls, outs):
            y_ref[:, sl] = y
            s_ref[g] = s_new
        return carry

    lax.fori_loop(0, n_groups // par, trip, 0)


def _rwkv_scan(r, lw, kd, kk, nb, v, y_prev, reverse, dims, prec=None):
    rows, d = r.shape
    bsz, l, lc, n_lat = dims["B"], dims["L"], dims["Lc"], dims["n_lat"]
    tt = CHUNK
    n_ctx, n_latc = lc // tt, l // tt
    nc = n_ctx + n_latc

    def rblk(b, c):
        if reverse:
            ctx_blk = (n_lat + b * lc) // tt + (n_ctx - 1 - c)
            lat_blk = b * n_latc + (n_latc - 1 - (c - n_ctx))
        else:
            ctx_blk = (n_lat + b * lc) // tt + c
            lat_blk = b * n_latc + (c - n_ctx)
        return jnp.where(c < n_ctx, ctx_blk, lat_blk)

    blk = pl.BlockSpec((tt, d), lambda b, c: (rblk(b, c), 0))
    args = [r, lw, kd, kk, nb, v]
    if y_prev is not None:
        args.append(y_prev)
    gw = HPG * HEAD
    return pl.pallas_call(
        functools.partial(_scan_kernel, reverse=reverse, has_prev=y_prev is not None, prec=prec),
        grid=(bsz, nc),
        in_specs=[blk] * len(args),
        out_specs=blk,
        out_shape=jax.ShapeDtypeStruct((rows, d), F32),
        scratch_shapes=[pltpu.VMEM((d // gw, gw, gw), F32)],
        compiler_params=_cparams("parallel", "arbitrary"),
        name="rwkv_scan_rev" if reverse else "rwkv_scan_fwd",
    )(*args)


def _readout_kernel(ys_ref, bonus_ref, sg_ref, g2_ref, lnx_ref, seg_ref, o_ref):
    ys = ys_ref[...]
    inv = 1.0 / HEAD
    mu = _head_sum(ys, seg_ref) * inv
    dlt = ys - mu
    var = _head_sum(dlt * dlt, seg_ref) * inv
    y = dlt * lax.rsqrt(var + GN_EPS) * lnx_ref[0:1, :] + lnx_ref[1:2, :] + bonus_ref[...]
    g = jnp.dot(sg_ref[...], g2_ref[...], preferred_element_type=F32)
    o_ref[...] = (y * g).astype(o_ref.dtype)


def _rwkv_readout(ys, bonus, sg, g2, lnx_g, lnx_b):
    rows, d = ys.shape
    tr = _pick(rows, (256, 128, 64, 32, 16, 8))
    cb = _pick(d, (1024, 512, 256, 128))
    cw = min(cb, MXU)
    tile = pl.BlockSpec((tr, cb), lambda i, j: (i, j))
    lnx = jnp.stack([lnx_g, lnx_b]).astype(F32)
    return pl.pallas_call(
        _readout_kernel,
        grid=(rows // tr, d // cb),
        in_specs=[tile, tile,
                  pl.BlockSpec((tr, sg.shape[1]), lambda i, j: (i, 0)),
                  pl.BlockSpec((g2.shape[0], cb), lambda i, j: (0, j)),
                  pl.BlockSpec((2, cb), lambda i, j: (0, j)),
                  pl.BlockSpec((cw, cw), lambda i, j: (0, 0))],
        out_specs=tile,
        out_shape=jax.ShapeDtypeStruct((rows, d), BF16),
        compiler_params=_cparams("parallel", "parallel"),
        name="rwkv_readout",
    )(ys, bonus, sg, g2, lnx, _seg_ones(cw))


def _resid_ln_kernel(x_ref, y_ref, mod_ref, ln_ref, xo_ref, *rest, alpha, gate_row, mod_rows):
    gate = mod_ref[0, gate_row:gate_row + 1, :]
    z = alpha * x_ref[...] + gate * y_ref[...]
    mu = jnp.mean(z, axis=-1, keepdims=True)
    dz = z - mu
    var = jnp.mean(dz * dz, axis=-1, keepdims=True)
    zn = dz * lax.rsqrt(var + LN_EPS) * ln_ref[0:1, :] + ln_ref[1:2, :]
    xo_ref[...] = zn
    if mod_rows is not None:
        sh = mod_ref[0, mod_rows[0]:mod_rows[0] + 1, :]
        sc = mod_ref[0, mod_rows[1]:mod_rows[1] + 1, :]
        rest[0][...] = (zn * (1.0 + sc) + sh).astype(rest[0].dtype)


def _resid_ln(x, y, mod, ln_g, ln_b, alpha, gate_row, mod_rows, dims, rows=None):
    d = x.shape[1]
    rows = rows or x.shape[0]
    tr = _pick(dims["tr"], (256, 128, 64, 32, 16, 8))
    gmap = dims["gmap"]
    ratio = dims["tr"] // tr
    tile = pl.BlockSpec((tr, d), lambda i: (i, 0))
    ln = jnp.stack([ln_g, ln_b]).astype(F32)
    out_shape = [jax.ShapeDtypeStruct((rows, d), F32)]
    out_specs = [tile]
    if mod_rows is not None:
        out_shape.append(jax.ShapeDtypeStruct((rows, d), BF16))
        out_specs.append(tile)
    outs = pl.pallas_call(
        functools.partial(_resid_ln_kernel, alpha=alpha, gate_row=gate_row, mod_rows=mod_rows),
        grid=(rows // tr,),
        in_specs=[tile, tile,
                  pl.BlockSpec((1, 6, d), lambda i: (gmap(i // ratio), 0, 0)),
                  pl.BlockSpec((2, d), lambda i: (0, 0))],
        out_specs=out_specs,
        out_shape=out_shape,
        compiler_params=_cparams("parallel"),
        name="resid_ln",
    )(x, y, mod, ln)
    return outs


def _s5_kernel(*refs, tcs, nq, ns, finish):
    refs = list(refs)
    x_ref, mod_ref, bm_ref, cm_ref, lam_ref = [refs.pop(0) for _ in range(5)]
    if finish:
        yf_ref, d_ref = refs.pop(0), refs.pop(0)
    y_ref = refs.pop(0)
    hre_refs = [refs.pop(0) for _ in range(nq)]
    him_refs = [refs.pop(0) for _ in range(nq)]
    h2d_refs = [refs.pop(0) for _ in range(nq)]
    cr_ref, ci_ref = refs
    reverse = finish
    nsl = ns // LANE

    @pl.when(pl.program_id(2) == 0)
    def _():
        cr_ref[...] = jnp.zeros_like(cr_ref)
        ci_ref[...] = jnp.zeros_like(ci_ref)

    def u_of(q):
        sl = slice(q * S5_CH, (q + 1) * S5_CH)
        return x_ref[:, sl] * (1.0 + mod_ref[0, 1:2, sl]) + mod_ref[0, 0:1, sl]

    tb = 8
    ntrip = tcs // tb
    npiece = 1

    def stream(qs):
        proj_in, scan, proj_out = [], [], []
        state = {}

        def in_piece(q, pc):
            def run():
                cw = 2 * ns // npiece
                bu = jnp.dot(u_of(q).astype(BF16), bm_ref[q, :, pc * cw:(pc + 1) * cw],
                             preferred_element_type=F32)
                for j in range(cw // LANE):
                    col = pc * cw + j * LANE
                    ref, s = (hre_refs[q], col // LANE) if col < ns else (him_refs[q], (col - ns) // LANE)
                    ref[pl.ds(s, tcs, stride=nsl), :] = bu[:, j * LANE:(j + 1) * LANE]
            return run

        def scan_piece(gi):
            def run():
                if gi == 0:
                    for q in qs:
                        state[q] = (cr_ref[q], ci_ref[q], lam_ref[q, 0], lam_ref[q, 1])
                g = ntrip - 1 - gi if reverse else gi
                for j in range(tb):
                    off = (g * tb + (tb - 1 - j if reverse else j)) * nsl
                    for q in qs:
                        hr, hi, lr, li = state[q]
                        nr = lr * hr - li * hi + hre_refs[q][pl.ds(off, nsl), :]
                        ni = lr * hi + li * hr + him_refs[q][pl.ds(off, nsl), :]
                        hre_refs[q][pl.ds(off, nsl), :] = nr
                        him_refs[q][pl.ds(off, nsl), :] = ni
                        state[q] = (nr, ni, lr, li)
                if gi == ntrip - 1:
                    for q in qs:
                        cr_ref[q] = state[q][0]
                        ci_ref[q] = state[q][1]
            return run

        def out_piece(q, pc):
            def run():
                for s in range(pc * nsl // npiece, (pc + 1) * nsl // npiece):
                    h2d_refs[q][:, s * LANE:(s + 1) * LANE] = \
                        hre_refs[q][pl.ds(s, tcs, stride=nsl), :].astype(BF16)
                    h2d_refs[q][:, ns + s * LANE:ns + (s + 1) * LANE] = \
                        him_refs[q][pl.ds(s, tcs, stride=nsl), :].astype(BF16)
            return run

        def out_dot(q):
            def run():
                sl = slice(q * S5_CH, (q + 1) * S5_CH)
                y = jnp.dot(h2d_refs[q][...], cm_ref[q], preferred_element_type=F32)
                if finish:
                    y = y + yf_ref[:, sl] + d_ref[:, sl] * u_of(q)
                    c = math.sqrt(2.0 / math.pi)
                    y = 0.5 * y * (1.0 + jnp.tanh(c * (y + 0.044715 * (y * y * y))))
                y_ref[:, sl] = y.astype(y_ref.dtype)
            return run

        for q in qs:
            proj_in += [in_piece(q, pc) for pc in range(npiece)]
            proj_out += [out_piece(q, pc) for pc in range(npiece)] + [out_dot(q)]
        scan += [scan_piece(gi) for gi in range(ntrip)]
        return proj_in, scan, proj_out

    s_in, s_scan, s_out = stream(list(range(nq)))
    for piece in s_in + s_scan + s_out:
        piece()


def _s5_scan(x, mod, bm, cm, lam, y_fwd, dvec, dims):
    reverse = y_fwd is not None
    rows, d = x.shape
    tcs = dims["tcs"]
    ns2 = bm.shape[2]
    ns = ns2 // 2
    nq = _pick(d // S5_CH, (4, 2, 1))
    cbw = nq * S5_CH
    n_lat_blk = dims["L"] // tcs
    n_ctx_blk = dims["Lc"] // tcs
    n_lat_rows_blk = dims["n_lat"] // tcs
    nchunk = n_ctx_blk + n_lat_blk

    def rmap(b, c):
        if reverse:
            ctx_blk = n_lat_rows_blk + b * n_ctx_blk + (n_ctx_blk - 1 - c)
            lat_blk = b * n_lat_blk + (n_lat_blk - 1 - (c - n_ctx_blk))
        else:
            ctx_blk = n_lat_rows_blk + b * n_ctx_blk + c
            lat_blk = b * n_lat_blk + (c - n_ctx_blk)
        return jnp.where(c < n_ctx_blk, ctx_blk, lat_blk)

    bsz = dims["B"]
    tile = pl.BlockSpec((tcs, cbw), lambda b, g, c: (rmap(b, c), g))
    in_specs = [tile,
                pl.BlockSpec((1, 6, cbw), lambda b, g, c: (jnp.where(c < n_ctx_blk, bsz, b), 0, g)),
                pl.BlockSpec((nq, S5_CH, ns2), lambda b, g, c: (g, 0, 0)),
                pl.BlockSpec((nq, ns2, S5_CH), lambda b, g, c: (g, 0, 0)),
                pl.BlockSpec((nq, 2, ns // LANE, LANE), lambda b, g, c: (g, 0, 0, 0))]
    args = [x, mod, bm, cm, lam]
    if reverse:
        in_specs += [tile, pl.BlockSpec((1, cbw), lambda b, g, c: (0, g))]
        args += [y_fwd, dvec.reshape(1, -1).astype(F32)]
    return pl.pallas_call(
        functools.partial(_s5_kernel, tcs=tcs, nq=nq, ns=ns, finish=reverse),
        grid=(bsz, d // cbw, nchunk),
        in_specs=in_specs,
        out_specs=tile,
        out_shape=jax.ShapeDtypeStruct((rows, d), BF16 if reverse else F32),
        scratch_shapes=([pltpu.VMEM((tcs * ns // LANE, LANE), F32)] * (2 * nq)
                        + [pltpu.VMEM((tcs, ns2), BF16)] * nq
                        + [pltpu.VMEM((nq, ns // LANE, LANE), F32)] * 2),
        compiler_params=_cparams("parallel", "parallel", "arbitrary"),
        name="s5_scan_rev" if reverse else "s5_scan_fwd",
    )(*args)


def _s5_params(lam_re, lam_im, log_dt, b_re, b_im, c_re, c_im, gpb):
    lam = lax.complex(lam_re.astype(F32), lam_im.astype(F32))
    dt = jnp.exp(log_dt.astype(F32))[..., None]
    lam_bar = jnp.exp(lam * dt)
    b_bar = ((lam_bar - 1.0) / lam)[..., None] * lax.complex(b_re.astype(F32), b_im.astype(F32))
    c_mat = lax.complex(c_re.astype(F32), c_im.astype(F32))
    ndir, g, p = lam_bar.shape
    i_sz = b_bar.shape[-1]
    nblk = g // gpb
    eye = jnp.eye(gpb, dtype=F32)

    def blockdiag_in(m):
        m = m.reshape(ndir, nblk, gpb, p, i_sz)
        out = jnp.einsum("dngpi,gh->dngihp", m, eye)
        return out.reshape(ndir, nblk, gpb * i_sz, gpb * p)

    def blockdiag_out(m):
        m = m.reshape(ndir, nblk, gpb, i_sz, p)
        out = jnp.einsum("dngip,gh->dngphi", m, eye)
        return out.reshape(ndir, nblk, gpb * p, gpb * i_sz)

    bm = jnp.concatenate([blockdiag_in(jnp.real(b_bar)), blockdiag_in(jnp.imag(b_bar))], axis=-1).astype(BF16)
    cm = jnp.concatenate([blockdiag_out(jnp.real(c_mat)), blockdiag_out(-jnp.imag(c_mat))], axis=-2).astype(BF16)
    lam_t = jnp.stack([jnp.real(lam_bar), jnp.imag(lam_bar)], axis=1)
    lam_t = lam_t.reshape(ndir, 2, nblk, gpb * p // LANE, LANE).transpose(0, 2, 1, 3, 4).astype(F32)
    return bm, cm, lam_t


def kernel(x, c, ctx, c_ctx, ada_down, ada_up, ada_bias, ln_g, ln_b, rw_mix, rw_wr, rw_wk, rw_wv, rw_wo, rw_w0, rw_w1, rw_w2, rw_a0, rw_a1, rw_a2, rw_v0, rw_v1, rw_v2, rw_g1, rw_g2, rw_kk, rw_ka, rw_rk, rw_lnx_g, rw_lnx_b, s5_lam_re, s5_lam_im, s5_log_dt, s5_b_re, s5_b_im, s5_c_re, s5_c_im, s5_d, s5_glu_w, s5_glu_b, mlp_w1, mlp_w2):
    bsz, seq, d = x.shape
    lc = ctx.shape[1]
    depth = ada_down.shape[0]
    assert d % S5_CH == 0 and seq % CHUNK == 0 and lc % CHUNK == 0
    alpha = (2 * depth) ** 0.25
    n_lat = bsz * seq
    tr = _pick(math.gcd(seq, bsz * lc), (512, 256, 128, 64))
    tiles_per_batch = seq // tr

    def gmap(i):
        return jnp.minimum(i // tiles_per_batch, bsz)

    g_t = math.gcd(seq, lc)
    dims = {"B": bsz, "L": seq, "Lc": lc, "n_lat": n_lat, "tr": tr, "gmap": gmap,
            "tcs": _pick(g_t, (256, 128, 64, 32, 16, 8))}

    xs = jnp.concatenate([x.reshape(n_lat, d), ctx.reshape(bsz * lc, d)], axis=0).astype(F32)
    vf = None

    cvec = jnp.concatenate([c, c_ctx[None, :]], axis=0)
    cvec = _pad_to(jax.nn.silu(cvec), 0, 16).astype(BF16)
    wr_b, wk_b, wv_b, wo_b = (w.astype(BF16) for w in (rw_wr, rw_wk, rw_wv, rw_wo))
    glu_b, w1_b, w2_b = (w.astype(BF16) for w in (s5_glu_w, mlp_w1, mlp_w2))

    for i in range(depth):
        low = _mm(cvec, ada_down[i].astype(BF16), out_dtype=BF16, name="adaln_down")
        mod = _mm(low, ada_up[i].astype(BF16), bias=ada_bias[i], name="adaln_up")
        mod = mod[:bsz + 1].reshape(bsz + 1, 6, d)
        j = i // 2
        if i % 2 == 0:
            p = {"w2": _pad_to(rw_w2[j], 1, LANE).astype(BF16), "a2": _pad_to(rw_a2[j], 1, LANE).astype(BF16),
                 "w0": rw_w0[j], "a0": rw_a0[j], "k_k": rw_kk[j], "k_a": rw_ka[j],
                 "r_k": rw_rk[j].reshape(-1)}
            lerps = _rwkv_prep(xs, mod, rw_mix[j], dims)
            r = _mm(lerps[0], wr_b, layer=j, name="rwkv_r")
            k = _mm(lerps[2], wk_b, layer=j, name="rwkv_k")
            v = _mm(lerps[3], wv_b, layer=j, name="rwkv_v")
            w1cat = jnp.concatenate([_pad_to(rw_w1[j, dd], 1, LANE) for dd in range(2)], axis=1)
            a1cat = jnp.concatenate([_pad_to(rw_a1[j, dd], 1, LANE) for dd in range(2)], axis=1)
            tw = _mm(lerps[1], w1cat.astype(BF16), act="tanh", out_dtype=BF16, name="rwkv_w1")
            ax = _mm(lerps[4], a1cat.astype(BF16), out_dtype=BF16, name="rwkv_a1")
            sg = _mm(lerps[5], _pad_to(rw_g1[j], 1, LANE).astype(BF16), act="sigmoid", out_dtype=BF16,
                     name="rwkv_g1")
            g2 = _pad_to(rw_g2[j], 0, LANE).astype(BF16)
            xv = None
            if j > 0:
                xv = _mm(lerps[3], _pad_to(rw_v1[j - 1], 1, LANE).astype(BF16), out_dtype=BF16, name="rwkv_v1")
                p["v2"] = _pad_to(rw_v2[j - 1], 0, LANE).astype(BF16)
                p["v0"] = rw_v0[j - 1]
            feats = _rwkv_features(r, k, v, tw, ax, p, vf if j > 0 else None, xv)
            kk, lw0, lw1, kd0, kd1, nb0, nb1, bonus = feats[:8]
            if j > 0:
                v = feats[8]
            else:
                vf = v
            y = _rwkv_scan(r, lw0, kd0, kk, nb0, v, None, False, dims)
            y = _rwkv_scan(r, lw1, kd1, kk, nb1, v, y, True, dims)
            yg = _rwkv_readout(y, bonus, sg, g2, rw_lnx_g[j], rw_lnx_b[j])
            mix_out = _mm(yg, wo_b, layer=j, name="rwkv_o")
        else:
            gpb = S5_CH // S5_GROUP
            bm, cm, lam_t = _s5_params(s5_lam_re[j], s5_lam_im[j], s5_log_dt[j], s5_b_re[j], s5_b_im[j],
                                       s5_c_re[j], s5_c_im[j], gpb)
            yf = _s5_scan(xs, mod, bm[0], cm[0], lam_t[0], None, None, dims)
            gl = _s5_scan(xs, mod, bm[1], cm[1], lam_t[1], yf, s5_d[j], dims)
            mix_out = _mm(gl, glu_b, layer=j, bias=s5_glu_b[j], glu=True,
                          tm=_pick(gl.shape[0], (768, 512, 256, 128, 64, 32, 16, 8)), name="s5_glu")
        xs, h2 = _resid_ln(xs, mix_out, mod, ln_g[i, 0], ln_b[i, 0], alpha, 2, (3, 4), dims)
        a1 = _mm(h2, w1_b, layer=i, act="relu2", out_dtype=BF16, name="mlp_w1")
        mlp_out = _mm(a1, w2_b, layer=i, tm=_pick(a1.shape[0], (768, 512, 256, 128, 64, 32, 16, 8)),
                      tn=_pick(d, (1024, 512, 256, 128)), tk=_pick(a1.shape[1], (4096, 2048, 1024, 512, 256, 128)),
                      name="mlp_w2")
        (xs,) = _resid_ln(xs, mlp_out, mod, ln_g[i, 1], ln_b[i, 1], alpha, 5, None, dims,
                          rows=n_lat if i == depth - 1 else None)
    return xs.reshape(bsz, seq, d).astype(x.dtype)
```

```python
import functools
import math

import jax
import jax.numpy as jnp
from jax import lax
from jax.experimental import pallas as pl
from jax.experimental.pallas import tpu as pltpu

F32 = jnp.float32
BF16 = jnp.bfloat16

GRID_W = 64
HEAD = 64
S5_GROUP = 16
S5_STATE = 64
S5_CH = 256
LN_EPS = 1e-6
GN_EPS = 64e-5
LANE = 128
SUBLANE = 8
MXU = 256
HPG = MXU // HEAD
INV_BASE = 4
GROUPS_PER_TRIP = 16
CHUNK = 64
VMEM_LIMIT = 56 * 1024 * 1024


def _cparams(*sem):
    return pltpu.CompilerParams(dimension_semantics=sem, vmem_limit_bytes=VMEM_LIMIT)


def _pick(n, prefs):
    for p in prefs:
        if n % p == 0:
            return p
    return n


def _pad_to(a, axis, mult):
    n = a.shape[axis]
    r = (-n) % mult
    if r == 0:
        return a
    pad = [(0, 0)] * a.ndim
    pad[axis] = (0, r)
    return jnp.pad(a, pad)


def _mm_kernel(*refs, nk, act, has_bias, glu):
    refs = list(refs)
    x_ref = refs.pop(0)
    w_refs = [refs.pop(0) for _ in range(2 if glu else 1)]
    b_refs = [refs.pop(0) for _ in range((2 if glu else 1) if has_bias else 0)]
    o_ref = refs.pop(0)
    acc_refs = refs
    k = pl.program_id(2)

    def finish(zs):
        if has_bias:
            zs = [z + b[...] for z, b in zip(zs, b_refs)]
        if glu:
            z = zs[0] * jax.nn.sigmoid(zs[1])
        else:
            z = zs[0]
            if act == "tanh":
                z = jnp.tanh(z)
            elif act == "sigmoid":
                z = jax.nn.sigmoid(z)
            elif act == "relu2":
                z = jnp.square(jnp.maximum(z, 0.0))
        o_ref[...] = z.astype(o_ref.dtype)

    x = x_ref[...]
    parts = [jnp.dot(x, w[...], preferred_element_type=F32) for w in w_refs]
    if nk == 1:
        finish(parts)
        return

    @pl.when(k == 0)
    def _():
        for a in acc_refs:
            a[...] = jnp.zeros_like(a)

    for a, p in zip(acc_refs, parts):
        a[...] += p

    @pl.when(k == nk - 1)
    def _():
        finish([a[...] for a in acc_refs])


def _mm(x, w, bias=None, act=None, out_dtype=F32, glu=False, tm=None, tn=None, tk=None, layer=None,
        name="mm"):
    m, kdim = x.shape
    n = w.shape[-1] // (2 if glu else 1)
    tm = tm or _pick(m, (1536, 1024, 768, 512, 256, 128, 64, 32, 16, 8))
    tn = tn or _pick(n, (512, 256, 128))
    tk = tk or (kdim if kdim <= 4096 else _pick(kdim, (2048, 1024, 512, 256, 128)))
    nk = kdim // tk
    nj = n // tn

    def w_spec(off):
        if layer is None:
            return pl.BlockSpec((tk, tn), lambda i, j, k: (k, j + off))
        return pl.BlockSpec((None, tk, tn), lambda i, j, k: (layer, k, j + off))

    in_specs = [pl.BlockSpec((tm, tk), lambda i, j, k: (i, k)), w_spec(0)]
    args = [x, w]
    if glu:
        in_specs.append(w_spec(nj))
        args.append(w)
    if bias is not None:
        b2 = bias.reshape(1, -1).astype(F32)
        in_specs.append(pl.BlockSpec((1, tn), lambda i, j, k: (0, j)))
        args.append(b2)
        if glu:
            in_specs.append(pl.BlockSpec((1, tn), lambda i, j, k: (0, j + nj)))
            args.append(b2)
    scratch = [] if nk == 1 else [pltpu.VMEM((tm, tn), F32) for _ in range(2 if glu else 1)]
    return pl.pallas_call(
        functools.partial(_mm_kernel, nk=nk, act=act, has_bias=bias is not None, glu=glu),
        grid=(m // tm, nj, nk),
        in_specs=in_specs,
        out_specs=pl.BlockSpec((tm, tn), lambda i, j, k: (i, j)),
        out_shape=jax.ShapeDtypeStruct((m, n), out_dtype),
        scratch_shapes=scratch,
        compiler_params=_cparams("parallel", "parallel", "arbitrary"),
        name=name,
    )(*args)


def _head_sum(x, seg_ref):
    cw = seg_ref.shape[0]
    hi = x.astype(BF16)
    rest = x - hi.astype(F32)
    mid = rest.astype(BF16)
    lo = (rest - mid.astype(F32)).astype(BF16)
    outs = []
    for s in range(x.shape[1] // cw):
        sl = slice(s * cw, (s + 1) * cw)
        outs.append(jnp.dot(hi[:, sl], seg_ref[...], preferred_element_type=F32)
                    + jnp.dot(mid[:, sl], seg_ref[...], preferred_element_type=F32)
                    + jnp.dot(lo[:, sl], seg_ref[...], preferred_element_type=F32))
    return outs[0] if len(outs) == 1 else jnp.concatenate(outs, axis=1)


def _seg_ones(cw):
    idx = jnp.arange(cw) // HEAD
    return (idx[:, None] == idx[None, :]).astype(BF16)


def _prep_kernel(xp_ref, xc_ref, xn_ref, mod_ref, mix_ref, *rest, tr, n_lat, l_img, l_ctx):
    o_refs = rest[:6]
    hbuf, sbuf = rest[6], rest[7]
    i = pl.program_id(0)
    j = pl.program_id(1)
    sh = mod_ref[0, 0:1, :]
    sc = 1.0 + mod_ref[0, 1:2, :]
    hbuf[0:GRID_W, :] = xp_ref[...] * sc + sh
    hbuf[GRID_W:GRID_W + tr, :] = xc_ref[...] * sc + sh
    hbuf[GRID_W + tr:GRID_W + tr + GRID_W, :] = xn_ref[...] * sc + sh
    row = lax.broadcasted_iota(jnp.int32, (tr, 1), 0) + i * tr
    t_lat = row % l_img
    is_lat = i < n_lat // tr

    def shifted(s, keep):
        sbuf[...] = jnp.where(keep, hbuf[GRID_W + s:GRID_W + s + tr, :], 0.0)

    @pl.when(jnp.logical_and(is_lat, j == 0))
    def _():
        shifted(-1, t_lat % GRID_W != 0)

    @pl.when(jnp.logical_and(is_lat, j == 1))
    def _():
        shifted(1, t_lat % GRID_W != GRID_W - 1)

    @pl.when(jnp.logical_and(is_lat, j == 2))
    def _():
        shifted(-GRID_W, t_lat >= GRID_W)

    @pl.when(jnp.logical_and(is_lat, j == 3))
    def _():
        shifted(GRID_W, t_lat < l_img - GRID_W)

    t_ctx = (row - n_lat) % l_ctx

    @pl.when(jnp.logical_and(jnp.logical_not(is_lat), j < 2))
    def _():
        shifted(-1, t_ctx != 0)

    @pl.when(jnp.logical_and(jnp.logical_not(is_lat), j >= 2))
    def _():
        shifted(1, t_ctx != l_ctx - 1)

    h = hbuf[GRID_W:GRID_W + tr, :]
    xx = sbuf[...] - h
    for m in range(6):
        o_refs[m][...] = (h + xx * mix_ref[m:m + 1, :]).astype(BF16)


def _rwkv_prep(x, mod, mix, dims):
    r, d = x.shape
    tr = dims["tr"]
    dc = d // 4
    nb = tr // GRID_W
    last = r // GRID_W - 1
    kern = functools.partial(_prep_kernel, tr=tr, n_lat=dims["n_lat"], l_img=dims["L"], l_ctx=dims["Lc"])
    gmap = dims["gmap"]
    outs = pl.pallas_call(
        kern,
        grid=(r // tr, 4),
        in_specs=[
            pl.BlockSpec((GRID_W, dc), lambda i, j: (jnp.maximum(i * nb - 1, 0), j)),
            pl.BlockSpec((tr, dc), lambda i, j: (i, j)),
            pl.BlockSpec((GRID_W, dc), lambda i, j: (jnp.minimum(i * nb + nb, last), j)),
            pl.BlockSpec((1, 6, dc), lambda i, j: (gmap(i), 0, j)),
            pl.BlockSpec((6, dc), lambda i, j: (0, j)),
        ],
        out_specs=[pl.BlockSpec((tr, dc), lambda i, j: (i, j)) for _ in range(6)],
        out_shape=[jax.ShapeDtypeStruct((r, d), BF16) for _ in range(6)],
        scratch_shapes=[pltpu.VMEM((tr + 2 * GRID_W, dc), F32), pltpu.VMEM((tr, dc), F32)],
        compiler_params=_cparams("parallel", "parallel"),
        name="rwkv_prep",
    )(x, x, x, mod, mix)
    return outs


def _feat_kernel(*refs, has_vf, rw, ra):
    refs = list(refs)
    r_ref, k_ref, v_ref, tw_ref, ax_ref = [refs.pop(0) for _ in range(5)]
    w2_ref, a2_ref, w0_ref, a0_ref, vec_ref, seg_ref = [refs.pop(0) for _ in range(6)]
    if has_vf:
        vf_ref, xv_ref, v2_ref, v0_ref = [refs.pop(0) for _ in range(4)]
    kk_o, lw0_o, lw1_o, kd0_o, kd1_o, nb0_o, nb1_o, bonus_o = refs[:8]
    v_o = refs[8] if has_vf else None

    k = k_ref[...]
    r = r_ref[...]
    v = v_ref[...]
    k_k = vec_ref[0:1, :]
    k_a = vec_ref[1:2, :]
    r_k = vec_ref[2:3, :]
    kraw = k * k_k
    kk = kraw * lax.rsqrt(jnp.maximum(_head_sum(kraw * kraw, seg_ref), 1e-24))
    kk_o[...] = kk
    if has_vf:
        vl = v0_ref[...] + jnp.dot(xv_ref[...], v2_ref[...], preferred_element_type=F32)
        v = v + (vf_ref[...] - v) * jax.nn.sigmoid(vl)
        v_o[...] = v
    kd_sum = None
    for d, (lw_o, kd_o, nb_o) in enumerate(((lw0_o, kd0_o, nb0_o), (lw1_o, kd1_o, nb1_o))):
        wl = w0_ref[d:d + 1, :] + jnp.dot(tw_ref[:, d * rw:(d + 1) * rw], w2_ref[d],
                                          preferred_element_type=F32)
        lw_o[...] = -math.exp(-0.5) * jax.nn.sigmoid(wl)
        al = a0_ref[d:d + 1, :] + jnp.dot(ax_ref[:, d * ra:(d + 1) * ra], a2_ref[d],
                                          preferred_element_type=F32)
        a = jax.nn.sigmoid(al)
        kd = k * (1.0 + (a - 1.0) * k_a)
        kd_o[...] = kd
        nb_o[...] = -(kk * a)
        kd_sum = kd if kd_sum is None else kd_sum + kd
    bonus_o[...] = _head_sum(r * kd_sum * r_k, seg_ref) * v


def _rwkv_features(r, k, v, tw, ax, p, vf, xv):
    rows, d = r.shape
    tr = _pick(rows, (256, 128, 64, 32, 16, 8))
    cb = _pick(d, (1024, 512, 256, 128))
    cw = min(cb, MXU)
    has_vf = vf is not None
    rw = p["w2"].shape[1]
    ra = p["a2"].shape[1]
    tile = pl.BlockSpec((tr, cb), lambda i, j: (i, j))

    def full_rows(a):
        return pl.BlockSpec((tr, a.shape[1]), lambda i, j: (i, 0))

    vec = jnp.stack([p["k_k"], p["k_a"], p["r_k"]]).astype(F32)
    in_specs = [tile, tile, tile, full_rows(tw), full_rows(ax),
                pl.BlockSpec((2, rw, cb), lambda i, j: (0, 0, j)),
                pl.BlockSpec((2, ra, cb), lambda i, j: (0, 0, j)),
                pl.BlockSpec((2, cb), lambda i, j: (0, j)),
                pl.BlockSpec((2, cb), lambda i, j: (0, j)),
                pl.BlockSpec((3, cb), lambda i, j: (0, j)),
                pl.BlockSpec((cw, cw), lambda i, j: (0, 0))]
    args = [r, k, v, tw, ax, p["w2"], p["a2"], p["w0"], p["a0"], vec, _seg_ones(cw)]
    n_out = 8
    if has_vf:
        in_specs += [tile, full_rows(xv),
                     pl.BlockSpec((p["v2"].shape[0], cb), lambda i, j: (0, j)),
                     pl.BlockSpec((1, cb), lambda i, j: (0, j))]
        args += [vf, xv, p["v2"], p["v0"].reshape(1, -1)]
        n_out = 9
    return pl.pallas_call(
        functools.partial(_feat_kernel, has_vf=has_vf, rw=rw, ra=ra),
        grid=(rows // tr, d // cb),
        in_specs=in_specs,
        out_specs=[tile] * n_out,
        out_shape=[jax.ShapeDtypeStruct((rows, d), F32)] * n_out,
        compiler_params=_cparams("parallel", "parallel"),
        name="rwkv_features",
    )(*args)


def _scan_kernel(*refs, reverse, has_prev):
    refs = list(refs)
    r_ref, lw_ref, kd_ref, kk_ref, nb_ref, v_ref = [refs.pop(0) for _ in range(6)]
    yp_ref = refs.pop(0) if has_prev else None
    y_ref, s_ref = refs
    tt = CHUNK
    gw = HPG * HEAD
    n_groups = y_ref.shape[1] // gw

    @pl.when(pl.program_id(1) == 0)
    def _():
        s_ref[...] = jnp.zeros_like(s_ref)

    row_t = lax.broadcasted_iota(jnp.int32, (tt, gw), 0)
    lane = lax.broadcasted_iota(jnp.int32, (tt, gw), 1)
    pos = lane % HEAD
    if reverse:
        strict, incl = pos > row_t, pos >= row_t
    else:
        strict, incl = pos < row_t, pos <= row_t
    eye = (pos == row_t).astype(F32)
    near = {}
    m = INV_BASE
    while m <= tt:
        near[m] = (pos // m) == (row_t // m)
        m *= 2
    ci = lax.broadcasted_iota(jnp.int32, (tt, tt), 0)
    cj = lax.broadcasted_iota(jnp.int32, (tt, tt), 1)
    csum = ((cj >= ci) if reverse else (cj <= ci)).astype(BF16)
    head_of_row = lax.broadcasted_iota(jnp.int32, (gw, gw), 0) // HEAD
    head_of_lane = lax.broadcasted_iota(jnp.int32, (gw, gw), 1) // HEAD
    same_head = head_of_row == head_of_lane
    last = 0 if reverse else tt - 1

    def mm(a, b):
        return jnp.dot(a.astype(BF16), b.astype(BF16), preferred_element_type=F32)

    def mm_nt(a, b):
        return lax.dot_general(a.astype(BF16), b.astype(BF16), (((1,), (1,)), ((), ())),
                               preferred_element_type=F32)

    def mm_tn(a, b):
        return lax.dot_general(a.astype(BF16), b.astype(BF16), (((0,), (0,)), ((), ())),
                               preferred_element_type=F32)

    def blockdiag(m):
        return jnp.where(same_head, jnp.concatenate([m.astype(BF16)] * HPG, axis=0), 0.0)

    def group(sl, s0):
        lw = lw_ref[:, sl]
        lw_hi = lw.astype(BF16)
        lw_mid = (lw - lw_hi.astype(F32)).astype(BF16)
        lw_lo = (lw - lw_hi.astype(F32) - lw_mid.astype(F32)).astype(BF16)
        c_in = (jnp.dot(csum, lw_hi, preferred_element_type=F32)
                + jnp.dot(csum, lw_mid, preferred_element_type=F32)
                + jnp.dot(csum, lw_lo, preferred_element_type=F32))
        c_tot = c_in[last:last + 1, :]
        g_inv = jnp.exp(-c_in)
        at = kk_ref[:, sl] * jnp.exp(c_in - lw)
        rt = r_ref[:, sl] * jnp.exp(c_in)
        nb = nb_ref[:, sl]
        kd = kd_ref[:, sl]
        v = v_ref[:, sl]
        lr = jnp.concatenate([at, rt], axis=0)
        yield
        ar = mm_nt(lr, s0)
        yield
        g_nb = mm_nt(lr, blockdiag(nb * g_inv))
        yield
        g_k = mm_nt(lr, blockdiag(kd * g_inv))
        yield
        n_mat = jnp.where(strict, g_nb[:tt], 0.0)
        m_ak = jnp.where(strict, g_k[:tt], 0.0)
        m_rb = jnp.where(incl, g_nb[tt:], 0.0)
        m_rk = jnp.where(incl, g_k[tt:], 0.0)
        mv = mm(jnp.concatenate([m_ak, m_rk], axis=0), blockdiag(v))
        x = ar[:tt] + mv[:tt]
        n_d = jnp.where(near[INV_BASE], n_mat, 0.0)
        yield
        p = mm(n_d, blockdiag(n_d))
        t_inv = eye + n_d
        yield
        t_inv = t_inv + mm(t_inv, blockdiag(p))
        m = INV_BASE
        while m < tt:
            n_off = jnp.where(jnp.logical_and(near[2 * m], jnp.logical_not(near[m])), n_mat, 0.0)
            t_bd = blockdiag(t_inv)
            yield
            e = mm(n_off, t_bd)
            yield
            t_inv = t_inv + mm(t_inv, blockdiag(e))
            m *= 2
        yield
        u = mm(t_inv, blockdiag(x))
        yield
        y = ar[tt:] + mm(m_rb, blockdiag(u)) + mv[tt:]
        if has_prev:
            y = y + yp_ref[:, sl]
        rem = jnp.exp(c_tot - c_in)
        upd = mm_tn(jnp.concatenate([u, v], axis=0), jnp.concatenate([nb * rem, kd * rem], axis=0))
        return y, s0 * jnp.exp(c_tot) + jnp.where(same_head, upd, 0.0)

    par = _pick(n_groups, (GROUPS_PER_TRIP, 4, 2, 1))

    def trip(i, carry):
        gs = [i * par + u for u in range(par)]
        sls = [pl.ds(pl.multiple_of(g * gw, gw), gw) for g in gs]
        chains = [group(sl, s_ref[g]) for g, sl in zip(gs, sls)]
        outs = [None] * par
        while any(o is None for o in outs):
            for u, chain in enumerate(chains):
                if outs[u] is None:
                    try:
                        next(chain)
                    except StopIteration as done:
                        outs[u] = done.value
        for g, sl, (y, s_new) in zip(gs, sls, outs):
            y_ref[:, sl] = y
            s_ref[g] = s_new
        return carry

    lax.fori_loop(0, n_groups // par, trip, 0)


def _rwkv_scan(r, lw, kd, kk, nb, v, y_prev, reverse, dims):
    rows, d = r.shape
    bsz, l, lc, n_lat = dims["B"], dims["L"], dims["Lc"], dims["n_lat"]
    tt = CHUNK
    n_ctx, n_latc = lc // tt, l // tt
    nc = n_ctx + n_latc

    def rblk(b, c):
        if reverse:
            ctx_blk = (n_lat + b * lc) // tt + (n_ctx - 1 - c)
            lat_blk = b * n_latc + (n_latc - 1 - (c - n_ctx))
        else:
            ctx_blk = (n_lat + b * lc) // tt + c
            lat_blk = b * n_latc + (c - n_ctx)
        return jnp.where(c < n_ctx, ctx_blk, lat_blk)

    blk = pl.BlockSpec((tt, d), lambda b, c: (rblk(b, c), 0))
    args = [r, lw, kd, kk, nb, v]
    if y_prev is not None:
        args.append(y_prev)
    gw = HPG * HEAD
    return pl.pallas_call(
        functools.partial(_scan_kernel, reverse=reverse, has_prev=y_prev is not None),
        grid=(bsz, nc),
        in_specs=[blk] * len(args),
        out_specs=blk,
        out_shape=jax.ShapeDtypeStruct((rows, d), F32),
        scratch_shapes=[pltpu.VMEM((d // gw, gw, gw), F32)],
        compiler_params=_cparams("parallel", "arbitrary"),
        name="rwkv_scan_rev" if reverse else "rwkv_scan_fwd",
    )(*args)


def _readout_kernel(ys_ref, bonus_ref, sg_ref, g2_ref, lnx_ref, seg_ref, o_ref):
    ys = ys_ref[...]
    inv = 1.0 / HEAD
    mu = _head_sum(ys, seg_ref) * inv
    dlt = ys - mu
    var = _head_sum(dlt * dlt, seg_ref) * inv
    y = dlt * lax.rsqrt(var + GN_EPS) * lnx_ref[0:1, :] + lnx_ref[1:2, :] + bonus_ref[...]
    g = jnp.dot(sg_ref[...], g2_ref[...], preferred_element_type=F32)
    o_ref[...] = (y * g).astype(o_ref.dtype)


def _rwkv_readout(ys, bonus, sg, g2, lnx_g, lnx_b):
    rows, d = ys.shape
    tr = _pick(rows, (256, 128, 64, 32, 16, 8))
    cb = _pick(d, (1024, 512, 256, 128))
    cw = min(cb, MXU)
    tile = pl.BlockSpec((tr, cb), lambda i, j: (i, j))
    lnx = jnp.stack([lnx_g, lnx_b]).astype(F32)
    return pl.pallas_call(
        _readout_kernel,
        grid=(rows // tr, d // cb),
        in_specs=[tile, tile,
                  pl.BlockSpec((tr, sg.shape[1]), lambda i, j: (i, 0)),
                  pl.BlockSpec((g2.shape[0], cb), lambda i, j: (0, j)),
                  pl.BlockSpec((2, cb), lambda i, j: (0, j)),
                  pl.BlockSpec((cw, cw), lambda i, j: (0, 0))],
        out_specs=tile,
        out_shape=jax.ShapeDtypeStruct((rows, d), BF16),
        compiler_params=_cparams("parallel", "parallel"),
        name="rwkv_readout",
    )(ys, bonus, sg, g2, lnx, _seg_ones(cw))


def _resid_ln_kernel(x_ref, y_ref, mod_ref, ln_ref, xo_ref, *rest, alpha, gate_row, mod_rows):
    gate = mod_ref[0, gate_row:gate_row + 1, :]
    z = alpha * x_ref[...] + gate * y_ref[...]
    mu = jnp.mean(z, axis=-1, keepdims=True)
    dz = z - mu
    var = jnp.mean(dz * dz, axis=-1, keepdims=True)
    zn = dz * lax.rsqrt(var + LN_EPS) * ln_ref[0:1, :] + ln_ref[1:2, :]
    xo_ref[...] = zn
    if mod_rows is not None:
        sh = mod_ref[0, mod_rows[0]:mod_rows[0] + 1, :]
        sc = mod_ref[0, mod_rows[1]:mod_rows[1] + 1, :]
        rest[0][...] = (zn * (1.0 + sc) + sh).astype(rest[0].dtype)


def _resid_ln(x, y, mod, ln_g, ln_b, alpha, gate_row, mod_rows, dims, rows=None):
    d = x.shape[1]
    rows = rows or x.shape[0]
    tr = _pick(dims["tr"], (256, 128, 64, 32, 16, 8))
    gmap = dims["gmap"]
    ratio = dims["tr"] // tr
    tile = pl.BlockSpec((tr, d), lambda i: (i, 0))
    ln = jnp.stack([ln_g, ln_b]).astype(F32)
    out_shape = [jax.ShapeDtypeStruct((rows, d), F32)]
    out_specs = [tile]
    if mod_rows is not None:
        out_shape.append(jax.ShapeDtypeStruct((rows, d), BF16))
        out_specs.append(tile)
    outs = pl.pallas_call(
        functools.partial(_resid_ln_kernel, alpha=alpha, gate_row=gate_row, mod_rows=mod_rows),
        grid=(rows // tr,),
        in_specs=[tile, tile,
                  pl.BlockSpec((1, 6, d), lambda i: (gmap(i // ratio), 0, 0)),
                  pl.BlockSpec((2, d), lambda i: (0, 0))],
        out_specs=out_specs,
        out_shape=out_shape,
        compiler_params=_cparams("parallel"),
        name="resid_ln",
    )(x, y, mod, ln)
    return outs


def _s5_kernel(*refs, tcs, nq, ns, finish):
    refs = list(refs)
    x_ref, mod_ref, bm_ref, cm_ref, lam_ref = [refs.pop(0) for _ in range(5)]
    if finish:
        yf_ref, d_ref = refs.pop(0), refs.pop(0)
    y_ref = refs.pop(0)
    hre_refs = [refs.pop(0) for _ in range(nq)]
    him_refs = [refs.pop(0) for _ in range(nq)]
    h2d_refs = [refs.pop(0) for _ in range(nq)]
    cr_ref, ci_ref = refs
    reverse = finish
    nsl = ns // LANE

    @pl.when(pl.program_id(2) == 0)
    def _():
        cr_ref[...] = jnp.zeros_like(cr_ref)
        ci_ref[...] = jnp.zeros_like(ci_ref)

    def u_of(q):
        sl = slice(q * S5_CH, (q + 1) * S5_CH)
        return x_ref[:, sl] * (1.0 + mod_ref[0, 1:2, sl]) + mod_ref[0, 0:1, sl]

    tb = 8
    ntrip = tcs // tb
    npiece = 1

    def stream(qs):
        proj_in, scan, proj_out = [], [], []
        state = {}

        def in_piece(q, pc):
            def run():
                cw = 2 * ns // npiece
                bu = jnp.dot(u_of(q).astype(BF16), bm_ref[q, :, pc * cw:(pc + 1) * cw],
                             preferred_element_type=F32)
                for j in range(cw // LANE):
                    col = pc * cw + j * LANE
                    ref, s = (hre_refs[q], col // LANE) if col < ns else (him_refs[q], (col - ns) // LANE)
                    ref[pl.ds(s, tcs, stride=nsl), :] = bu[:, j * LANE:(j + 1) * LANE]
            return run

        def scan_piece(gi):
            def run():
                if gi == 0:
                    for q in qs:
                        state[q] = (cr_ref[q], ci_ref[q], lam_ref[q, 0], lam_ref[q, 1])
                g = ntrip - 1 - gi if reverse else gi
                for j in range(tb):
                    off = (g * tb + (tb - 1 - j if reverse else j)) * nsl
                    for q in qs:
                        hr, hi, lr, li = state[q]
                        nr = lr * hr - li * hi + hre_refs[q][pl.ds(off, nsl), :]
                        ni = lr * hi + li * hr + him_refs[q][pl.ds(off, nsl), :]
                        hre_refs[q][pl.ds(off, nsl), :] = nr
                        him_refs[q][pl.ds(off, nsl), :] = ni
                        state[q] = (nr, ni, lr, li)
                if gi == ntrip - 1:
                    for q in qs:
                        cr_ref[q] = state[q][0]
                        ci_ref[q] = state[q][1]
            return run

        def out_piece(q, pc):
            def run():
                for s in range(pc * nsl // npiece, (pc + 1) * nsl // npiece):
                    h2d_refs[q][:, s * LANE:(s + 1) * LANE] = \
                        hre_refs[q][pl.ds(s, tcs, stride=nsl), :].astype(BF16)
                    h2d_refs[q][:, ns + s * LANE:ns + (s + 1) * LANE] = \
                        him_refs[q][pl.ds(s, tcs, stride=nsl), :].astype(BF16)
            return run

        def out_dot(q):
            def run():
                sl = slice(q * S5_CH, (q + 1) * S5_CH)
                y = jnp.dot(h2d_refs[q][...], cm_ref[q], preferred_element_type=F32)
                if finish:
                    y = y + yf_ref[:, sl] + d_ref[:, sl] * u_of(q)
                    c = math.sqrt(2.0 / math.pi)
                    y = 0.5 * y * (1.0 + jnp.tanh(c * (y + 0.044715 * (y * y * y))))
                y_ref[:, sl] = y.astype(y_ref.dtype)
            return run

        for q in qs:
            proj_in += [in_piece(q, pc) for pc in range(npiece)]
            proj_out += [out_piece(q, pc) for pc in range(npiece)] + [out_dot(q)]
        scan += [scan_piece(gi) for gi in range(ntrip)]
        return proj_in, scan, proj_out

    s_in, s_scan, s_out = stream(list(range(nq)))
    for piece in s_in + s_scan + s_out:
        piece()


def _s5_scan(x, mod, bm, cm, lam, y_fwd, dvec, dims):
    reverse = y_fwd is not None
    rows, d = x.shape
    tcs = dims["tcs"]
    ns2 = bm.shape[2]
    ns = ns2 // 2
    nq = _pick(d // S5_CH, (4, 2, 1))
    cbw = nq * S5_CH
    n_lat_blk = dims["L"] // tcs
    n_ctx_blk = dims["Lc"] // tcs
    n_lat_rows_blk = dims["n_lat"] // tcs
    nchunk = n_ctx_blk + n_lat_blk

    def rmap(b, c):
        if reverse:
            ctx_blk = n_lat_rows_blk + b * n_ctx_blk + (n_ctx_blk - 1 - c)
            lat_blk = b * n_lat_blk + (n_lat_blk - 1 - (c - n_ctx_blk))
        else:
            ctx_blk = n_lat_rows_blk + b * n_ctx_blk + c
            lat_blk = b * n_lat_blk + (c - n_ctx_blk)
        return jnp.where(c < n_ctx_blk, ctx_blk, lat_blk)

    bsz = dims["B"]
    tile = pl.BlockSpec((tcs, cbw), lambda b, g, c: (rmap(b, c), g))
    in_specs = [tile,
                pl.BlockSpec((1, 6, cbw), lambda b, g, c: (jnp.where(c < n_ctx_blk, bsz, b), 0, g)),
                pl.BlockSpec((nq, S5_CH, ns2), lambda b, g, c: (g, 0, 0)),
                pl.BlockSpec((nq, ns2, S5_CH), lambda b, g, c: (g, 0, 0)),
                pl.BlockSpec((nq, 2, ns // LANE, LANE), lambda b, g, c: (g, 0, 0, 0))]
    args = [x, mod, bm, cm, lam]
    if reverse:
        in_specs += [tile, pl.BlockSpec((1, cbw), lambda b, g, c: (0, g))]
        args += [y_fwd, dvec.reshape(1, -1).astype(F32)]
    return pl.pallas_call(
        functools.partial(_s5_kernel, tcs=tcs, nq=nq, ns=ns, finish=reverse),
        grid=(bsz, d // cbw, nchunk),
        in_specs=in_specs,
        out_specs=tile,
        out_shape=jax.ShapeDtypeStruct((rows, d), BF16 if reverse else F32),
        scratch_shapes=([pltpu.VMEM((tcs * ns // LANE, LANE), F32)] * (2 * nq)
                        + [pltpu.VMEM((tcs, ns2), BF16)] * nq
                        + [pltpu.VMEM((nq, ns // LANE, LANE), F32)] * 2),
        compiler_params=_cparams("parallel", "parallel", "arbitrary"),
        name="s5_scan_rev" if reverse else "s5_scan_fwd",
    )(*args)


def _s5_params(lam_re, lam_im, log_dt, b_re, b_im, c_re, c_im, gpb):
    lam = lax.complex(lam_re.astype(F32), lam_im.astype(F32))
    dt = jnp.exp(log_dt.astype(F32))[..., None]
    lam_bar = jnp.exp(lam * dt)
    b_bar = ((lam_bar - 1.0) / lam)[..., None] * lax.complex(b_re.astype(F32), b_im.astype(F32))
    c_mat = lax.complex(c_re.astype(F32), c_im.astype(F32))
    ndir, g, p = lam_bar.shape
    i_sz = b_bar.shape[-1]
    nblk = g // gpb
    eye = jnp.eye(gpb, dtype=F32)

    def blockdiag_in(m):
        m = m.reshape(ndir, nblk, gpb, p, i_sz)
        out = jnp.einsum("dngpi,gh->dngihp", m, eye)
        return out.reshape(ndir, nblk, gpb * i_sz, gpb * p)

    def blockdiag_out(m):
        m = m.reshape(ndir, nblk, gpb, i_sz, p)
        out = jnp.einsum("dngip,gh->dngphi", m, eye)
        return out.reshape(ndir, nblk, gpb * p, gpb * i_sz)

    bm = jnp.concatenate([blockdiag_in(jnp.real(b_bar)), blockdiag_in(jnp.imag(b_bar))], axis=-1).astype(BF16)
    cm = jnp.concatenate([blockdiag_out(jnp.real(c_mat)), blockdiag_out(-jnp.imag(c_mat))], axis=-2).astype(BF16)
    lam_t = jnp.stack([jnp.real(lam_bar), jnp.imag(lam_bar)], axis=1)
    lam_t = lam_t.reshape(ndir, 2, nblk, gpb * p // LANE, LANE).transpose(0, 2, 1, 3, 4).astype(F32)
    return bm, cm, lam_t


def kernel(x, c, ctx, c_ctx, ada_down, ada_up, ada_bias, ln_g, ln_b, rw_mix, rw_wr, rw_wk, rw_wv, rw_wo, rw_w0, rw_w1, rw_w2, rw_a0, rw_a1, rw_a2, rw_v0, rw_v1, rw_v2, rw_g1, rw_g2, rw_kk, rw_ka, rw_rk, rw_lnx_g, rw_lnx_b, s5_lam_re, s5_lam_im, s5_log_dt, s5_b_re, s5_b_im, s5_c_re, s5_c_im, s5_d, s5_glu_w, s5_glu_b, mlp_w1, mlp_w2):
    bsz, seq, d = x.shape
    lc = ctx.shape[1]
    depth = ada_down.shape[0]
    assert d % S5_CH == 0 and seq % CHUNK == 0 and lc % CHUNK == 0
    alpha = (2 * depth) ** 0.25
    n_lat = bsz * seq
    tr = _pick(math.gcd(seq, bsz * lc), (512, 256, 128, 64))
    tiles_per_batch = seq // tr

    def gmap(i):
        return jnp.minimum(i // tiles_per_batch, bsz)

    g_t = math.gcd(seq, lc)
    dims = {"B": bsz, "L": seq, "Lc": lc, "n_lat": n_lat, "tr": tr, "gmap": gmap,
            "tcs": _pick(g_t, (256, 128, 64, 32, 16, 8))}

    xs = jnp.concatenate([x.reshape(n_lat, d), ctx.reshape(bsz * lc, d)], axis=0).astype(F32)
    vf = None

    cvec = jnp.concatenate([c, c_ctx[None, :]], axis=0)
    cvec = _pad_to(jax.nn.silu(cvec), 0, 16).astype(BF16)
    wr_b, wk_b, wv_b, wo_b = (w.astype(BF16) for w in (rw_wr, rw_wk, rw_wv, rw_wo))
    glu_b, w1_b, w2_b = (w.astype(BF16) for w in (s5_glu_w, mlp_w1, mlp_w2))

    for i in range(depth):
        low = _mm(cvec, ada_down[i].astype(BF16), out_dtype=BF16, name="adaln_down")
        mod = _mm(low, ada_up[i].astype(BF16), bias=ada_bias[i], name="adaln_up")
        mod = mod[:bsz + 1].reshape(bsz + 1, 6, d)
        j = i // 2
        if i % 2 == 0:
            p = {"w2": _pad_to(rw_w2[j], 1, LANE).astype(BF16), "a2": _pad_to(rw_a2[j], 1, LANE).astype(BF16),
                 "w0": rw_w0[j], "a0": rw_a0[j], "k_k": rw_kk[j], "k_a": rw_ka[j],
                 "r_k": rw_rk[j].reshape(-1)}
            lerps = _rwkv_prep(xs, mod, rw_mix[j], dims)
            r = _mm(lerps[0], wr_b, layer=j, name="rwkv_r")
            k = _mm(lerps[2], wk_b, layer=j, name="rwkv_k")
            v = _mm(lerps[3], wv_b, layer=j, name="rwkv_v")
            w1cat = jnp.concatenate([_pad_to(rw_w1[j, dd], 1, LANE) for dd in range(2)], axis=1)
            a1cat = jnp.concatenate([_pad_to(rw_a1[j, dd], 1, LANE) for dd in range(2)], axis=1)
            tw = _mm(lerps[1], w1cat.astype(BF16), act="tanh", out_dtype=BF16, name="rwkv_w1")
            ax = _mm(lerps[4], a1cat.astype(BF16), out_dtype=BF16, name="rwkv_a1")
            sg = _mm(lerps[5], _pad_to(rw_g1[j], 1, LANE).astype(BF16), act="sigmoid", out_dtype=BF16,
                     name="rwkv_g1")
            g2 = _pad_to(rw_g2[j], 0, LANE).astype(BF16)
            xv = None
            if j > 0:
                xv = _mm(lerps[3], _pad_to(rw_v1[j - 1], 1, LANE).astype(BF16), out_dtype=BF16, name="rwkv_v1")
                p["v2"] = _pad_to(rw_v2[j - 1], 0, LANE).astype(BF16)
                p["v0"] = rw_v0[j - 1]
            feats = _rwkv_features(r, k, v, tw, ax, p, vf if j > 0 else None, xv)
            kk, lw0, lw1, kd0, kd1, nb0, nb1, bonus = feats[:8]
            if j > 0:
                v = feats[8]
            else:
                vf = v
            y = _rwkv_scan(r, lw0, kd0, kk, nb0, v, None, False, dims)
            y = _rwkv_scan(r, lw1, kd1, kk, nb1, v, y, True, dims)
            yg = _rwkv_readout(y, bonus, sg, g2, rw_lnx_g[j], rw_lnx_b[j])
            mix_out = _mm(yg, wo_b, layer=j, name="rwkv_o")
        else:
            gpb = S5_CH // S5_GROUP
            bm, cm, lam_t = _s5_params(s5_lam_re[j], s5_lam_im[j], s5_log_dt[j], s5_b_re[j], s5_b_im[j],
                                       s5_c_re[j], s5_c_im[j], gpb)
            yf = _s5_scan(xs, mod, bm[0], cm[0], lam_t[0], None, None, dims)
            gl = _s5_scan(xs, mod, bm[1], cm[1], lam_t[1], yf, s5_d[j], dims)
            mix_out = _mm(gl, glu_b, layer=j, bias=s5_glu_b[j], glu=True,
                          tm=_pick(gl.shape[0], (768, 512, 256, 128, 64, 32, 16, 8)), name="s5_glu")
        xs, h2 = _resid_ln(xs, mix_out, mod, ln_g[i, 0], ln_b[i, 0], alpha, 2, (3, 4), dims)
        a1 = _mm(h2, w1_b, layer=i, act="relu2", out_dtype=BF16, name="mlp_w1")
        mlp_out = _mm(a1, w2_b, layer=i, tm=_pick(a1.shape[0], (768, 512, 256, 128, 64, 32, 16, 8)),
                      tn=_pick(d, (1024, 512, 256, 128)), tk=_pick(a1.shape[1], (4096, 2048, 1024, 512, 256, 128)),
                      name="mlp_w2")
        (xs,) = _resid_ln(xs, mlp_out, mod, ln_g[i, 1], ln_b[i, 1], alpha, 5, None, dims,
                          rows=n_lat if i == depth - 1 else None)
    return xs.reshape(bsz, seq, d).astype(x.dtype)
```

```python
import functools
import math

import jax
import jax.numpy as jnp
from jax import lax
from jax.experimental import pallas as pl
from jax.experimental.pallas import tpu as pltpu

F32 = jnp.float32
BF16 = jnp.bfloat16

GRID_W = 64
HEAD = 64
S5_GROUP = 16
S5_CH = 256
LN_EPS = 1e-6
GN_EPS = 64e-5
LANE = 128
MXU = 256
HPG = MXU // HEAD
INV_BASE = 4
GROUPS_PER_TRIP = 16
CHUNK = 64
VMEM_LIMIT = 56 * 1024 * 1024


def _cparams(*sem):
    return pltpu.CompilerParams(dimension_semantics=sem, vmem_limit_bytes=VMEM_LIMIT)


def _pick(n, prefs):
    for p in prefs:
        if n % p == 0:
            return p
    return n


def _pad_to(a, axis, mult):
    n = a.shape[axis]
    r = (-n) % mult
    if r == 0:
        return a
    pad = [(0, 0)] * a.ndim
    pad[axis] = (0, r)
    return jnp.pad(a, pad)


def _mm_kernel(*refs, nk, act, has_bias, glu):
    refs = list(refs)
    x_ref = refs.pop(0)
    w_refs = [refs.pop(0) for _ in range(2 if glu else 1)]
    b_refs = [refs.pop(0) for _ in range((2 if glu else 1) if has_bias else 0)]
    o_ref = refs.pop(0)
    acc_refs = refs
    k = pl.program_id(2)

    def finish(zs):
        if has_bias:
            zs = [z + b[...] for z, b in zip(zs, b_refs)]
        if glu:
            z = zs[0] * jax.nn.sigmoid(zs[1])
        else:
            z = zs[0]
            if act == "tanh":
                z = jnp.tanh(z)
            elif act == "sigmoid":
                z = jax.nn.sigmoid(z)
            elif act == "relu2":
                z = jnp.square(jnp.maximum(z, 0.0))
        o_ref[...] = z.astype(o_ref.dtype)

    x = x_ref[...]
    parts = [jnp.dot(x, w[...], preferred_element_type=F32) for w in w_refs]
    if nk == 1:
        finish(parts)
        return

    @pl.when(k == 0)
    def _():
        for a in acc_refs:
            a[...] = jnp.zeros_like(a)

    for a, p in zip(acc_refs, parts):
        a[...] += p

    @pl.when(k == nk - 1)
    def _():
        finish([a[...] for a in acc_refs])


def _mm(x, w, bias=None, act=None, out_dtype=F32, glu=False, tm=None, tn=None, tk=None, layer=None,
        name="mm"):
    m, kdim = x.shape
    n = w.shape[-1] // (2 if glu else 1)
    tm = tm or _pick(m, (1536, 1024, 768, 512, 256, 128, 64, 32, 16, 8))
    tn = tn or _pick(n, (512, 256, 128))
    tk = tk or (kdim if kdim <= 4096 else _pick(kdim, (2048, 1024, 512, 256, 128)))
    nk = kdim // tk
    nj = n // tn

    def w_spec(off):
        if layer is None:
            return pl.BlockSpec((tk, tn), lambda i, j, k: (k, j + off))
        return pl.BlockSpec((None, tk, tn), lambda i, j, k: (layer, k, j + off))

    in_specs = [pl.BlockSpec((tm, tk), lambda i, j, k: (i, k)), w_spec(0)]
    args = [x, w]
    if glu:
        in_specs.append(w_spec(nj))
        args.append(w)
    if bias is not None:
        b2 = bias.reshape(1, -1).astype(F32)
        in_specs.append(pl.BlockSpec((1, tn), lambda i, j, k: (0, j)))
        args.append(b2)
        if glu:
            in_specs.append(pl.BlockSpec((1, tn), lambda i, j, k: (0, j + nj)))
            args.append(b2)
    scratch = [] if nk == 1 else [pltpu.VMEM((tm, tn), F32) for _ in range(2 if glu else 1)]
    return pl.pallas_call(
        functools.partial(_mm_kernel, nk=nk, act=act, has_bias=bias is not None, glu=glu),
        grid=(m // tm, nj, nk),
        in_specs=in_specs,
        out_specs=pl.BlockSpec((tm, tn), lambda i, j, k: (i, j)),
        out_shape=jax.ShapeDtypeStruct((m, n), out_dtype),
        scratch_shapes=scratch,
        compiler_params=_cparams("parallel", "parallel", "arbitrary"),
        name=name,
    )(*args)


def _head_sum(x, seg_ref):
    cw = seg_ref.shape[0]
    hi = x.astype(BF16)
    rest = x - hi.astype(F32)
    mid = rest.astype(BF16)
    lo = (rest - mid.astype(F32)).astype(BF16)
    outs = []
    for s in range(x.shape[1] // cw):
        sl = slice(s * cw, (s + 1) * cw)
        outs.append(jnp.dot(hi[:, sl], seg_ref[...], preferred_element_type=F32)
                    + jnp.dot(mid[:, sl], seg_ref[...], preferred_element_type=F32)
                    + jnp.dot(lo[:, sl], seg_ref[...], preferred_element_type=F32))
    return outs[0] if len(outs) == 1 else jnp.concatenate(outs, axis=1)


def _seg_ones(cw):
    idx = jnp.arange(cw) // HEAD
    return (idx[:, None] == idx[None, :]).astype(BF16)


def _prep_kernel(xp_ref, xc_ref, xn_ref, mod_ref, mix_ref, *rest, tr, n_lat, l_img, l_ctx):
    o_refs = rest[:6]
    hbuf, sbuf = rest[6], rest[7]
    i = pl.program_id(0)
    j = pl.program_id(1)
    sh = mod_ref[0, 0:1, :]
    sc = 1.0 + mod_ref[0, 1:2, :]
    hbuf[0:GRID_W, :] = xp_ref[...] * sc + sh
    hbuf[GRID_W:GRID_W + tr, :] = xc_ref[...] * sc + sh
    hbuf[GRID_W + tr:GRID_W + tr + GRID_W, :] = xn_ref[...] * sc + sh
    row = lax.broadcasted_iota(jnp.int32, (tr, 1), 0) + i * tr
    t_lat = row % l_img
    is_lat = i < n_lat // tr

    def shifted(s, keep):
        sbuf[...] = jnp.where(keep, hbuf[GRID_W + s:GRID_W + s + tr, :], 0.0)

    @pl.when(jnp.logical_and(is_lat, j == 0))
    def _():
        shifted(-1, t_lat % GRID_W != 0)

    @pl.when(jnp.logical_and(is_lat, j == 1))
    def _():
        shifted(1, t_lat % GRID_W != GRID_W - 1)

    @pl.when(jnp.logical_and(is_lat, j == 2))
    def _():
        shifted(-GRID_W, t_lat >= GRID_W)

    @pl.when(jnp.logical_and(is_lat, j == 3))
    def _():
        shifted(GRID_W, t_lat < l_img - GRID_W)

    t_ctx = (row - n_lat) % l_ctx

    @pl.when(jnp.logical_and(jnp.logical_not(is_lat), j < 2))
    def _():
        shifted(-1, t_ctx != 0)

    @pl.when(jnp.logical_and(jnp.logical_not(is_lat), j >= 2))
    def _():
        shifted(1, t_ctx != l_ctx - 1)

    h = hbuf[GRID_W:GRID_W + tr, :]
    xx = sbuf[...] - h
    for m in range(6):
        o_refs[m][...] = (h + xx * mix_ref[m:m + 1, :]).astype(BF16)


def _rwkv_prep(x, mod, mix, dims):
    r, d = x.shape
    tr = dims["tr"]
    dc = d // 4
    nb = tr // GRID_W
    last = r // GRID_W - 1
    kern = functools.partial(_prep_kernel, tr=tr, n_lat=dims["n_lat"], l_img=dims["L"], l_ctx=dims["Lc"])
    gmap = dims["gmap"]
    outs = pl.pallas_call(
        kern,
        grid=(r // tr, 4),
        in_specs=[
            pl.BlockSpec((GRID_W, dc), lambda i, j: (jnp.maximum(i * nb - 1, 0), j)),
            pl.BlockSpec((tr, dc), lambda i, j: (i, j)),
            pl.BlockSpec((GRID_W, dc), lambda i, j: (jnp.minimum(i * nb + nb, last), j)),
            pl.BlockSpec((1, 6, dc), lambda i, j: (gmap(i), 0, j)),
            pl.BlockSpec((6, dc), lambda i, j: (0, j)),
        ],
        out_specs=[pl.BlockSpec((tr, dc), lambda i, j: (i, j)) for _ in range(6)],
        out_shape=[jax.ShapeDtypeStruct((r, d), BF16) for _ in range(6)],
        scratch_shapes=[pltpu.VMEM((tr + 2 * GRID_W, dc), F32), pltpu.VMEM((tr, dc), F32)],
        compiler_params=_cparams("parallel", "parallel"),
        name="rwkv_prep",
    )(x, x, x, mod, mix)
    return outs


def _feat_kernel(*refs, has_vf, rw, ra):
    refs = list(refs)
    r_ref, k_ref, v_ref, tw_ref, ax_ref = [refs.pop(0) for _ in range(5)]
    w2_ref, a2_ref, w0_ref, a0_ref, vec_ref, seg_ref = [refs.pop(0) for _ in range(6)]
    if has_vf:
        vf_ref, xv_ref, v2_ref, v0_ref = [refs.pop(0) for _ in range(4)]
    kk_o, lw0_o, lw1_o, kd0_o, kd1_o, nb0_o, nb1_o, bonus_o = refs[:8]
    v_o = refs[8] if has_vf else None

    k = k_ref[...]
    r = r_ref[...]
    v = v_ref[...]
    k_k = vec_ref[0:1, :]
    k_a = vec_ref[1:2, :]
    r_k = vec_ref[2:3, :]
    kraw = k * k_k
    kk = kraw * lax.rsqrt(jnp.maximum(_head_sum(kraw * kraw, seg_ref), 1e-24))
    kk_o[...] = kk
    if has_vf:
        vl = v0_ref[...] + jnp.dot(xv_ref[...], v2_ref[...], preferred_element_type=F32)
        v = v + (vf_ref[...] - v) * jax.nn.sigmoid(vl)
        v_o[...] = v
    kd_sum = None
    for d, (lw_o, kd_o, nb_o) in enumerate(((lw0_o, kd0_o, nb0_o), (lw1_o, kd1_o, nb1_o))):
        wl = w0_ref[d:d + 1, :] + jnp.dot(tw_ref[:, d * rw:(d + 1) * rw], w2_ref[d],
                                          preferred_element_type=F32)
        lw_o[...] = -math.exp(-0.5) * jax.nn.sigmoid(wl)
        al = a0_ref[d:d + 1, :] + jnp.dot(ax_ref[:, d * ra:(d + 1) * ra], a2_ref[d],
                                          preferred_element_type=F32)
        a = jax.nn.sigmoid(al)
        kd = k * (1.0 + (a - 1.0) * k_a)
        kd_o[...] = kd
        nb_o[...] = -(kk * a)
        kd_sum = kd if kd_sum is None else kd_sum + kd
    bonus_o[...] = _head_sum(r * kd_sum * r_k, seg_ref) * v


def _rwkv_features(r, k, v, tw, ax, p, vf, xv):
    rows, d = r.shape
    tr = _pick(rows, (256, 128, 64, 32, 16, 8))
    cb = _pick(d, (1024, 512, 256, 128))
    cw = min(cb, MXU)
    has_vf = vf is not None
    rw = p["w2"].shape[1]
    ra = p["a2"].shape[1]
    tile = pl.BlockSpec((tr, cb), lambda i, j: (i, j))

    def full_rows(a):
        return pl.BlockSpec((tr, a.shape[1]), lambda i, j: (i, 0))

    vec = jnp.stack([p["k_k"], p["k_a"], p["r_k"]]).astype(F32)
    in_specs = [tile, tile, tile, full_rows(tw), full_rows(ax),
                pl.BlockSpec((2, rw, cb), lambda i, j: (0, 0, j)),
                pl.BlockSpec((2, ra, cb), lambda i, j: (0, 0, j)),
                pl.BlockSpec((2, cb), lambda i, j: (0, j)),
                pl.BlockSpec((2, cb), lambda i, j: (0, j)),
                pl.BlockSpec((3, cb), lambda i, j: (0, j)),
                pl.BlockSpec((cw, cw), lambda i, j: (0, 0))]
    args = [r, k, v, tw, ax, p["w2"], p["a2"], p["w0"], p["a0"], vec, _seg_ones(cw)]
    n_out = 8
    if has_vf:
        in_specs += [tile, full_rows(xv),
                     pl.BlockSpec((p["v2"].shape[0], cb), lambda i, j: (0, j)),
                     pl.BlockSpec((1, cb), lambda i, j: (0, j))]
        args += [vf, xv, p["v2"], p["v0"].reshape(1, -1)]
        n_out = 9
    return pl.pallas_call(
        functools.partial(_feat_kernel, has_vf=has_vf, rw=rw, ra=ra),
        grid=(rows // tr, d // cb),
        in_specs=in_specs,
        out_specs=[tile] * n_out,
        out_shape=[jax.ShapeDtypeStruct((rows, d), F32)] * n_out,
        compiler_params=_cparams("parallel", "parallel"),
        name="rwkv_features",
    )(*args)


def _scan_kernel(*refs, reverse, has_prev):
    refs = list(refs)
    r_ref, lw_ref, kd_ref, kk_ref, nb_ref, v_ref = [refs.pop(0) for _ in range(6)]
    yp_ref = refs.pop(0) if has_prev else None
    y_ref, s_ref = refs
    tt = CHUNK
    gw = HPG * HEAD
    n_groups = y_ref.shape[1] // gw

    @pl.when(pl.program_id(1) == 0)
    def _():
        s_ref[...] = jnp.zeros_like(s_ref)

    row_t = lax.broadcasted_iota(jnp.int32, (tt, gw), 0)
    lane = lax.broadcasted_iota(jnp.int32, (tt, gw), 1)
    pos = lane % HEAD
    if reverse:
        strict, incl = pos > row_t, pos >= row_t
    else:
        strict, incl = pos < row_t, pos <= row_t
    eye = (pos == row_t).astype(F32)
    near = {}
    m = INV_BASE
    while m <= tt:
        near[m] = (pos // m) == (row_t // m)
        m *= 2
    ci = lax.broadcasted_iota(jnp.int32, (tt, tt), 0)
    cj = lax.broadcasted_iota(jnp.int32, (tt, tt), 1)
    csum = ((cj >= ci) if reverse else (cj <= ci)).astype(BF16)
    head_of_row = lax.broadcasted_iota(jnp.int32, (gw, gw), 0) // HEAD
    head_of_lane = lax.broadcasted_iota(jnp.int32, (gw, gw), 1) // HEAD
    same_head = head_of_row == head_of_lane
    last = 0 if reverse else tt - 1

    def mm(a, b):
        return jnp.dot(a.astype(BF16), b.astype(BF16), preferred_element_type=F32)

    def mm_nt(a, b):
        return lax.dot_general(a.astype(BF16), b.astype(BF16), (((1,), (1,)), ((), ())),
                               preferred_element_type=F32)

    def mm_tn(a, b):
        return lax.dot_general(a.astype(BF16), b.astype(BF16), (((0,), (0,)), ((), ())),
                               preferred_element_type=F32)

    def blockdiag(m):
        return jnp.where(same_head, jnp.concatenate([m.astype(BF16)] * HPG, axis=0), 0.0)

    def group(sl, s0):
        lw = lw_ref[:, sl]
        lw_hi = lw.astype(BF16)
        lw_mid = (lw - lw_hi.astype(F32)).astype(BF16)
        lw_lo = (lw - lw_hi.astype(F32) - lw_mid.astype(F32)).astype(BF16)
        c_in = (jnp.dot(csum, lw_hi, preferred_element_type=F32)
                + jnp.dot(csum, lw_mid, preferred_element_type=F32)
                + jnp.dot(csum, lw_lo, preferred_element_type=F32))
        c_tot = c_in[last:last + 1, :]
        g_inv = jnp.exp(-c_in)
        at = kk_ref[:, sl] * jnp.exp(c_in - lw)
        rt = r_ref[:, sl] * jnp.exp(c_in)
        nb = nb_ref[:, sl]
        kd = kd_ref[:, sl]
        v = v_ref[:, sl]
        lr = jnp.concatenate([at, rt], axis=0)
        yield
        ar = mm_nt(lr, s0)
        yield
        g_nb = mm_nt(lr, blockdiag(nb * g_inv))
        yield
        g_k = mm_nt(lr, blockdiag(kd * g_inv))
        yield
        n_mat = jnp.where(strict, g_nb[:tt], 0.0)
        m_ak = jnp.where(strict, g_k[:tt], 0.0)
        m_rb = jnp.where(incl, g_nb[tt:], 0.0)
        m_rk = jnp.where(incl, g_k[tt:], 0.0)
        mv = mm(jnp.concatenate([m_ak, m_rk], axis=0), blockdiag(v))
        x = ar[:tt] + mv[:tt]
        n_d = jnp.where(near[INV_BASE], n_mat, 0.0)
        yield
        p = mm(n_d, blockdiag(n_d))
        t_inv = eye + n_d
        yield
        t_inv = t_inv + mm(t_inv, blockdiag(p))
        m = INV_BASE
        while m < tt:
            n_off = jnp.where(jnp.logical_and(near[2 * m], jnp.logical_not(near[m])), n_mat, 0.0)
            t_bd = blockdiag(t_inv)
            yield
            e = mm(n_off, t_bd)
            yield
            t_inv = t_inv + mm(t_inv, blockdiag(e))
            m *= 2
        yield
        u = mm(t_inv, blockdiag(x))
        yield
        y = ar[tt:] + mm(m_rb, blockdiag(u)) + mv[tt:]
        if has_prev:
            y = y + yp_ref[:, sl]
        rem = jnp.exp(c_tot - c_in)
        upd = mm_tn(jnp.concatenate([u, v], axis=0), jnp.concatenate([nb * rem, kd * rem], axis=0))
        return y, s0 * jnp.exp(c_tot) + jnp.where(same_head, upd, 0.0)

    par = _pick(n_groups, (GROUPS_PER_TRIP, 4, 2, 1))

    def trip(i, carry):
        gs = [i * par + u for u in range(par)]
        sls = [pl.ds(pl.multiple_of(g * gw, gw), gw) for g in gs]
        chains = [group(sl, s_ref[g]) for g, sl in zip(gs, sls)]
        outs = [None] * par
        while any(o is None for o in outs):
            for u, chain in enumerate(chains):
                if outs[u] is None:
                    try:
                        next(chain)
                    except StopIteration as done:
                        outs[u] = done.value
        for g, sl, (y, s_new) in zip(gs, sls, outs):
            y_ref[:, sl] = y
            s_ref[g] = s_new
        return carry

    lax.fori_loop(0, n_groups // par, trip, 0)


def _rwkv_scan(r, lw, kd, kk, nb, v, y_prev, reverse, dims):
    rows, d = r.shape
    bsz, l, lc, n_lat = dims["B"], dims["L"], dims["Lc"], dims["n_lat"]
    tt = CHUNK
    n_ctx, n_latc = lc // tt, l // tt
    nc = n_ctx + n_latc

    def rblk(b, c):
        if reverse:
            ctx_blk = (n_lat + b * lc) // tt + (n_ctx - 1 - c)
            lat_blk = b * n_latc + (n_latc - 1 - (c - n_ctx))
        else:
            ctx_blk = (n_lat + b * lc) // tt + c
            lat_blk = b * n_latc + (c - n_ctx)
        return jnp.where(c < n_ctx, ctx_blk, lat_blk)

    blk = pl.BlockSpec((tt, d), lambda b, c: (rblk(b, c), 0))
    args = [r, lw, kd, kk, nb, v]
    if y_prev is not None:
        args.append(y_prev)
    gw = HPG * HEAD
    return pl.pallas_call(
        functools.partial(_scan_kernel, reverse=reverse, has_prev=y_prev is not None),
        grid=(bsz, nc),
        in_specs=[blk] * len(args),
        out_specs=blk,
        out_shape=jax.ShapeDtypeStruct((rows, d), F32),
        scratch_shapes=[pltpu.VMEM((d // gw, gw, gw), F32)],
        compiler_params=_cparams("parallel", "arbitrary"),
        name="rwkv_scan_rev" if reverse else "rwkv_scan_fwd",
    )(*args)


def _readout_kernel(ys_ref, bonus_ref, sg_ref, g2_ref, lnx_ref, seg_ref, o_ref):
    ys = ys_ref[...]
    inv = 1.0 / HEAD
    mu = _head_sum(ys, seg_ref) * inv
    dlt = ys - mu
    var = _head_sum(dlt * dlt, seg_ref) * inv
    y = dlt * lax.rsqrt(var + GN_EPS) * lnx_ref[0:1, :] + lnx_ref[1:2, :] + bonus_ref[...]
    g = jnp.dot(sg_ref[...], g2_ref[...], preferred_element_type=F32)
    o_ref[...] = (y * g).astype(o_ref.dtype)


def _rwkv_readout(ys, bonus, sg, g2, lnx_g, lnx_b):
    rows, d = ys.shape
    tr = _pick(rows, (256, 128, 64, 32, 16, 8))
    cb = _pick(d, (1024, 512, 256, 128))
    cw = min(cb, MXU)
    tile = pl.BlockSpec((tr, cb), lambda i, j: (i, j))
    lnx = jnp.stack([lnx_g, lnx_b]).astype(F32)
    return pl.pallas_call(
        _readout_kernel,
        grid=(rows // tr, d // cb),
        in_specs=[tile, tile,
                  pl.BlockSpec((tr, sg.shape[1]), lambda i, j: (i, 0)),
                  pl.BlockSpec((g2.shape[0], cb), lambda i, j: (0, j)),
                  pl.BlockSpec((2, cb), lambda i, j: (0, j)),
                  pl.BlockSpec((cw, cw), lambda i, j: (0, 0))],
        out_specs=tile,
        out_shape=jax.ShapeDtypeStruct((rows, d), BF16),
        compiler_params=_cparams("parallel", "parallel"),
        name="rwkv_readout",
    )(ys, bonus, sg, g2, lnx, _seg_ones(cw))


def _resid_ln_kernel(x_ref, y_ref, mod_ref, ln_ref, xo_ref, *rest, alpha, gate_row, mod_rows):
    gate = mod_ref[0, gate_row:gate_row + 1, :]
    z = alpha * x_ref[...] + gate * y_ref[...]
    mu = jnp.mean(z, axis=-1, keepdims=True)
    dz = z - mu
    var = jnp.mean(dz * dz, axis=-1, keepdims=True)
    zn = dz * lax.rsqrt(var + LN_EPS) * ln_ref[0:1, :] + ln_ref[1:2, :]
    xo_ref[...] = zn
    if mod_rows is not None:
        sh = mod_ref[0, mod_rows[0]:mod_rows[0] + 1, :]
        sc = mod_ref[0, mod_rows[1]:mod_rows[1] + 1, :]
        rest[0][...] = (zn * (1.0 + sc) + sh).astype(rest[0].dtype)


def _resid_ln(x, y, mod, ln_g, ln_b, alpha, gate_row, mod_rows, dims, rows=None):
    d = x.shape[1]
    rows = rows or x.shape[0]
    tr = _pick(dims["tr"], (256, 128, 64, 32, 16, 8))
    gmap = dims["gmap"]
    ratio = dims["tr"] // tr
    tile = pl.BlockSpec((tr, d), lambda i: (i, 0))
    ln = jnp.stack([ln_g, ln_b]).astype(F32)
    out_shape = [jax.ShapeDtypeStruct((rows, d), F32)]
    out_specs = [tile]
    if mod_rows is not None:
        out_shape.append(jax.ShapeDtypeStruct((rows, d), BF16))
        out_specs.append(tile)
    outs = pl.pallas_call(
        functools.partial(_resid_ln_kernel, alpha=alpha, gate_row=gate_row, mod_rows=mod_rows),
        grid=(rows // tr,),
        in_specs=[tile, tile,
                  pl.BlockSpec((1, 6, d), lambda i: (gmap(i // ratio), 0, 0)),
                  pl.BlockSpec((2, d), lambda i: (0, 0))],
        out_specs=out_specs,
        out_shape=out_shape,
        compiler_params=_cparams("parallel"),
        name="resid_ln",
    )(x, y, mod, ln)
    return outs


def _s5_kernel(*refs, tcs, nq, ns, finish):
    refs = list(refs)
    x_ref, mod_ref, bm_ref, cm_ref, lam_ref = [refs.pop(0) for _ in range(5)]
    if finish:
        yf_ref, d_ref = refs.pop(0), refs.pop(0)
    y_ref = refs.pop(0)
    hre_refs = [refs.pop(0) for _ in range(nq)]
    him_refs = [refs.pop(0) for _ in range(nq)]
    h2d_refs = [refs.pop(0) for _ in range(nq)]
    cr_ref, ci_ref = refs
    reverse = finish
    nsl = ns // LANE

    @pl.when(pl.program_id(2) == 0)
    def _():
        cr_ref[...] = jnp.zeros_like(cr_ref)
        ci_ref[...] = jnp.zeros_like(ci_ref)

    def u_of(q):
        sl = slice(q * S5_CH, (q + 1) * S5_CH)
        return x_ref[:, sl] * (1.0 + mod_ref[0, 1:2, sl]) + mod_ref[0, 0:1, sl]

    for q in range(nq):
        bu = jnp.dot(u_of(q).astype(BF16), bm_ref[q], preferred_element_type=F32)
        for s in range(nsl):
            hre_refs[q][pl.ds(s, tcs, stride=nsl), :] = bu[:, s * LANE:(s + 1) * LANE]
            him_refs[q][pl.ds(s, tcs, stride=nsl), :] = bu[:, ns + s * LANE:ns + (s + 1) * LANE]

    h = [(cr_ref[q], ci_ref[q]) for q in range(nq)]
    lam = [(lam_ref[q, 0], lam_ref[q, 1]) for q in range(nq)]
    for step in range(tcs):
        off = (tcs - 1 - step if reverse else step) * nsl
        for q in range(nq):
            (hr, hi), (lr, li) = h[q], lam[q]
            nr = lr * hr - li * hi + hre_refs[q][pl.ds(off, nsl), :]
            ni = lr * hi + li * hr + him_refs[q][pl.ds(off, nsl), :]
            hre_refs[q][pl.ds(off, nsl), :] = nr
            him_refs[q][pl.ds(off, nsl), :] = ni
            h[q] = (nr, ni)
    for q in range(nq):
        cr_ref[q], ci_ref[q] = h[q]

    for q in range(nq):
        for s in range(nsl):
            h2d_refs[q][:, s * LANE:(s + 1) * LANE] = hre_refs[q][pl.ds(s, tcs, stride=nsl), :].astype(BF16)
            h2d_refs[q][:, ns + s * LANE:ns + (s + 1) * LANE] = \
                him_refs[q][pl.ds(s, tcs, stride=nsl), :].astype(BF16)
        sl = slice(q * S5_CH, (q + 1) * S5_CH)
        y = jnp.dot(h2d_refs[q][...], cm_ref[q], preferred_element_type=F32)
        if finish:
            y = y + yf_ref[:, sl] + d_ref[:, sl] * u_of(q)
            c = math.sqrt(2.0 / math.pi)
            y = 0.5 * y * (1.0 + jnp.tanh(c * (y + 0.044715 * (y * y * y))))
        y_ref[:, sl] = y.astype(y_ref.dtype)


def _s5_scan(x, mod, bm, cm, lam, y_fwd, dvec, dims):
    reverse = y_fwd is not None
    rows, d = x.shape
    tcs = dims["tcs"]
    ns2 = bm.shape[2]
    ns = ns2 // 2
    nq = _pick(d // S5_CH, (4, 2, 1))
    cbw = nq * S5_CH
    n_lat_blk = dims["L"] // tcs
    n_ctx_blk = dims["Lc"] // tcs
    n_lat_rows_blk = dims["n_lat"] // tcs
    nchunk = n_ctx_blk + n_lat_blk

    def rmap(b, c):
        if reverse:
            ctx_blk = n_lat_rows_blk + b * n_ctx_blk + (n_ctx_blk - 1 - c)
            lat_blk = b * n_lat_blk + (n_lat_blk - 1 - (c - n_ctx_blk))
        else:
            ctx_blk = n_lat_rows_blk + b * n_ctx_blk + c
            lat_blk = b * n_lat_blk + (c - n_ctx_blk)
        return jnp.where(c < n_ctx_blk, ctx_blk, lat_blk)

    bsz = dims["B"]
    tile = pl.BlockSpec((tcs, cbw), lambda b, g, c: (rmap(b, c), g))
    in_specs = [tile,
                pl.BlockSpec((1, 6, cbw), lambda b, g, c: (jnp.where(c < n_ctx_blk, bsz, b), 0, g)),
                pl.BlockSpec((nq, S5_CH, ns2), lambda b, g, c: (g, 0, 0)),
                pl.BlockSpec((nq, ns2, S5_CH), lambda b, g, c: (g, 0, 0)),
                pl.BlockSpec((nq, 2, ns // LANE, LANE), lambda b, g, c: (g, 0, 0, 0))]
    args = [x, mod, bm, cm, lam]
    if reverse:
        in_specs += [tile, pl.BlockSpec((1, cbw), lambda b, g, c: (0, g))]
        args += [y_fwd, dvec.reshape(1, -1).astype(F32)]
    return pl.pallas_call(
        functools.partial(_s5_kernel, tcs=tcs, nq=nq, ns=ns, finish=reverse),
        grid=(bsz, d // cbw, nchunk),
        in_specs=in_specs,
        out_specs=tile,
        out_shape=jax.ShapeDtypeStruct((rows, d), BF16 if reverse else F32),
        scratch_shapes=([pltpu.VMEM((tcs * ns // LANE, LANE), F32)] * (2 * nq)
                        + [pltpu.VMEM((tcs, ns2), BF16)] * nq
                        + [pltpu.VMEM((nq, ns // LANE, LANE), F32)] * 2),
        compiler_params=_cparams("parallel", "parallel", "arbitrary"),
        name="s5_scan_rev" if reverse else "s5_scan_fwd",
    )(*args)


def _s5_params(lam_re, lam_im, log_dt, b_re, b_im, c_re, c_im, gpb):
    lam = lax.complex(lam_re.astype(F32), lam_im.astype(F32))
    dt = jnp.exp(log_dt.astype(F32))[..., None]
    lam_bar = jnp.exp(lam * dt)
    b_bar = ((lam_bar - 1.0) / lam)[..., None] * lax.complex(b_re.astype(F32), b_im.astype(F32))
    c_mat = lax.complex(c_re.astype(F32), c_im.astype(F32))
    ndir, g, p = lam_bar.shape
    i_sz = b_bar.shape[-1]
    nblk = g // gpb
    eye = jnp.eye(gpb, dtype=F32)

    def blockdiag_in(m):
        m = m.reshape(ndir, nblk, gpb, p, i_sz)
        out = jnp.einsum("dngpi,gh->dngihp", m, eye)
        return out.reshape(ndir, nblk, gpb * i_sz, gpb * p)

    def blockdiag_out(m):
        m = m.reshape(ndir, nblk, gpb, i_sz, p)
        out = jnp.einsum("dngip,gh->dngphi", m, eye)
        return out.reshape(ndir, nblk, gpb * p, gpb * i_sz)

    bm = jnp.concatenate([blockdiag_in(jnp.real(b_bar)), blockdiag_in(jnp.imag(b_bar))], axis=-1).astype(BF16)
    cm = jnp.concatenate([blockdiag_out(jnp.real(c_mat)), blockdiag_out(-jnp.imag(c_mat))], axis=-2).astype(BF16)
    lam_t = jnp.stack([jnp.real(lam_bar), jnp.imag(lam_bar)], axis=1)
    lam_t = lam_t.reshape(ndir, 2, nblk, gpb * p // LANE, LANE).transpose(0, 2, 1, 3, 4).astype(F32)
    return bm, cm, lam_t


def kernel(x, c, ctx, c_ctx, ada_down, ada_up, ada_bias, ln_g, ln_b, rw_mix, rw_wr, rw_wk, rw_wv, rw_wo, rw_w0, rw_w1, rw_w2, rw_a0, rw_a1, rw_a2, rw_v0, rw_v1, rw_v2, rw_g1, rw_g2, rw_kk, rw_ka, rw_rk, rw_lnx_g, rw_lnx_b, s5_lam_re, s5_lam_im, s5_log_dt, s5_b_re, s5_b_im, s5_c_re, s5_c_im, s5_d, s5_glu_w, s5_glu_b, mlp_w1, mlp_w2):
    bsz, seq, d = x.shape
    lc = ctx.shape[1]
    depth = ada_down.shape[0]
    assert d % S5_CH == 0 and seq % CHUNK == 0 and lc % CHUNK == 0
    alpha = (2 * depth) ** 0.25
    n_lat = bsz * seq
    tr = _pick(math.gcd(seq, bsz * lc), (512, 256, 128, 64))
    tiles_per_batch = seq // tr

    def gmap(i):
        return jnp.minimum(i // tiles_per_batch, bsz)

    g_t = math.gcd(seq, lc)
    dims = {"B": bsz, "L": seq, "Lc": lc, "n_lat": n_lat, "tr": tr, "gmap": gmap,
            "tcs": _pick(g_t, (256, 128, 64, 32, 16, 8))}

    xs = jnp.concatenate([x.reshape(n_lat, d), ctx.reshape(bsz * lc, d)], axis=0).astype(F32)
    vf = None

    cvec = jnp.concatenate([c, c_ctx[None, :]], axis=0)
    cvec = _pad_to(jax.nn.silu(cvec), 0, 16).astype(BF16)
    wr_b, wk_b, wv_b, wo_b = (w.astype(BF16) for w in (rw_wr, rw_wk, rw_wv, rw_wo))
    glu_b, w1_b, w2_b = (w.astype(BF16) for w in (s5_glu_w, mlp_w1, mlp_w2))

    for i in range(depth):
        low = _mm(cvec, ada_down[i].astype(BF16), out_dtype=BF16, name="adaln_down")
        mod = _mm(low, ada_up[i].astype(BF16), bias=ada_bias[i], name="adaln_up")
        mod = mod[:bsz + 1].reshape(bsz + 1, 6, d)
        j = i // 2
        if i % 2 == 0:
            p = {"w2": _pad_to(rw_w2[j], 1, LANE).astype(BF16), "a2": _pad_to(rw_a2[j], 1, LANE).astype(BF16),
                 "w0": rw_w0[j], "a0": rw_a0[j], "k_k": rw_kk[j], "k_a": rw_ka[j],
                 "r_k": rw_rk[j].reshape(-1)}
            lerps = _rwkv_prep(xs, mod, rw_mix[j], dims)
            r = _mm(lerps[0], wr_b, layer=j, name="rwkv_r")
            k = _mm(lerps[2], wk_b, layer=j, name="rwkv_k")
            v = _mm(lerps[3], wv_b, layer=j, name="rwkv_v")
            w1cat = jnp.concatenate([_pad_to(rw_w1[j, dd], 1, LANE) for dd in range(2)], axis=1)
            a1cat = jnp.concatenate([_pad_to(rw_a1[j, dd], 1, LANE) for dd in range(2)], axis=1)
            tw = _mm(lerps[1], w1cat.astype(BF16), act="tanh", out_dtype=BF16, name="rwkv_w1")
            ax = _mm(lerps[4], a1cat.astype(BF16), out_dtype=BF16, name="rwkv_a1")
            sg = _mm(lerps[5], _pad_to(rw_g1[j], 1, LANE).astype(BF16), act="sigmoid", out_dtype=BF16,
                     name="rwkv_g1")
            g2 = _pad_to(rw_g2[j], 0, LANE).astype(BF16)
            xv = None
            if j > 0:
                xv = _mm(lerps[3], _pad_to(rw_v1[j - 1], 1, LANE).astype(BF16), out_dtype=BF16, name="rwkv_v1")
                p["v2"] = _pad_to(rw_v2[j - 1], 0, LANE).astype(BF16)
                p["v0"] = rw_v0[j - 1]
            feats = _rwkv_features(r, k, v, tw, ax, p, vf if j > 0 else None, xv)
            kk, lw0, lw1, kd0, kd1, nb0, nb1, bonus = feats[:8]
            if j > 0:
                v = feats[8]
            else:
                vf = v
            y = _rwkv_scan(r, lw0, kd0, kk, nb0, v, None, False, dims)
            y = _rwkv_scan(r, lw1, kd1, kk, nb1, v, y, True, dims)
            yg = _rwkv_readout(y, bonus, sg, g2, rw_lnx_g[j], rw_lnx_b[j])
            mix_out = _mm(yg, wo_b, layer=j, name="rwkv_o")
        else:
            gpb = S5_CH // S5_GROUP
            bm, cm, lam_t = _s5_params(s5_lam_re[j], s5_lam_im[j], s5_log_dt[j], s5_b_re[j], s5_b_im[j],
                                       s5_c_re[j], s5_c_im[j], gpb)
            yf = _s5_scan(xs, mod, bm[0], cm[0], lam_t[0], None, None, dims)
            gl = _s5_scan(xs, mod, bm[1], cm[1], lam_t[1], yf, s5_d[j], dims)
            mix_out = _mm(gl, glu_b, layer=j, bias=s5_glu_b[j], glu=True,
                          tm=_pick(gl.shape[0], (768, 512, 256, 128, 64, 32, 16, 8)), name="s5_glu")
        xs, h2 = _resid_ln(xs, mix_out, mod, ln_g[i, 0], ln_b[i, 0], alpha, 2, (3, 4), dims)
        a1 = _mm(h2, w1_b, layer=i, act="relu2", out_dtype=BF16, name="mlp_w1")
        mlp_out = _mm(a1, w2_b, layer=i, tm=_pick(a1.shape[0], (768, 512, 256, 128, 64, 32, 16, 8)),
                      tn=_pick(d, (1024, 512, 256, 128)), tk=_pick(a1.shape[1], (4096, 2048, 1024, 512, 256, 128)),
                      name="mlp_w2")
        (xs,) = _resid_ln(xs, mlp_out, mod, ln_g[i, 1], ln_b[i, 1], alpha, 5, None, dims,
                          rows=n_lat if i == depth - 1 else None)
    return xs.reshape(bsz, seq, d).astype(x.dtype)
```

```python
import functools
import math

import jax
import jax.numpy as jnp
from jax import lax
from jax.experimental import pallas as pl
from jax.experimental.pallas import tpu as pltpu

F32 = jnp.float32
BF16 = jnp.bfloat16

GRID_W = 64
HEAD = 64
S5_GROUP = 16
S5_CH = 256
LN_EPS = 1e-6
GN_EPS = 64e-5
LANE = 128
MXU = 256
HPG = MXU // HEAD
INV_BASE = 4
GROUPS_PER_TRIP = 16
CHUNK = 64
VMEM_LIMIT = 56 * 1024 * 1024


def _cparams(*sem):
    return pltpu.CompilerParams(dimension_semantics=sem, vmem_limit_bytes=VMEM_LIMIT)


def _pick(n, prefs):
    for p in prefs:
        if n % p == 0:
            return p
    return n


def _pad_to(a, axis, mult):
    n = a.shape[axis]
    r = (-n) % mult
    if r == 0:
        return a
    pad = [(0, 0)] * a.ndim
    pad[axis] = (0, r)
    return jnp.pad(a, pad)


def _mm_kernel(*refs, nk, act, has_bias, glu, has_cast):
    refs = list(refs)
    x_ref = refs.pop(0)
    w_refs = [refs.pop(0) for _ in range(2 if glu else 1)]
    b_refs = [refs.pop(0) for _ in range((2 if glu else 1) if has_bias else 0)]
    cast_in = refs.pop(0) if has_cast else None
    o_ref = refs.pop(0)
    if has_cast:
        refs.pop(0)[...] = cast_in[...].astype(BF16)
    acc_refs = refs
    k = pl.program_id(2)

    def finish(zs):
        if has_bias:
            zs = [z + b[...] for z, b in zip(zs, b_refs)]
        if glu:
            z = zs[0] * jax.nn.sigmoid(zs[1])
        else:
            z = zs[0]
            if act == "tanh":
                z = jnp.tanh(z)
            elif act == "sigmoid":
                z = jax.nn.sigmoid(z)
            elif act == "relu2":
                z = jnp.square(jnp.maximum(z, 0.0))
        o_ref[...] = z.astype(o_ref.dtype)

    x = x_ref[...]
    parts = [jnp.dot(x, w[...], preferred_element_type=F32) for w in w_refs]
    if nk == 1:
        finish(parts)
        return

    @pl.when(k == 0)
    def _():
        for a in acc_refs:
            a[...] = jnp.zeros_like(a)

    for a, p in zip(acc_refs, parts):
        a[...] += p

    @pl.when(k == nk - 1)
    def _():
        finish([a[...] for a in acc_refs])


def _mm(x, w, bias=None, act=None, out_dtype=F32, glu=False, tm=None, tn=None, tk=None, layer=None,
        cast_next=None, name="mm"):
    m, kdim = x.shape
    n = w.shape[-1] // (2 if glu else 1)
    tm = tm or _pick(m, (1536, 1024, 768, 512, 256, 128, 64, 32, 16, 8))
    tn = tn or _pick(n, (512, 256, 128))
    tk = tk or (kdim if kdim <= 4096 else _pick(kdim, (2048, 1024, 512, 256, 128)))
    nk = kdim // tk
    nj = n // tn

    def w_spec(off):
        if layer is None:
            return pl.BlockSpec((tk, tn), lambda i, j, k: (k, j + off))
        return pl.BlockSpec((None, tk, tn), lambda i, j, k: (layer, k, j + off))

    in_specs = [pl.BlockSpec((tm, tk), lambda i, j, k: (i, k)), w_spec(0)]
    args = [x, w]
    if glu:
        in_specs.append(w_spec(nj))
        args.append(w)
    if bias is not None:
        b2 = bias.reshape(1, -1).astype(F32)
        in_specs.append(pl.BlockSpec((1, tn), lambda i, j, k: (0, j)))
        args.append(b2)
        if glu:
            in_specs.append(pl.BlockSpec((1, tn), lambda i, j, k: (0, j + nj)))
            args.append(b2)
    scratch = [] if nk == 1 else [pltpu.VMEM((tm, tn), F32) for _ in range(2 if glu else 1)]
    out_specs = [pl.BlockSpec((tm, tn), lambda i, j, k: (i, j))]
    out_shape = [jax.ShapeDtypeStruct((m, n), out_dtype)]
    if cast_next is not None:
        src, src_layer = cast_next
        rows2, cols2 = src.shape[1:]
        steps = (m // tm) * nj * nk
        pc = _pick(cols2, (256, 128))
        pr = next(r for r in (rows2 // f for f in (16, 8, 4, 2, 1)) if (rows2 // r) * (cols2 // pc) <= steps)
        ncol = cols2 // pc
        last_piece = (rows2 // pr) * ncol - 1

        def piece(i, j, k):
            return jnp.minimum((i * nj + j) * nk + k, last_piece)

        in_specs.append(pl.BlockSpec((None, pr, pc), lambda i, j, k: (src_layer, piece(i, j, k) // ncol,
                                                                     piece(i, j, k) % ncol)))
        args.append(src)
        out_specs.append(pl.BlockSpec((pr, pc), lambda i, j, k: (piece(i, j, k) // ncol, piece(i, j, k) % ncol)))
        out_shape.append(jax.ShapeDtypeStruct((rows2, cols2), BF16))
    outs = pl.pallas_call(
        functools.partial(_mm_kernel, nk=nk, act=act, has_bias=bias is not None, glu=glu,
                          has_cast=cast_next is not None),
        grid=(m // tm, nj, nk),
        in_specs=in_specs,
        out_specs=out_specs,
        out_shape=out_shape,
        scratch_shapes=scratch,
        compiler_params=(_cparams("parallel", "parallel", "arbitrary") if cast_next is None
                         else _cparams("arbitrary", "arbitrary", "arbitrary")),
        name=name,
    )(*args)
    return outs[0] if cast_next is None else outs


def _head_sum(x, seg_ref):
    cw = seg_ref.shape[0]
    hi = x.astype(BF16)
    rest = x - hi.astype(F32)
    mid = rest.astype(BF16)
    lo = (rest - mid.astype(F32)).astype(BF16)
    outs = []
    for s in range(x.shape[1] // cw):
        sl = slice(s * cw, (s + 1) * cw)
        outs.append(jnp.dot(hi[:, sl], seg_ref[...], preferred_element_type=F32)
                    + jnp.dot(mid[:, sl], seg_ref[...], preferred_element_type=F32)
                    + jnp.dot(lo[:, sl], seg_ref[...], preferred_element_type=F32))
    return outs[0] if len(outs) == 1 else jnp.concatenate(outs, axis=1)


def _seg_ones(cw):
    idx = jnp.arange(cw) // HEAD
    return (idx[:, None] == idx[None, :]).astype(BF16)


def _prep_kernel(xp_ref, xc_ref, xn_ref, mod_ref, mix_ref, *rest, tr, n_lat, l_img, l_ctx):
    o_refs = rest[:6]
    hbuf, sbuf = rest[6], rest[7]
    i = pl.program_id(0)
    j = pl.program_id(1)
    sh = mod_ref[0, 0:1, :]
    sc = 1.0 + mod_ref[0, 1:2, :]
    hbuf[0:GRID_W, :] = xp_ref[...] * sc + sh
    hbuf[GRID_W:GRID_W + tr, :] = xc_ref[...] * sc + sh
    hbuf[GRID_W + tr:GRID_W + tr + GRID_W, :] = xn_ref[...] * sc + sh
    row = lax.broadcasted_iota(jnp.int32, (tr, 1), 0) + i * tr
    t_lat = row % l_img
    is_lat = i < n_lat // tr

    def shifted(s, keep):
        sbuf[...] = jnp.where(keep, hbuf[GRID_W + s:GRID_W + s + tr, :], 0.0)

    @pl.when(jnp.logical_and(is_lat, j == 0))
    def _():
        shifted(-1, t_lat % GRID_W != 0)

    @pl.when(jnp.logical_and(is_lat, j == 1))
    def _():
        shifted(1, t_lat % GRID_W != GRID_W - 1)

    @pl.when(jnp.logical_and(is_lat, j == 2))
    def _():
        shifted(-GRID_W, t_lat >= GRID_W)

    @pl.when(jnp.logical_and(is_lat, j == 3))
    def _():
        shifted(GRID_W, t_lat < l_img - GRID_W)

    t_ctx = (row - n_lat) % l_ctx

    @pl.when(jnp.logical_and(jnp.logical_not(is_lat), j < 2))
    def _():
        shifted(-1, t_ctx != 0)

    @pl.when(jnp.logical_and(jnp.logical_not(is_lat), j >= 2))
    def _():
        shifted(1, t_ctx != l_ctx - 1)

    h = hbuf[GRID_W:GRID_W + tr, :]
    xx = sbuf[...] - h
    for m in range(6):
        o_refs[m][...] = (h + xx * mix_ref[m:m + 1, :]).astype(BF16)


def _rwkv_prep(x, mod, mix, dims):
    r, d = x.shape
    tr = dims["tr"]
    dc = d // 4
    nb = tr // GRID_W
    last = r // GRID_W - 1
    kern = functools.partial(_prep_kernel, tr=tr, n_lat=dims["n_lat"], l_img=dims["L"], l_ctx=dims["Lc"])
    gmap = dims["gmap"]
    outs = pl.pallas_call(
        kern,
        grid=(r // tr, 4),
        in_specs=[
            pl.BlockSpec((GRID_W, dc), lambda i, j: (jnp.maximum(i * nb - 1, 0), j)),
            pl.BlockSpec((tr, dc), lambda i, j: (i, j)),
            pl.BlockSpec((GRID_W, dc), lambda i, j: (jnp.minimum(i * nb + nb, last), j)),
            pl.BlockSpec((1, 6, dc), lambda i, j: (gmap(i), 0, j)),
            pl.BlockSpec((6, dc), lambda i, j: (0, j)),
        ],
        out_specs=[pl.BlockSpec((tr, dc), lambda i, j: (i, j)) for _ in range(6)],
        out_shape=[jax.ShapeDtypeStruct((r, d), BF16) for _ in range(6)],
        scratch_shapes=[pltpu.VMEM((tr + 2 * GRID_W, dc), F32), pltpu.VMEM((tr, dc), F32)],
        compiler_params=_cparams("parallel", "parallel"),
        name="rwkv_prep",
    )(x, x, x, mod, mix)
    return outs


def _feat_kernel(*refs, has_vf, rw, ra):
    refs = list(refs)
    r_ref, k_ref, v_ref, tw_ref, ax_ref = [refs.pop(0) for _ in range(5)]
    w2_ref, a2_ref, w0_ref, a0_ref, vec_ref, seg_ref = [refs.pop(0) for _ in range(6)]
    if has_vf:
        vf_ref, xv_ref, v2_ref, v0_ref = [refs.pop(0) for _ in range(4)]
    kk_o, lw0_o, lw1_o, kd0_o, kd1_o, nb0_o, nb1_o, bonus_o = refs[:8]
    v_o = refs[8] if has_vf else None

    k = k_ref[...]
    r = r_ref[...]
    v = v_ref[...]
    k_k = vec_ref[0:1, :]
    k_a = vec_ref[1:2, :]
    r_k = vec_ref[2:3, :]
    kraw = k * k_k
    kk = kraw * lax.rsqrt(jnp.maximum(_head_sum(kraw * kraw, seg_ref), 1e-24))
    kk_o[...] = kk
    if has_vf:
        vl = v0_ref[...] + jnp.dot(xv_ref[...], v2_ref[...], preferred_element_type=F32)
        v = v + (vf_ref[...] - v) * jax.nn.sigmoid(vl)
        v_o[...] = v
    kd_sum = None
    for d, (lw_o, kd_o, nb_o) in enumerate(((lw0_o, kd0_o, nb0_o), (lw1_o, kd1_o, nb1_o))):
        wl = w0_ref[d:d + 1, :] + jnp.dot(tw_ref[:, d * rw:(d + 1) * rw], w2_ref[d],
                                          preferred_element_type=F32)
        lw_o[...] = -math.exp(-0.5) * jax.nn.sigmoid(wl)
        al = a0_ref[d:d + 1, :] + jnp.dot(ax_ref[:, d * ra:(d + 1) * ra], a2_ref[d],
                                          preferred_element_type=F32)
        a = jax.nn.sigmoid(al)
        kd = k * (1.0 + (a - 1.0) * k_a)
        kd_o[...] = kd
        nb_o[...] = -(kk * a)
        kd_sum = kd if kd_sum is None else kd_sum + kd
    bonus_o[...] = _head_sum(r * kd_sum * r_k, seg_ref) * v


def _rwkv_features(r, k, v, tw, ax, p, vf, xv):
    rows, d = r.shape
    tr = _pick(rows, (256, 128, 64, 32, 16, 8))
    cb = _pick(d, (1024, 512, 256, 128))
    cw = min(cb, MXU)
    has_vf = vf is not None
    rw = p["w2"].shape[1]
    ra = p["a2"].shape[1]
    tile = pl.BlockSpec((tr, cb), lambda i, j: (i, j))

    def full_rows(a):
        return pl.BlockSpec((tr, a.shape[1]), lambda i, j: (i, 0))

    vec = jnp.stack([p["k_k"], p["k_a"], p["r_k"]]).astype(F32)
    in_specs = [tile, tile, tile, full_rows(tw), full_rows(ax),
                pl.BlockSpec((2, rw, cb), lambda i, j: (0, 0, j)),
                pl.BlockSpec((2, ra, cb), lambda i, j: (0, 0, j)),
                pl.BlockSpec((2, cb), lambda i, j: (0, j)),
                pl.BlockSpec((2, cb), lambda i, j: (0, j)),
                pl.BlockSpec((3, cb), lambda i, j: (0, j)),
                pl.BlockSpec((cw, cw), lambda i, j: (0, 0))]
    args = [r, k, v, tw, ax, p["w2"], p["a2"], p["w0"], p["a0"], vec, _seg_ones(cw)]
    n_out = 8
    if has_vf:
        in_specs += [tile, full_rows(xv),
                     pl.BlockSpec((p["v2"].shape[0], cb), lambda i, j: (0, j)),
                     pl.BlockSpec((1, cb), lambda i, j: (0, j))]
        args += [vf, xv, p["v2"], p["v0"].reshape(1, -1)]
        n_out = 9
    return pl.pallas_call(
        functools.partial(_feat_kernel, has_vf=has_vf, rw=rw, ra=ra),
        grid=(rows // tr, d // cb),
        in_specs=in_specs,
        out_specs=[tile] * n_out,
        out_shape=[jax.ShapeDtypeStruct((rows, d), F32)] * n_out,
        compiler_params=_cparams("parallel", "parallel"),
        name="rwkv_features",
    )(*args)


def _scan_kernel(*refs, reverse, has_prev):
    refs = list(refs)
    r_ref, lw_ref, kd_ref, kk_ref, nb_ref, v_ref = [refs.pop(0) for _ in range(6)]
    yp_ref = refs.pop(0) if has_prev else None
    y_ref, s_ref = refs
    tt = CHUNK
    gw = HPG * HEAD
    n_groups = y_ref.shape[1] // gw

    @pl.when(pl.program_id(1) == 0)
    def _():
        s_ref[...] = jnp.zeros_like(s_ref)

    row_t = lax.broadcasted_iota(jnp.int32, (tt, gw), 0)
    lane = lax.broadcasted_iota(jnp.int32, (tt, gw), 1)
    pos = lane % HEAD
    if reverse:
        strict, incl = pos > row_t, pos >= row_t
    else:
        strict, incl = pos < row_t, pos <= row_t
    eye = (pos == row_t).astype(F32)
    near = {}
    m = INV_BASE
    while m <= tt:
        near[m] = (pos // m) == (row_t // m)
        m *= 2
    ci = lax.broadcasted_iota(jnp.int32, (tt, tt), 0)
    cj = lax.broadcasted_iota(jnp.int32, (tt, tt), 1)
    csum = ((cj >= ci) if reverse else (cj <= ci)).astype(BF16)
    head_of_row = lax.broadcasted_iota(jnp.int32, (gw, gw), 0) // HEAD
    head_of_lane = lax.broadcasted_iota(jnp.int32, (gw, gw), 1) // HEAD
    same_head = head_of_row == head_of_lane
    last = 0 if reverse else tt - 1

    def mm(a, b):
        return jnp.dot(a.astype(BF16), b.astype(BF16), preferred_element_type=F32)

    def mm_nt(a, b):
        return lax.dot_general(a.astype(BF16), b.astype(BF16), (((1,), (1,)), ((), ())),
                               preferred_element_type=F32)

    def mm_tn(a, b):
        return lax.dot_general(a.astype(BF16), b.astype(BF16), (((0,), (0,)), ((), ())),
                               preferred_element_type=F32)

    def blockdiag(m):
        return jnp.where(same_head, jnp.concatenate([m.astype(BF16)] * HPG, axis=0), 0.0)

    def group(sl, s0):
        lw = lw_ref[:, sl]
        lw_hi = lw.astype(BF16)
        lw_mid = (lw - lw_hi.astype(F32)).astype(BF16)
        lw_lo = (lw - lw_hi.astype(F32) - lw_mid.astype(F32)).astype(BF16)
        c_in = (jnp.dot(csum, lw_hi, preferred_element_type=F32)
                + jnp.dot(csum, lw_mid, preferred_element_type=F32)
                + jnp.dot(csum, lw_lo, preferred_element_type=F32))
        c_tot = c_in[last:last + 1, :]
        g_inv = jnp.exp(-c_in)
        at = kk_ref[:, sl] * jnp.exp(c_in - lw)
        rt = r_ref[:, sl] * jnp.exp(c_in)
        nb = nb_ref[:, sl]
        kd = kd_ref[:, sl]
        v = v_ref[:, sl]
        lr = jnp.concatenate([at, rt], axis=0)
        yield
        ar = mm_nt(lr, s0)
        yield
        g_nb = mm_nt(lr, blockdiag(nb * g_inv))
        yield
        g_k = mm_nt(lr, blockdiag(kd * g_inv))
        yield
        n_mat = jnp.where(strict, g_nb[:tt], 0.0)
        m_ak = jnp.where(strict, g_k[:tt], 0.0)
        m_rb = jnp.where(incl, g_nb[tt:], 0.0)
        m_rk = jnp.where(incl, g_k[tt:], 0.0)
        mv = mm(jnp.concatenate([m_ak, m_rk], axis=0), blockdiag(v))
        x = ar[:tt] + mv[:tt]
        n_d = jnp.where(near[INV_BASE], n_mat, 0.0)
        yield
        p = mm(n_d, blockdiag(n_d))
        t_inv = eye + n_d
        yield
        t_inv = t_inv + mm(t_inv, blockdiag(p))
        m = INV_BASE
        while m < tt:
            n_off = jnp.where(jnp.logical_and(near[2 * m], jnp.logical_not(near[m])), n_mat, 0.0)
            t_bd = blockdiag(t_inv)
            yield
            e = mm(n_off, t_bd)
            yield
            t_inv = t_inv + mm(t_inv, blockdiag(e))
            m *= 2
        yield
        u = mm(t_inv, blockdiag(x))
        yield
        y = ar[tt:] + mm(m_rb, blockdiag(u)) + mv[tt:]
        if has_prev:
            y = y + yp_ref[:, sl]
        rem = jnp.exp(c_tot - c_in)
        upd = mm_tn(jnp.concatenate([u, v], axis=0), jnp.concatenate([nb * rem, kd * rem], axis=0))
        return y, s0 * jnp.exp(c_tot) + jnp.where(same_head, upd, 0.0)

    par = _pick(n_groups, (GROUPS_PER_TRIP, 4, 2, 1))

    def trip(i, carry):
        gs = [i * par + u for u in range(par)]
        sls = [pl.ds(pl.multiple_of(g * gw, gw), gw) for g in gs]
        chains = [group(sl, s_ref[g]) for g, sl in zip(gs, sls)]
        outs = [None] * par
        while any(o is None for o in outs):
            for u, chain in enumerate(chains):
                if outs[u] is None:
                    try:
                        next(chain)
                    except StopIteration as done:
                        outs[u] = done.value
        for g, sl, (y, s_new) in zip(gs, sls, outs):
            y_ref[:, sl] = y
            s_ref[g] = s_new
        return carry

    lax.fori_loop(0, n_groups // par, trip, 0)


def _rwkv_scan(r, lw, kd, kk, nb, v, y_prev, reverse, dims):
    rows, d = r.shape
    bsz, l, lc, n_lat = dims["B"], dims["L"], dims["Lc"], dims["n_lat"]
    tt = CHUNK
    n_ctx, n_latc = lc // tt, l // tt
    nc = n_ctx + n_latc

    def rblk(b, c):
        if reverse:
            ctx_blk = (n_lat + b * lc) // tt + (n_ctx - 1 - c)
            lat_blk = b * n_latc + (n_latc - 1 - (c - n_ctx))
        else:
            ctx_blk = (n_lat + b * lc) // tt + c
            lat_blk = b * n_latc + (c - n_ctx)
        return jnp.where(c < n_ctx, ctx_blk, lat_blk)

    blk = pl.BlockSpec((tt, d), lambda b, c: (rblk(b, c), 0))
    args = [r, lw, kd, kk, nb, v]
    if y_prev is not None:
        args.append(y_prev)
    gw = HPG * HEAD
    return pl.pallas_call(
        functools.partial(_scan_kernel, reverse=reverse, has_prev=y_prev is not None),
        grid=(bsz, nc),
        in_specs=[blk] * len(args),
        out_specs=blk,
        out_shape=jax.ShapeDtypeStruct((rows, d), F32),
        scratch_shapes=[pltpu.VMEM((d // gw, gw, gw), F32)],
        compiler_params=_cparams("parallel", "arbitrary"),
        name="rwkv_scan_rev" if reverse else "rwkv_scan_fwd",
    )(*args)


def _readout_kernel(ys_ref, bonus_ref, sg_ref, g2_ref, lnx_ref, seg_ref, o_ref):
    ys = ys_ref[...]
    inv = 1.0 / HEAD
    mu = _head_sum(ys, seg_ref) * inv
    dlt = ys - mu
    var = _head_sum(dlt * dlt, seg_ref) * inv
    y = dlt * lax.rsqrt(var + GN_EPS) * lnx_ref[0:1, :] + lnx_ref[1:2, :] + bonus_ref[...]
    g = jnp.dot(sg_ref[...], g2_ref[...], preferred_element_type=F32)
    o_ref[...] = (y * g).astype(o_ref.dtype)


def _rwkv_readout(ys, bonus, sg, g2, lnx_g, lnx_b):
    rows, d = ys.shape
    tr = _pick(rows, (256, 128, 64, 32, 16, 8))
    cb = _pick(d, (1024, 512, 256, 128))
    cw = min(cb, MXU)
    tile = pl.BlockSpec((tr, cb), lambda i, j: (i, j))
    lnx = jnp.stack([lnx_g, lnx_b]).astype(F32)
    return pl.pallas_call(
        _readout_kernel,
        grid=(rows // tr, d // cb),
        in_specs=[tile, tile,
                  pl.BlockSpec((tr, sg.shape[1]), lambda i, j: (i, 0)),
                  pl.BlockSpec((g2.shape[0], cb), lambda i, j: (0, j)),
                  pl.BlockSpec((2, cb), lambda i, j: (0, j)),
                  pl.BlockSpec((cw, cw), lambda i, j: (0, 0))],
        out_specs=tile,
        out_shape=jax.ShapeDtypeStruct((rows, d), BF16),
        compiler_params=_cparams("parallel", "parallel"),
        name="rwkv_readout",
    )(ys, bonus, sg, g2, lnx, _seg_ones(cw))


def _resid_ln_kernel(x_ref, y_ref, mod_ref, ln_ref, xo_ref, *rest, alpha, gate_row, mod_rows):
    gate = mod_ref[0, gate_row:gate_row + 1, :]
    z = alpha * x_ref[...] + gate * y_ref[...]
    mu = jnp.mean(z, axis=-1, keepdims=True)
    dz = z - mu
    var = jnp.mean(dz * dz, axis=-1, keepdims=True)
    zn = dz * lax.rsqrt(var + LN_EPS) * ln_ref[0:1, :] + ln_ref[1:2, :]
    xo_ref[...] = zn
    if mod_rows is not None:
        sh = mod_ref[0, mod_rows[0]:mod_rows[0] + 1, :]
        sc = mod_ref[0, mod_rows[1]:mod_rows[1] + 1, :]
        rest[0][...] = (zn * (1.0 + sc) + sh).astype(rest[0].dtype)


def _resid_ln(x, y, mod, ln_g, ln_b, alpha, gate_row, mod_rows, dims, rows=None):
    d = x.shape[1]
    rows = rows or x.shape[0]
    tr = _pick(dims["tr"], (256, 128, 64, 32, 16, 8))
    gmap = dims["gmap"]
    ratio = dims["tr"] // tr
    tile = pl.BlockSpec((tr, d), lambda i: (i, 0))
    ln = jnp.stack([ln_g, ln_b]).astype(F32)
    out_shape = [jax.ShapeDtypeStruct((rows, d), F32)]
    out_specs = [tile]
    if mod_rows is not None:
        out_shape.append(jax.ShapeDtypeStruct((rows, d), BF16))
        out_specs.append(tile)
    outs = pl.pallas_call(
        functools.partial(_resid_ln_kernel, alpha=alpha, gate_row=gate_row, mod_rows=mod_rows),
        grid=(rows // tr,),
        in_specs=[tile, tile,
                  pl.BlockSpec((1, 6, d), lambda i: (gmap(i // ratio), 0, 0)),
                  pl.BlockSpec((2, d), lambda i: (0, 0))],
        out_specs=out_specs,
        out_shape=out_shape,
        compiler_params=_cparams("parallel"),
        name="resid_ln",
    )(x, y, mod, ln)
    return outs


def _s5_kernel(*refs, tcs, nq, ns, finish):
    refs = list(refs)
    x_ref, mod_ref, bm_ref, cm_ref, lam_ref = [refs.pop(0) for _ in range(5)]
    if finish:
        yf_ref, d_ref = refs.pop(0), refs.pop(0)
    y_ref = refs.pop(0)
    hre_refs = [refs.pop(0) for _ in range(nq)]
    him_refs = [refs.pop(0) for _ in range(nq)]
    h2d_refs = [refs.pop(0) for _ in range(nq)]
    cr_ref, ci_ref = refs
    reverse = finish
    nsl = ns // LANE

    @pl.when(pl.program_id(2) == 0)
    def _():
        cr_ref[...] = jnp.zeros_like(cr_ref)
        ci_ref[...] = jnp.zeros_like(ci_ref)

    def u_of(q):
        sl = slice(q * S5_CH, (q + 1) * S5_CH)
        return x_ref[:, sl] * (1.0 + mod_ref[0, 1:2, sl]) + mod_ref[0, 0:1, sl]

    for q in range(nq):
        bu = jnp.dot(u_of(q).astype(BF16), bm_ref[q], preferred_element_type=F32)
        for s in range(nsl):
            hre_refs[q][pl.ds(s, tcs, stride=nsl), :] = bu[:, s * LANE:(s + 1) * LANE]
            him_refs[q][pl.ds(s, tcs, stride=nsl), :] = bu[:, ns + s * LANE:ns + (s + 1) * LANE]

    h = [(cr_ref[q], ci_ref[q]) for q in range(nq)]
    lam = [(lam_ref[q, 0], lam_ref[q, 1]) for q in range(nq)]
    for step in range(tcs):
        off = (tcs - 1 - step if reverse else step) * nsl
        for q in range(nq):
            (hr, hi), (lr, li) = h[q], lam[q]
            nr = lr * hr - li * hi + hre_refs[q][pl.ds(off, nsl), :]
            ni = lr * hi + li * hr + him_refs[q][pl.ds(off, nsl), :]
            hre_refs[q][pl.ds(off, nsl), :] = nr
            him_refs[q][pl.ds(off, nsl), :] = ni
            h[q] = (nr, ni)
    for q in range(nq):
        cr_ref[q], ci_ref[q] = h[q]

    for q in range(nq):
        for s in range(nsl):
            h2d_refs[q][:, s * LANE:(s + 1) * LANE] = hre_refs[q][pl.ds(s, tcs, stride=nsl), :].astype(BF16)
            h2d_refs[q][:, ns + s * LANE:ns + (s + 1) * LANE] = \
                him_refs[q][pl.ds(s, tcs, stride=nsl), :].astype(BF16)
        sl = slice(q * S5_CH, (q + 1) * S5_CH)
        y = jnp.dot(h2d_refs[q][...], cm_ref[q], preferred_element_type=F32)
        if finish:
            y = y + yf_ref[:, sl] + d_ref[:, sl] * u_of(q)
            c = math.sqrt(2.0 / math.pi)
            y = 0.5 * y * (1.0 + jnp.tanh(c * (y + 0.044715 * (y * y * y))))
        y_ref[:, sl] = y.astype(y_ref.dtype)


def _s5_scan(x, mod, bm, cm, lam, y_fwd, dvec, dims):
    reverse = y_fwd is not None
    rows, d = x.shape
    tcs = dims["tcs"]
    ns2 = bm.shape[2]
    ns = ns2 // 2
    nq = _pick(d // S5_CH, (4, 2, 1))
    cbw = nq * S5_CH
    n_lat_blk = dims["L"] // tcs
    n_ctx_blk = dims["Lc"] // tcs
    n_lat_rows_blk = dims["n_lat"] // tcs
    nchunk = n_ctx_blk + n_lat_blk

    def rmap(b, c):
        if reverse:
            ctx_blk = n_lat_rows_blk + b * n_ctx_blk + (n_ctx_blk - 1 - c)
            lat_blk = b * n_lat_blk + (n_lat_blk - 1 - (c - n_ctx_blk))
        else:
            ctx_blk = n_lat_rows_blk + b * n_ctx_blk + c
            lat_blk = b * n_lat_blk + (c - n_ctx_blk)
        return jnp.where(c < n_ctx_blk, ctx_blk, lat_blk)

    bsz = dims["B"]
    tile = pl.BlockSpec((tcs, cbw), lambda b, g, c: (rmap(b, c), g))
    in_specs = [tile,
                pl.BlockSpec((1, 6, cbw), lambda b, g, c: (jnp.where(c < n_ctx_blk, bsz, b), 0, g)),
                pl.BlockSpec((nq, S5_CH, ns2), lambda b, g, c: (g, 0, 0)),
                pl.BlockSpec((nq, ns2, S5_CH), lambda b, g, c: (g, 0, 0)),
                pl.BlockSpec((nq, 2, ns // LANE, LANE), lambda b, g, c: (g, 0, 0, 0))]
    args = [x, mod, bm, cm, lam]
    if reverse:
        in_specs += [tile, pl.BlockSpec((1, cbw), lambda b, g, c: (0, g))]
        args += [y_fwd, dvec.reshape(1, -1).astype(F32)]
    return pl.pallas_call(
        functools.partial(_s5_kernel, tcs=tcs, nq=nq, ns=ns, finish=reverse),
        grid=(bsz, d // cbw, nchunk),
        in_specs=in_specs,
        out_specs=tile,
        out_shape=jax.ShapeDtypeStruct((rows, d), BF16 if reverse else F32),
        scratch_shapes=([pltpu.VMEM((tcs * ns // LANE, LANE), F32)] * (2 * nq)
                        + [pltpu.VMEM((tcs, ns2), BF16)] * nq
                        + [pltpu.VMEM((nq, ns // LANE, LANE), F32)] * 2),
        compiler_params=_cparams("parallel", "parallel", "arbitrary"),
        name="s5_scan_rev" if reverse else "s5_scan_fwd",
    )(*args)


def _s5_params(lam_re, lam_im, log_dt, b_re, b_im, c_re, c_im, gpb):
    lam = lax.complex(lam_re.astype(F32), lam_im.astype(F32))
    dt = jnp.exp(log_dt.astype(F32))[..., None]
    lam_bar = jnp.exp(lam * dt)
    b_bar = ((lam_bar - 1.0) / lam)[..., None] * lax.complex(b_re.astype(F32), b_im.astype(F32))
    c_mat = lax.complex(c_re.astype(F32), c_im.astype(F32))
    ndir, g, p = lam_bar.shape
    i_sz = b_bar.shape[-1]
    nblk = g // gpb
    eye = jnp.eye(gpb, dtype=F32)

    def blockdiag_in(m):
        m = m.reshape(ndir, nblk, gpb, p, i_sz)
        out = jnp.einsum("dngpi,gh->dngihp", m, eye)
        return out.reshape(ndir, nblk, gpb * i_sz, gpb * p)

    def blockdiag_out(m):
        m = m.reshape(ndir, nblk, gpb, i_sz, p)
        out = jnp.einsum("dngip,gh->dngphi", m, eye)
        return out.reshape(ndir, nblk, gpb * p, gpb * i_sz)

    bm = jnp.concatenate([blockdiag_in(jnp.real(b_bar)), blockdiag_in(jnp.imag(b_bar))], axis=-1).astype(BF16)
    cm = jnp.concatenate([blockdiag_out(jnp.real(c_mat)), blockdiag_out(-jnp.imag(c_mat))], axis=-2).astype(BF16)
    lam_t = jnp.stack([jnp.real(lam_bar), jnp.imag(lam_bar)], axis=1)
    lam_t = lam_t.reshape(ndir, 2, nblk, gpb * p // LANE, LANE).transpose(0, 2, 1, 3, 4).astype(F32)
    return bm, cm, lam_t


def kernel(x, c, ctx, c_ctx, ada_down, ada_up, ada_bias, ln_g, ln_b, rw_mix, rw_wr, rw_wk, rw_wv, rw_wo, rw_w0, rw_w1, rw_w2, rw_a0, rw_a1, rw_a2, rw_v0, rw_v1, rw_v2, rw_g1, rw_g2, rw_kk, rw_ka, rw_rk, rw_lnx_g, rw_lnx_b, s5_lam_re, s5_lam_im, s5_log_dt, s5_b_re, s5_b_im, s5_c_re, s5_c_im, s5_d, s5_glu_w, s5_glu_b, mlp_w1, mlp_w2):
    bsz, seq, d = x.shape
    lc = ctx.shape[1]
    depth = ada_down.shape[0]
    assert d % S5_CH == 0 and seq % CHUNK == 0 and lc % CHUNK == 0
    alpha = (2 * depth) ** 0.25
    n_lat = bsz * seq
    tr = _pick(math.gcd(seq, bsz * lc), (512, 256, 128, 64))
    tiles_per_batch = seq // tr

    def gmap(i):
        return jnp.minimum(i // tiles_per_batch, bsz)

    g_t = math.gcd(seq, lc)
    dims = {"B": bsz, "L": seq, "Lc": lc, "n_lat": n_lat, "tr": tr, "gmap": gmap,
            "tcs": _pick(g_t, (256, 128, 64, 32, 16, 8))}

    xs = jnp.concatenate([x.reshape(n_lat, d), ctx.reshape(bsz * lc, d)], axis=0).astype(F32)
    vf = None

    cvec = jnp.concatenate([c, c_ctx[None, :]], axis=0)
    cvec = _pad_to(jax.nn.silu(cvec), 0, 16).astype(BF16)
    wr_b, wk_b, wv_b, wo_b = (w.astype(BF16) for w in (rw_wr, rw_wk, rw_wv, rw_wo))
    glu_b = s5_glu_w.astype(BF16)
    w1_b, w2_b = mlp_w1[0].astype(BF16), mlp_w2[0].astype(BF16)

    for i in range(depth):
        low = _mm(cvec, ada_down[i].astype(BF16), out_dtype=BF16, name="adaln_down")
        mod = _mm(low, ada_up[i].astype(BF16), bias=ada_bias[i], name="adaln_up")
        mod = mod[:bsz + 1].reshape(bsz + 1, 6, d)
        j = i // 2
        if i % 2 == 0:
            p = {"w2": _pad_to(rw_w2[j], 1, LANE).astype(BF16), "a2": _pad_to(rw_a2[j], 1, LANE).astype(BF16),
                 "w0": rw_w0[j], "a0": rw_a0[j], "k_k": rw_kk[j], "k_a": rw_ka[j],
                 "r_k": rw_rk[j].reshape(-1)}
            lerps = _rwkv_prep(xs, mod, rw_mix[j], dims)
            r = _mm(lerps[0], wr_b, layer=j, name="rwkv_r")
            k = _mm(lerps[2], wk_b, layer=j, name="rwkv_k")
            v = _mm(lerps[3], wv_b, layer=j, name="rwkv_v")
            w1cat = jnp.concatenate([_pad_to(rw_w1[j, dd], 1, LANE) for dd in range(2)], axis=1)
            a1cat = jnp.concatenate([_pad_to(rw_a1[j, dd], 1, LANE) for dd in range(2)], axis=1)
            tw = _mm(lerps[1], w1cat.astype(BF16), act="tanh", out_dtype=BF16, name="rwkv_w1")
            ax = _mm(lerps[4], a1cat.astype(BF16), out_dtype=BF16, name="rwkv_a1")
            sg = _mm(lerps[5], _pad_to(rw_g1[j], 1, LANE).astype(BF16), act="sigmoid", out_dtype=BF16,
                     name="rwkv_g1")
            g2 = _pad_to(rw_g2[j], 0, LANE).astype(BF16)
            xv = None
            if j > 0:
                xv = _mm(lerps[3], _pad_to(rw_v1[j - 1], 1, LANE).astype(BF16), out_dtype=BF16, name="rwkv_v1")
                p["v2"] = _pad_to(rw_v2[j - 1], 0, LANE).astype(BF16)
                p["v0"] = rw_v0[j - 1]
            feats = _rwkv_features(r, k, v, tw, ax, p, vf if j > 0 else None, xv)
            kk, lw0, lw1, kd0, kd1, nb0, nb1, bonus = feats[:8]
            if j > 0:
                v = feats[8]
            else:
                vf = v
            y = _rwkv_scan(r, lw0, kd0, kk, nb0, v, None, False, dims)
            y = _rwkv_scan(r, lw1, kd1, kk, nb1, v, y, True, dims)
            yg = _rwkv_readout(y, bonus, sg, g2, rw_lnx_g[j], rw_lnx_b[j])
            mix_out = _mm(yg, wo_b, layer=j, name="rwkv_o")
        else:
            gpb = S5_CH // S5_GROUP
            bm, cm, lam_t = _s5_params(s5_lam_re[j], s5_lam_im[j], s5_log_dt[j], s5_b_re[j], s5_b_im[j],
                                       s5_c_re[j], s5_c_im[j], gpb)
            yf = _s5_scan(xs, mod, bm[0], cm[0], lam_t[0], None, None, dims)
            gl = _s5_scan(xs, mod, bm[1], cm[1], lam_t[1], yf, s5_d[j], dims)
            mix_out = _mm(gl, glu_b, layer=j, bias=s5_glu_b[j], glu=True,
                          tm=_pick(gl.shape[0], (768, 512, 256, 128, 64, 32, 16, 8)), name="s5_glu")
        xs, h2 = _resid_ln(xs, mix_out, mod, ln_g[i, 0], ln_b[i, 0], alpha, 2, (3, 4), dims)
        more = i + 1 < depth
        res = _mm(h2, w1_b, act="relu2", out_dtype=BF16, cast_next=(mlp_w1, i + 1) if more else None,
                  name="mlp_w1")
        a1, w1_next = res if more else (res, None)
        res = _mm(a1, w2_b, tm=_pick(h2.shape[0], (768, 512, 256, 128, 64, 32, 16, 8)),
                  tn=_pick(d, (1024, 512, 256, 128)), tk=_pick(w2_b.shape[0], (4096, 2048, 1024, 512, 256, 128)),
                  cast_next=(mlp_w2, i + 1) if more else None, name="mlp_w2")
        mlp_out, w2_next = res if more else (res, None)
        w1_b, w2_b = w1_next, w2_next
        (xs,) = _resid_ln(xs, mlp_out, mod, ln_g[i, 1], ln_b[i, 1], alpha, 5, None, dims,
                          rows=n_lat if i == depth - 1 else None)
    return xs.reshape(bsz, seq, d).astype(x.dtype)
```

```python
import functools
import math

import jax
import jax.numpy as jnp
from jax import lax
from jax.experimental import pallas as pl
from jax.experimental.pallas import tpu as pltpu

F32 = jnp.float32
BF16 = jnp.bfloat16

GRID_W = 64
HEAD = 64
S5_GROUP = 16
S5_CH = 256
LN_EPS = 1e-6
GN_EPS = 64e-5
LANE = 128
MXU = 256
HPG = MXU // HEAD
INV_BASE = 4
GROUPS_PER_TRIP = 16
CHUNK = 64
VMEM_LIMIT = 56 * 1024 * 1024


def _cparams(*sem):
    return pltpu.CompilerParams(dimension_semantics=sem, vmem_limit_bytes=VMEM_LIMIT)


def _pick(n, prefs):
    for p in prefs:
        if n % p == 0:
            return p
    return n


def _pad_to(a, axis, mult):
    n = a.shape[axis]
    r = (-n) % mult
    if r == 0:
        return a
    pad = [(0, 0)] * a.ndim
    pad[axis] = (0, r)
    return jnp.pad(a, pad)


def _mm_kernel(*refs, nk, act, has_bias, glu, has_cast):
    refs = list(refs)
    x_ref = refs.pop(0)
    w_refs = [refs.pop(0) for _ in range(2 if glu else 1)]
    b_refs = [refs.pop(0) for _ in range((2 if glu else 1) if has_bias else 0)]
    cast_in = refs.pop(0) if has_cast else None
    o_ref = refs.pop(0)
    if has_cast:
        refs.pop(0)[...] = cast_in[...].astype(BF16)
    acc_refs = refs
    k = pl.program_id(2)

    def finish(zs):
        if has_bias:
            zs = [z + b[...] for z, b in zip(zs, b_refs)]
        if glu:
            z = zs[0] * jax.nn.sigmoid(zs[1])
        else:
            z = zs[0]
            if act == "tanh":
                z = jnp.tanh(z)
            elif act == "sigmoid":
                z = jax.nn.sigmoid(z)
            elif act == "relu2":
                z = jnp.square(jnp.maximum(z, 0.0))
        o_ref[...] = z.astype(o_ref.dtype)

    x = x_ref[...]
    parts = [jnp.dot(x, w[...], preferred_element_type=F32) for w in w_refs]
    if nk == 1:
        finish(parts)
        return

    @pl.when(k == 0)
    def _():
        for a in acc_refs:
            a[...] = jnp.zeros_like(a)

    for a, p in zip(acc_refs, parts):
        a[...] += p

    @pl.when(k == nk - 1)
    def _():
        finish([a[...] for a in acc_refs])


def _mm(x, w, bias=None, act=None, out_dtype=F32, glu=False, tm=None, tn=None, tk=None, layer=None,
        cast_next=None, name="mm"):
    m, kdim = x.shape
    n = w.shape[-1] // (2 if glu else 1)
    tm = tm or _pick(m, (1536, 1024, 768, 512, 256, 128, 64, 32, 16, 8))
    tn = tn or _pick(n, (512, 256, 128))
    tk = tk or (kdim if kdim <= 4096 else _pick(kdim, (2048, 1024, 512, 256, 128)))
    nk = kdim // tk
    nj = n // tn

    def w_spec(off):
        if layer is None:
            return pl.BlockSpec((tk, tn), lambda i, j, k: (k, j + off))
        return pl.BlockSpec((None, tk, tn), lambda i, j, k: (layer, k, j + off))

    in_specs = [pl.BlockSpec((tm, tk), lambda i, j, k: (i, k)), w_spec(0)]
    args = [x, w]
    if glu:
        in_specs.append(w_spec(nj))
        args.append(w)
    if bias is not None:
        b2 = bias.reshape(1, -1).astype(F32)
        in_specs.append(pl.BlockSpec((1, tn), lambda i, j, k: (0, j)))
        args.append(b2)
        if glu:
            in_specs.append(pl.BlockSpec((1, tn), lambda i, j, k: (0, j + nj)))
            args.append(b2)
    scratch = [] if nk == 1 else [pltpu.VMEM((tm, tn), F32) for _ in range(2 if glu else 1)]
    out_specs = [pl.BlockSpec((tm, tn), lambda i, j, k: (i, j))]
    out_shape = [jax.ShapeDtypeStruct((m, n), out_dtype)]
    if cast_next is not None:
        src, src_layer = cast_next
        rows2, cols2 = src.shape[1:]
        steps = (m // tm) * nj * nk
        pc = _pick(cols2, (256, 128))
        pr = next((r for r in (rows2 // f for f in (16, 8, 4, 2, 1)) if (rows2 // r) * (cols2 // pc) <= steps),
                  None)
        if pr is None:
            return _mm(x, w, bias, act, out_dtype, glu, tm, tn, tk, layer, None, name), \
                src[src_layer].astype(BF16)
        ncol = cols2 // pc
        last_piece = (rows2 // pr) * ncol - 1

        def piece(i, j, k):
            return jnp.minimum((i * nj + j) * nk + k, last_piece)

        in_specs.append(pl.BlockSpec((None, pr, pc), lambda i, j, k: (src_layer, piece(i, j, k) // ncol,
                                                                     piece(i, j, k) % ncol)))
        args.append(src)
        out_specs.append(pl.BlockSpec((pr, pc), lambda i, j, k: (piece(i, j, k) // ncol, piece(i, j, k) % ncol)))
        out_shape.append(jax.ShapeDtypeStruct((rows2, cols2), BF16))
    outs = pl.pallas_call(
        functools.partial(_mm_kernel, nk=nk, act=act, has_bias=bias is not None, glu=glu,
                          has_cast=cast_next is not None),
        grid=(m // tm, nj, nk),
        in_specs=in_specs,
        out_specs=out_specs,
        out_shape=out_shape,
        scratch_shapes=scratch,
        compiler_params=(_cparams("parallel", "parallel", "arbitrary") if cast_next is None
                         else _cparams("arbitrary", "arbitrary", "arbitrary")),
        name=name,
    )(*args)
    return outs[0] if cast_next is None else outs


def _head_sum(x, seg_ref):
    cw = seg_ref.shape[0]
    hi = x.astype(BF16)
    rest = x - hi.astype(F32)
    mid = rest.astype(BF16)
    lo = (rest - mid.astype(F32)).astype(BF16)
    outs = []
    for s in range(x.shape[1] // cw):
        sl = slice(s * cw, (s + 1) * cw)
        outs.append(jnp.dot(hi[:, sl], seg_ref[...], preferred_element_type=F32)
                    + jnp.dot(mid[:, sl], seg_ref[...], preferred_element_type=F32)
                    + jnp.dot(lo[:, sl], seg_ref[...], preferred_element_type=F32))
    return outs[0] if len(outs) == 1 else jnp.concatenate(outs, axis=1)


def _seg_ones(cw):
    idx = jnp.arange(cw) // HEAD
    return (idx[:, None] == idx[None, :]).astype(BF16)


def _prep_kernel(xp_ref, xc_ref, xn_ref, mod_ref, mix_ref, *rest, tr, n_lat, l_img, l_ctx):
    o_refs = rest[:6]
    hbuf, sbuf = rest[6], rest[7]
    i = pl.program_id(0)
    j = pl.program_id(1)
    sh = mod_ref[0, 0:1, :]
    sc = 1.0 + mod_ref[0, 1:2, :]
    hbuf[0:GRID_W, :] = xp_ref[...] * sc + sh
    hbuf[GRID_W:GRID_W + tr, :] = xc_ref[...] * sc + sh
    hbuf[GRID_W + tr:GRID_W + tr + GRID_W, :] = xn_ref[...] * sc + sh
    row = lax.broadcasted_iota(jnp.int32, (tr, 1), 0) + i * tr
    t_lat = row % l_img
    is_lat = i < n_lat // tr

    def shifted(s, keep):
        sbuf[...] = jnp.where(keep, hbuf[GRID_W + s:GRID_W + s + tr, :], 0.0)

    @pl.when(jnp.logical_and(is_lat, j == 0))
    def _():
        shifted(-1, t_lat % GRID_W != 0)

    @pl.when(jnp.logical_and(is_lat, j == 1))
    def _():
        shifted(1, t_lat % GRID_W != GRID_W - 1)

    @pl.when(jnp.logical_and(is_lat, j == 2))
    def _():
        shifted(-GRID_W, t_lat >= GRID_W)

    @pl.when(jnp.logical_and(is_lat, j == 3))
    def _():
        shifted(GRID_W, t_lat < l_img - GRID_W)

    t_ctx = (row - n_lat) % l_ctx

    @pl.when(jnp.logical_and(jnp.logical_not(is_lat), j < 2))
    def _():
        shifted(-1, t_ctx != 0)

    @pl.when(jnp.logical_and(jnp.logical_not(is_lat), j >= 2))
    def _():
        shifted(1, t_ctx != l_ctx - 1)

    h = hbuf[GRID_W:GRID_W + tr, :]
    xx = sbuf[...] - h
    for m in range(6):
        o_refs[m][...] = (h + xx * mix_ref[m:m + 1, :]).astype(BF16)


def _rwkv_prep(x, mod, mix, dims):
    r, d = x.shape
    tr = dims["tr"]
    dc = d // 4
    nb = tr // GRID_W
    last = r // GRID_W - 1
    kern = functools.partial(_prep_kernel, tr=tr, n_lat=dims["n_lat"], l_img=dims["L"], l_ctx=dims["Lc"])
    gmap = dims["gmap"]
    outs = pl.pallas_call(
        kern,
        grid=(r // tr, 4),
        in_specs=[
            pl.BlockSpec((GRID_W, dc), lambda i, j: (jnp.maximum(i * nb - 1, 0), j)),
            pl.BlockSpec((tr, dc), lambda i, j: (i, j)),
            pl.BlockSpec((GRID_W, dc), lambda i, j: (jnp.minimum(i * nb + nb, last), j)),
            pl.BlockSpec((1, 6, dc), lambda i, j: (gmap(i), 0, j)),
            pl.BlockSpec((6, dc), lambda i, j: (0, j)),
        ],
        out_specs=[pl.BlockSpec((tr, dc), lambda i, j: (i, j)) for _ in range(6)],
        out_shape=[jax.ShapeDtypeStruct((r, d), BF16) for _ in range(6)],
        scratch_shapes=[pltpu.VMEM((tr + 2 * GRID_W, dc), F32), pltpu.VMEM((tr, dc), F32)],
        compiler_params=_cparams("parallel", "parallel"),
        name="rwkv_prep",
    )(x, x, x, mod, mix)
    return outs


def _feat_kernel(*refs, has_vf, rw, ra):
    refs = list(refs)
    r_ref, k_ref, v_ref, tw_ref, ax_ref = [refs.pop(0) for _ in range(5)]
    w2_ref, a2_ref, w0_ref, a0_ref, vec_ref, seg_ref = [refs.pop(0) for _ in range(6)]
    if has_vf:
        vf_ref, xv_ref, v2_ref, v0_ref = [refs.pop(0) for _ in range(4)]
    kk_o, lw0_o, lw1_o, kd0_o, kd1_o, nb0_o, nb1_o, bonus_o = refs[:8]
    v_o = refs[8] if has_vf else None

    k = k_ref[...]
    r = r_ref[...]
    v = v_ref[...]
    k_k = vec_ref[0:1, :]
    k_a = vec_ref[1:2, :]
    r_k = vec_ref[2:3, :]
    kraw = k * k_k
    kk = kraw * lax.rsqrt(jnp.maximum(_head_sum(kraw * kraw, seg_ref), 1e-24))
    kk_o[...] = kk
    if has_vf:
        vl = v0_ref[...] + jnp.dot(xv_ref[...], v2_ref[...], preferred_element_type=F32)
        v = v + (vf_ref[...] - v) * jax.nn.sigmoid(vl)
        v_o[...] = v
    kd_sum = None
    for d, (lw_o, kd_o, nb_o) in enumerate(((lw0_o, kd0_o, nb0_o), (lw1_o, kd1_o, nb1_o))):
        wl = w0_ref[d:d + 1, :] + jnp.dot(tw_ref[:, d * rw:(d + 1) * rw], w2_ref[d],
                                          preferred_element_type=F32)
        lw_o[...] = -math.exp(-0.5) * jax.nn.sigmoid(wl)
        al = a0_ref[d:d + 1, :] + jnp.dot(ax_ref[:, d * ra:(d + 1) * ra], a2_ref[d],
                                          preferred_element_type=F32)
        a = jax.nn.sigmoid(al)
        kd = k * (1.0 + (a - 1.0) * k_a)
        kd_o[...] = kd
        nb_o[...] = -(kk * a)
        kd_sum = kd if kd_sum is None else kd_sum + kd
    bonus_o[...] = _head_sum(r * kd_sum * r_k, seg_ref) * v


def _rwkv_features(r, k, v, tw, ax, p, vf, xv):
    rows, d = r.shape
    tr = _pick(rows, (256, 128, 64, 32, 16, 8))
    cb = _pick(d, (1024, 512, 256, 128))
    cw = min(cb, MXU)
    has_vf = vf is not None
    rw = p["w2"].shape[1]
    ra = p["a2"].shape[1]
    tile = pl.BlockSpec((tr, cb), lambda i, j: (i, j))

    def full_rows(a):
        return pl.BlockSpec((tr, a.shape[1]), lambda i, j: (i, 0))

    vec = jnp.stack([p["k_k"], p["k_a"], p["r_k"]]).astype(F32)
    in_specs = [tile, tile, tile, full_rows(tw), full_rows(ax),
                pl.BlockSpec((2, rw, cb), lambda i, j: (0, 0, j)),
                pl.BlockSpec((2, ra, cb), lambda i, j: (0, 0, j)),
                pl.BlockSpec((2, cb), lambda i, j: (0, j)),
                pl.BlockSpec((2, cb), lambda i, j: (0, j)),
                pl.BlockSpec((3, cb), lambda i, j: (0, j)),
                pl.BlockSpec((cw, cw), lambda i, j: (0, 0))]
    args = [r, k, v, tw, ax, p["w2"], p["a2"], p["w0"], p["a0"], vec, _seg_ones(cw)]
    n_out = 8
    if has_vf:
        in_specs += [tile, full_rows(xv),
                     pl.BlockSpec((p["v2"].shape[0], cb), lambda i, j: (0, j)),
                     pl.BlockSpec((1, cb), lambda i, j: (0, j))]
        args += [vf, xv, p["v2"], p["v0"].reshape(1, -1)]
        n_out = 9
    return pl.pallas_call(
        functools.partial(_feat_kernel, has_vf=has_vf, rw=rw, ra=ra),
        grid=(rows // tr, d // cb),
        in_specs=in_specs,
        out_specs=[tile] * n_out,
        out_shape=[jax.ShapeDtypeStruct((rows, d), F32)] * n_out,
        compiler_params=_cparams("parallel", "parallel"),
        name="rwkv_features",
    )(*args)


def _scan_kernel(*refs, reverse, has_prev):
    refs = list(refs)
    r_ref, lw_ref, kd_ref, kk_ref, nb_ref, v_ref = [refs.pop(0) for _ in range(6)]
    yp_ref = refs.pop(0) if has_prev else None
    y_ref, s_ref = refs
    tt = CHUNK
    gw = HPG * HEAD
    n_groups = y_ref.shape[1] // gw

    @pl.when(pl.program_id(1) == 0)
    def _():
        s_ref[...] = jnp.zeros_like(s_ref)

    row_t = lax.broadcasted_iota(jnp.int32, (tt, gw), 0)
    lane = lax.broadcasted_iota(jnp.int32, (tt, gw), 1)
    pos = lane % HEAD
    if reverse:
        strict, incl = pos > row_t, pos >= row_t
    else:
        strict, incl = pos < row_t, pos <= row_t
    eye = (pos == row_t).astype(F32)
    near = {}
    m = INV_BASE
    while m <= tt:
        near[m] = (pos // m) == (row_t // m)
        m *= 2
    ci = lax.broadcasted_iota(jnp.int32, (tt, tt), 0)
    cj = lax.broadcasted_iota(jnp.int32, (tt, tt), 1)
    csum = ((cj >= ci) if reverse else (cj <= ci)).astype(BF16)
    head_of_row = lax.broadcasted_iota(jnp.int32, (gw, gw), 0) // HEAD
    head_of_lane = lax.broadcasted_iota(jnp.int32, (gw, gw), 1) // HEAD
    same_head = head_of_row == head_of_lane
    last = 0 if reverse else tt - 1

    def mm(a, b):
        return jnp.dot(a.astype(BF16), b.astype(BF16), preferred_element_type=F32)

    def mm_nt(a, b):
        return lax.dot_general(a.astype(BF16), b.astype(BF16), (((1,), (1,)), ((), ())),
                               preferred_element_type=F32)

    def mm_tn(a, b):
        return lax.dot_general(a.astype(BF16), b.astype(BF16), (((0,), (0,)), ((), ())),
                               preferred_element_type=F32)

    def blockdiag(m):
        return jnp.where(same_head, jnp.concatenate([m.astype(BF16)] * HPG, axis=0), 0.0)

    def group(sl, s0):
        lw = lw_ref[:, sl]
        lw_hi = lw.astype(BF16)
        lw_mid = (lw - lw_hi.astype(F32)).astype(BF16)
        lw_lo = (lw - lw_hi.astype(F32) - lw_mid.astype(F32)).astype(BF16)
        c_in = (jnp.dot(csum, lw_hi, preferred_element_type=F32)
                + jnp.dot(csum, lw_mid, preferred_element_type=F32)
                + jnp.dot(csum, lw_lo, preferred_element_type=F32))
        c_tot = c_in[last:last + 1, :]
        g_inv = jnp.exp(-c_in)
        at = kk_ref[:, sl] * jnp.exp(c_in - lw)
        rt = r_ref[:, sl] * jnp.exp(c_in)
        nb = nb_ref[:, sl]
        kd = kd_ref[:, sl]
        v = v_ref[:, sl]
        lr = jnp.concatenate([at, rt], axis=0)
        yield
        ar = mm_nt(lr, s0)
        yield
        g_nb = mm_nt(lr, blockdiag(nb * g_inv))
        yield
        g_k = mm_nt(lr, blockdiag(kd * g_inv))
        yield
        n_mat = jnp.where(strict, g_nb[:tt], 0.0)
        m_ak = jnp.where(strict, g_k[:tt], 0.0)
        m_rb = jnp.where(incl, g_nb[tt:], 0.0)
        m_rk = jnp.where(incl, g_k[tt:], 0.0)
        mv = mm(jnp.concatenate([m_ak, m_rk], axis=0), blockdiag(v))
        x = ar[:tt] + mv[:tt]
        n_d = jnp.where(near[INV_BASE], n_mat, 0.0)
        yield
        p = mm(n_d, blockdiag(n_d))
        t_inv = eye + n_d
        yield
        t_inv = t_inv + mm(t_inv, blockdiag(p))
        m = INV_BASE
        while m < tt:
            n_off = jnp.where(jnp.logical_and(near[2 * m], jnp.logical_not(near[m])), n_mat, 0.0)
            t_bd = blockdiag(t_inv)
            yield
            e = mm(n_off, t_bd)
            yield
            t_inv = t_inv + mm(t_inv, blockdiag(e))
            m *= 2
        yield
        u = mm(t_inv, blockdiag(x))
        yield
        y = ar[tt:] + mm(m_rb, blockdiag(u)) + mv[tt:]
        if has_prev:
            y = y + yp_ref[:, sl]
        rem = jnp.exp(c_tot - c_in)
        upd = mm_tn(jnp.concatenate([u, v], axis=0), jnp.concatenate([nb * rem, kd * rem], axis=0))
        return y, s0 * jnp.exp(c_tot) + jnp.where(same_head, upd, 0.0)

    par = _pick(n_groups, (GROUPS_PER_TRIP, 4, 2, 1))

    def trip(i, carry):
        gs = [i * par + u for u in range(par)]
        sls = [pl.ds(pl.multiple_of(g * gw, gw), gw) for g in gs]
        chains = [group(sl, s_ref[g]) for g, sl in zip(gs, sls)]
        outs = [None] * par
        while any(o is None for o in outs):
            for u, chain in enumerate(chains):
                if outs[u] is None:
                    try:
                        next(chain)
                    except StopIteration as done:
                        outs[u] = done.value
        for g, sl, (y, s_new) in zip(gs, sls, outs):
            y_ref[:, sl] = y
            s_ref[g] = s_new
        return carry

    lax.fori_loop(0, n_groups // par, trip, 0)


def _rwkv_scan(r, lw, kd, kk, nb, v, y_prev, reverse, dims):
    rows, d = r.shape
    bsz, l, lc, n_lat = dims["B"], dims["L"], dims["Lc"], dims["n_lat"]
    tt = CHUNK
    n_ctx, n_latc = lc // tt, l // tt
    nc = n_ctx + n_latc

    def rblk(b, c):
        if reverse:
            ctx_blk = (n_lat + b * lc) // tt + (n_ctx - 1 - c)
            lat_blk = b * n_latc + (n_latc - 1 - (c - n_ctx))
        else:
            ctx_blk = (n_lat + b * lc) // tt + c
            lat_blk = b * n_latc + (c - n_ctx)
        return jnp.where(c < n_ctx, ctx_blk, lat_blk)

    blk = pl.BlockSpec((tt, d), lambda b, c: (rblk(b, c), 0))
    args = [r, lw, kd, kk, nb, v]
    if y_prev is not None:
        args.append(y_prev)
    gw = HPG * HEAD
    return pl.pallas_call(
        functools.partial(_scan_kernel, reverse=reverse, has_prev=y_prev is not None),
        grid=(bsz, nc),
        in_specs=[blk] * len(args),
        out_specs=blk,
        out_shape=jax.ShapeDtypeStruct((rows, d), F32),
        scratch_shapes=[pltpu.VMEM((d // gw, gw, gw), F32)],
        compiler_params=_cparams("parallel", "arbitrary"),
        name="rwkv_scan_rev" if reverse else "rwkv_scan_fwd",
    )(*args)


def _readout_kernel(ys_ref, bonus_ref, sg_ref, g2_ref, lnx_ref, seg_ref, o_ref):
    ys = ys_ref[...]
    inv = 1.0 / HEAD
    mu = _head_sum(ys, seg_ref) * inv
    dlt = ys - mu
    var = _head_sum(dlt * dlt, seg_ref) * inv
    y = dlt * lax.rsqrt(var + GN_EPS) * lnx_ref[0:1, :] + lnx_ref[1:2, :] + bonus_ref[...]
    g = jnp.dot(sg_ref[...], g2_ref[...], preferred_element_type=F32)
    o_ref[...] = (y * g).astype(o_ref.dtype)


def _rwkv_readout(ys, bonus, sg, g2, lnx_g, lnx_b):
    rows, d = ys.shape
    tr = _pick(rows, (256, 128, 64, 32, 16, 8))
    cb = _pick(d, (1024, 512, 256, 128))
    cw = min(cb, MXU)
    tile = pl.BlockSpec((tr, cb), lambda i, j: (i, j))
    lnx = jnp.stack([lnx_g, lnx_b]).astype(F32)
    return pl.pallas_call(
        _readout_kernel,
        grid=(rows // tr, d // cb),
        in_specs=[tile, tile,
                  pl.BlockSpec((tr, sg.shape[1]), lambda i, j: (i, 0)),
                  pl.BlockSpec((g2.shape[0], cb), lambda i, j: (0, j)),
                  pl.BlockSpec((2, cb), lambda i, j: (0, j)),
                  pl.BlockSpec((cw, cw), lambda i, j: (0, 0))],
        out_specs=tile,
        out_shape=jax.ShapeDtypeStruct((rows, d), BF16),
        compiler_params=_cparams("parallel", "parallel"),
        name="rwkv_readout",
    )(ys, bonus, sg, g2, lnx, _seg_ones(cw))


def _resid_ln_kernel(x_ref, y_ref, mod_ref, ln_ref, xo_ref, *rest, alpha, gate_row, mod_rows):
    gate = mod_ref[0, gate_row:gate_row + 1, :]
    z = alpha * x_ref[...] + gate * y_ref[...]
    mu = jnp.mean(z, axis=-1, keepdims=True)
    dz = z - mu
    var = jnp.mean(dz * dz, axis=-1, keepdims=True)
    zn = dz * lax.rsqrt(var + LN_EPS) * ln_ref[0:1, :] + ln_ref[1:2, :]
    xo_ref[...] = zn
    if mod_rows is not None:
        sh = mod_ref[0, mod_rows[0]:mod_rows[0] + 1, :]
        sc = mod_ref[0, mod_rows[1]:mod_rows[1] + 1, :]
        rest[0][...] = (zn * (1.0 + sc) + sh).astype(rest[0].dtype)


def _resid_ln(x, y, mod, ln_g, ln_b, alpha, gate_row, mod_rows, dims, rows=None):
    d = x.shape[1]
    rows = rows or x.shape[0]
    tr = _pick(dims["tr"], (256, 128, 64, 32, 16, 8))
    gmap = dims["gmap"]
    ratio = dims["tr"] // tr
    tile = pl.BlockSpec((tr, d), lambda i: (i, 0))
    ln = jnp.stack([ln_g, ln_b]).astype(F32)
    out_shape = [jax.ShapeDtypeStruct((rows, d), F32)]
    out_specs = [tile]
    if mod_rows is not None:
        out_shape.append(jax.ShapeDtypeStruct((rows, d), BF16))
        out_specs.append(tile)
    outs = pl.pallas_call(
        functools.partial(_resid_ln_kernel, alpha=alpha, gate_row=gate_row, mod_rows=mod_rows),
        grid=(rows // tr,),
        in_specs=[tile, tile,
                  pl.BlockSpec((1, 6, d), lambda i: (gmap(i // ratio), 0, 0)),
                  pl.BlockSpec((2, d), lambda i: (0, 0))],
        out_specs=out_specs,
        out_shape=out_shape,
        compiler_params=_cparams("parallel"),
        name="resid_ln",
    )(x, y, mod, ln)
    return outs


def _s5_kernel(*refs, tcs, nq, ns, finish):
    refs = list(refs)
    x_ref, mod_ref, bm_ref, cm_ref, lam_ref = [refs.pop(0) for _ in range(5)]
    if finish:
        yf_ref, d_ref = refs.pop(0), refs.pop(0)
    y_ref = refs.pop(0)
    hre_refs = [refs.pop(0) for _ in range(nq)]
    him_refs = [refs.pop(0) for _ in range(nq)]
    h2d_refs = [refs.pop(0) for _ in range(nq)]
    cr_ref, ci_ref = refs
    reverse = finish
    nsl = ns // LANE

    @pl.when(pl.program_id(2) == 0)
    def _():
        cr_ref[...] = jnp.zeros_like(cr_ref)
        ci_ref[...] = jnp.zeros_like(ci_ref)

    def u_of(q):
        sl = slice(q * S5_CH, (q + 1) * S5_CH)
        return x_ref[:, sl] * (1.0 + mod_ref[0, 1:2, sl]) + mod_ref[0, 0:1, sl]

    for q in range(nq):
        bu = jnp.dot(u_of(q).astype(BF16), bm_ref[q], preferred_element_type=F32)
        for s in range(nsl):
            hre_refs[q][pl.ds(s, tcs, stride=nsl), :] = bu[:, s * LANE:(s + 1) * LANE]
            him_refs[q][pl.ds(s, tcs, stride=nsl), :] = bu[:, ns + s * LANE:ns + (s + 1) * LANE]

    h = [(cr_ref[q], ci_ref[q]) for q in range(nq)]
    lam = [(lam_ref[q, 0], lam_ref[q, 1]) for q in range(nq)]
    for step in range(tcs):
        off = (tcs - 1 - step if reverse else step) * nsl
        for q in range(nq):
            (hr, hi), (lr, li) = h[q], lam[q]
            nr = lr * hr - li * hi + hre_refs[q][pl.ds(off, nsl), :]
            ni = lr * hi + li * hr + him_refs[q][pl.ds(off, nsl), :]
            hre_refs[q][pl.ds(off, nsl), :] = nr
            him_refs[q][pl.ds(off, nsl), :] = ni
            h[q] = (nr, ni)
    for q in range(nq):
        cr_ref[q], ci_ref[q] = h[q]

    for q in range(nq):
        for s in range(nsl):
            h2d_refs[q][:, s * LANE:(s + 1) * LANE] = hre_refs[q][pl.ds(s, tcs, stride=nsl), :].astype(BF16)
            h2d_refs[q][:, ns + s * LANE:ns + (s + 1) * LANE] = \
                him_refs[q][pl.ds(s, tcs, stride=nsl), :].astype(BF16)
        sl = slice(q * S5_CH, (q + 1) * S5_CH)
        y = jnp.dot(h2d_refs[q][...], cm_ref[q], preferred_element_type=F32)
        if finish:
            y = y + yf_ref[:, sl] + d_ref[:, sl] * u_of(q)
            c = math.sqrt(2.0 / math.pi)
            y = 0.5 * y * (1.0 + jnp.tanh(c * (y + 0.044715 * (y * y * y))))
        y_ref[:, sl] = y.astype(y_ref.dtype)


def _s5_scan(x, mod, bm, cm, lam, y_fwd, dvec, dims):
    reverse = y_fwd is not None
    rows, d = x.shape
    tcs = dims["tcs"]
    ns2 = bm.shape[2]
    ns = ns2 // 2
    nq = _pick(d // S5_CH, (4, 2, 1))
    cbw = nq * S5_CH
    n_lat_blk = dims["L"] // tcs
    n_ctx_blk = dims["Lc"] // tcs
    n_lat_rows_blk = dims["n_lat"] // tcs
    nchunk = n_ctx_blk + n_lat_blk

    def rmap(b, c):
        if reverse:
            ctx_blk = n_lat_rows_blk + b * n_ctx_blk + (n_ctx_blk - 1 - c)
            lat_blk = b * n_lat_blk + (n_lat_blk - 1 - (c - n_ctx_blk))
        else:
            ctx_blk = n_lat_rows_blk + b * n_ctx_blk + c
            lat_blk = b * n_lat_blk + (c - n_ctx_blk)
        return jnp.where(c < n_ctx_blk, ctx_blk, lat_blk)

    bsz = dims["B"]
    tile = pl.BlockSpec((tcs, cbw), lambda b, g, c: (rmap(b, c), g))
    in_specs = [tile,
                pl.BlockSpec((1, 6, cbw), lambda b, g, c: (jnp.where(c < n_ctx_blk, bsz, b), 0, g)),
                pl.BlockSpec((nq, S5_CH, ns2), lambda b, g, c: (g, 0, 0)),
                pl.BlockSpec((nq, ns2, S5_CH), lambda b, g, c: (g, 0, 0)),
                pl.BlockSpec((nq, 2, ns // LANE, LANE), lambda b, g, c: (g, 0, 0, 0))]
    args = [x, mod, bm, cm, lam]
    if reverse:
        in_specs += [tile, pl.BlockSpec((1, cbw), lambda b, g, c: (0, g))]
        args += [y_fwd, dvec.reshape(1, -1).astype(F32)]
    return pl.pallas_call(
        functools.partial(_s5_kernel, tcs=tcs, nq=nq, ns=ns, finish=reverse),
        grid=(bsz, d // cbw, nchunk),
        in_specs=in_specs,
        out_specs=tile,
        out_shape=jax.ShapeDtypeStruct((rows, d), BF16 if reverse else F32),
        scratch_shapes=([pltpu.VMEM((tcs * ns // LANE, LANE), F32)] * (2 * nq)
                        + [pltpu.VMEM((tcs, ns2), BF16)] * nq
                        + [pltpu.VMEM((nq, ns // LANE, LANE), F32)] * 2),
        compiler_params=_cparams("parallel", "parallel", "arbitrary"),
        name="s5_scan_rev" if reverse else "s5_scan_fwd",
    )(*args)


def _s5_params(lam_re, lam_im, log_dt, b_re, b_im, c_re, c_im, gpb):
    lam = lax.complex(lam_re.astype(F32), lam_im.astype(F32))
    dt = jnp.exp(log_dt.astype(F32))[..., None]
    lam_bar = jnp.exp(lam * dt)
    b_bar = ((lam_bar - 1.0) / lam)[..., None] * lax.complex(b_re.astype(F32), b_im.astype(F32))
    c_mat = lax.complex(c_re.astype(F32), c_im.astype(F32))
    ndir, g, p = lam_bar.shape
    i_sz = b_bar.shape[-1]
    nblk = g // gpb
    eye = jnp.eye(gpb, dtype=F32)

    def blockdiag_in(m):
        m = m.reshape(ndir, nblk, gpb, p, i_sz)
        out = jnp.einsum("dngpi,gh->dngihp", m, eye)
        return out.reshape(ndir, nblk, gpb * i_sz, gpb * p)

    def blockdiag_out(m):
        m = m.reshape(ndir, nblk, gpb, i_sz, p)
        out = jnp.einsum("dngip,gh->dngphi", m, eye)
        return out.reshape(ndir, nblk, gpb * p, gpb * i_sz)

    bm = jnp.concatenate([blockdiag_in(jnp.real(b_bar)), blockdiag_in(jnp.imag(b_bar))], axis=-1).astype(BF16)
    cm = jnp.concatenate([blockdiag_out(jnp.real(c_mat)), blockdiag_out(-jnp.imag(c_mat))], axis=-2).astype(BF16)
    lam_t = jnp.stack([jnp.real(lam_bar), jnp.imag(lam_bar)], axis=1)
    lam_t = lam_t.reshape(ndir, 2, nblk, gpb * p // LANE, LANE).transpose(0, 2, 1, 3, 4).astype(F32)
    return bm, cm, lam_t


def kernel(x, c, ctx, c_ctx, ada_down, ada_up, ada_bias, ln_g, ln_b, rw_mix, rw_wr, rw_wk, rw_wv, rw_wo, rw_w0, rw_w1, rw_w2, rw_a0, rw_a1, rw_a2, rw_v0, rw_v1, rw_v2, rw_g1, rw_g2, rw_kk, rw_ka, rw_rk, rw_lnx_g, rw_lnx_b, s5_lam_re, s5_lam_im, s5_log_dt, s5_b_re, s5_b_im, s5_c_re, s5_c_im, s5_d, s5_glu_w, s5_glu_b, mlp_w1, mlp_w2):
    bsz, seq, d = x.shape
    lc = ctx.shape[1]
    depth = ada_down.shape[0]
    assert d % S5_CH == 0 and seq % CHUNK == 0 and lc % CHUNK == 0
    alpha = (2 * depth) ** 0.25
    n_lat = bsz * seq
    tr = _pick(math.gcd(seq, bsz * lc), (512, 256, 128, 64))
    tiles_per_batch = seq // tr

    def gmap(i):
        return jnp.minimum(i // tiles_per_batch, bsz)

    g_t = math.gcd(seq, lc)
    dims = {"B": bsz, "L": seq, "Lc": lc, "n_lat": n_lat, "tr": tr, "gmap": gmap,
            "tcs": _pick(g_t, (256, 128, 64, 32, 16, 8))}

    xs = jnp.concatenate([x.reshape(n_lat, d), ctx.reshape(bsz * lc, d)], axis=0).astype(F32)
    vf = None

    cvec = jnp.concatenate([c, c_ctx[None, :]], axis=0)
    cvec = _pad_to(jax.nn.silu(cvec), 0, 16).astype(BF16)
    wr_b, wk_b, wv_b, wo_b = (w.astype(BF16) for w in (rw_wr, rw_wk, rw_wv, rw_wo))
    w1_b = w2_b = None
    glu_bs = [None] * s5_glu_w.shape[0]

    for i in range(depth):
        low = _mm(cvec, ada_down[i].astype(BF16), out_dtype=BF16, name="adaln_down")
        mod = _mm(low, ada_up[i].astype(BF16), bias=ada_bias[i], name="adaln_up")
        mod = mod[:bsz + 1].reshape(bsz + 1, 6, d)
        j = i // 2
        if i % 2 == 0:
            p = {"w2": _pad_to(rw_w2[j], 1, LANE).astype(BF16), "a2": _pad_to(rw_a2[j], 1, LANE).astype(BF16),
                 "w0": rw_w0[j], "a0": rw_a0[j], "k_k": rw_kk[j], "k_a": rw_ka[j],
                 "r_k": rw_rk[j].reshape(-1)}
            lerps = _rwkv_prep(xs, mod, rw_mix[j], dims)
            if i == 0:
                r, w1_b = _mm(lerps[0], wr_b, layer=j, cast_next=(mlp_w1, 0), name="rwkv_r")
                k, w2_b = _mm(lerps[2], wk_b, layer=j, cast_next=(mlp_w2, 0), name="rwkv_k")
            else:
                r = _mm(lerps[0], wr_b, layer=j, name="rwkv_r")
                k = _mm(lerps[2], wk_b, layer=j, name="rwkv_k")
            if i == 0 and glu_bs:
                v, glu_bs[0] = _mm(lerps[3], wv_b, layer=j, cast_next=(s5_glu_w, 0), name="rwkv_v")
            else:
                v = _mm(lerps[3], wv_b, layer=j, name="rwkv_v")
            w1cat = jnp.concatenate([_pad_to(rw_w1[j, dd], 1, LANE) for dd in range(2)], axis=1)
            a1cat = jnp.concatenate([_pad_to(rw_a1[j, dd], 1, LANE) for dd in range(2)], axis=1)
            tw = _mm(lerps[1], w1cat.astype(BF16), act="tanh", out_dtype=BF16, name="rwkv_w1")
            ax = _mm(lerps[4], a1cat.astype(BF16), out_dtype=BF16, name="rwkv_a1")
            sg = _mm(lerps[5], _pad_to(rw_g1[j], 1, LANE).astype(BF16), act="sigmoid", out_dtype=BF16,
                     name="rwkv_g1")
            g2 = _pad_to(rw_g2[j], 0, LANE).astype(BF16)
            xv = None
            if j > 0:
                xv = _mm(lerps[3], _pad_to(rw_v1[j - 1], 1, LANE).astype(BF16), out_dtype=BF16, name="rwkv_v1")
                p["v2"] = _pad_to(rw_v2[j - 1], 0, LANE).astype(BF16)
                p["v0"] = rw_v0[j - 1]
            feats = _rwkv_features(r, k, v, tw, ax, p, vf if j > 0 else None, xv)
            kk, lw0, lw1, kd0, kd1, nb0, nb1, bonus = feats[:8]
            if j > 0:
                v = feats[8]
            else:
                vf = v
            y = _rwkv_scan(r, lw0, kd0, kk, nb0, v, None, False, dims)
            y = _rwkv_scan(r, lw1, kd1, kk, nb1, v, y, True, dims)
            yg = _rwkv_readout(y, bonus, sg, g2, rw_lnx_g[j], rw_lnx_b[j])
            if i == 0 and len(glu_bs) > 1:
                mix_out, glu_bs[1] = _mm(yg, wo_b, layer=j, cast_next=(s5_glu_w, 1), name="rwkv_o")
            else:
                mix_out = _mm(yg, wo_b, layer=j, name="rwkv_o")
        else:
            gpb = S5_CH // S5_GROUP
            bm, cm, lam_t = _s5_params(s5_lam_re[j], s5_lam_im[j], s5_log_dt[j], s5_b_re[j], s5_b_im[j],
                                       s5_c_re[j], s5_c_im[j], gpb)
            yf = _s5_scan(xs, mod, bm[0], cm[0], lam_t[0], None, None, dims)
            gl = _s5_scan(xs, mod, bm[1], cm[1], lam_t[1], yf, s5_d[j], dims)
            glu_w = glu_bs[j] if glu_bs[j] is not None else s5_glu_w[j].astype(BF16)
            mix_out = _mm(gl, glu_w, bias=s5_glu_b[j], glu=True,
                          tm=_pick(gl.shape[0], (768, 512, 256, 128, 64, 32, 16, 8)), name="s5_glu")
        xs, h2 = _resid_ln(xs, mix_out, mod, ln_g[i, 0], ln_b[i, 0], alpha, 2, (3, 4), dims)
        more = i + 1 < depth
        res = _mm(h2, w1_b, act="relu2", out_dtype=BF16, cast_next=(mlp_w1, i + 1) if more else None,
                  name="mlp_w1")
        a1, w1_next = res if more else (res, None)
        res = _mm(a1, w2_b, tm=_pick(h2.shape[0], (768, 512, 256, 128, 64, 32, 16, 8)),
                  tn=_pick(d, (1024, 512, 256, 128)), tk=_pick(w2_b.shape[0], (4096, 2048, 1024, 512, 256, 128)),
                  cast_next=(mlp_w2, i + 1) if more else None, name="mlp_w2")
        mlp_out, w2_next = res if more else (res, None)
        w1_b, w2_b = w1_next, w2_next
        (xs,) = _resid_ln(xs, mlp_out, mod, ln_g[i, 1], ln_b[i, 1], alpha, 5, None, dims,
                          rows=n_lat if i == depth - 1 else None)
    return xs.reshape(bsz, seq, d).astype(x.dtype)
```
